```python
import jax, jax.numpy as jnp
from jax import lax
import numpy as np

D_MODEL = 2048
BATCH = 1
SEQ = 8192
DEPTH = 1
DEC_BATCH = 8
DEC_SEQ = 64
PAST_LEN = 1024

CHUNK = 64
MLA_HEADS = 8
Q_RANK = 512
KV_RANK = 256
NOPE_DIM = 128
ROPE_DIM = 64
V_DIM = 128
ROPE_THETA = 10000.0
Q_BLOCK = 128
MLA_SCALE = (NOPE_DIM + ROPE_DIM) ** -0.5
GLA_HEADS = 4
GLA_DK = 128
GLA_DV = 256
GATE_RANK = 16
GATE_NORMALIZER = 16.0
PEER_HEADS = 8
N_KEYS = 128
N_EXPERTS = N_KEYS * N_KEYS
PEER_QDIM = 256
PEER_HALF = PEER_QDIM // 2
PEER_TOPK = 16
PEER_BLOCK = 128
MLA_WIDTH = MLA_HEADS * V_DIM
GLA_WIDTH = GLA_HEADS * GLA_DV
MIX_WIDTH = MLA_WIDTH + GLA_WIDTH
SPLITS = (Q_RANK, KV_RANK, ROPE_DIM, GLA_HEADS * GLA_DK, GLA_HEADS * GLA_DK, GLA_WIDTH, GLA_WIDTH, GATE_RANK)
IN_WIDTH = Q_RANK + KV_RANK + ROPE_DIM + 2 * GLA_HEADS * GLA_DK + 2 * GLA_WIDTH + GATE_RANK
DN_ALPHA = (2.0 * DEPTH) ** 0.25
DN_BETA = (8.0 * DEPTH) ** -0.25
LN_EPS = 1e-5
RMS_EPS = 1e-6

kernel_name = 'mla_gla_peer_stream_step'


def _split_points():
    pts, acc = [], 0
    for s in SPLITS[:-1]:
        acc += s
        pts.append(acc)
    return pts


def rms_norm(x, g):
    xf = x.astype(jnp.float32)
    y = xf * lax.rsqrt(jnp.mean(xf * xf, axis=-1, keepdims=True) + RMS_EPS)
    return (y * g.astype(jnp.float32)).astype(x.dtype)


def layer_norm(x, g, b):
    xf = x.astype(jnp.float32)
    mu = jnp.mean(xf, axis=-1, keepdims=True)
    xc = xf - mu
    var = jnp.mean(xc * xc, axis=-1, keepdims=True)
    y = xc * lax.rsqrt(var + LN_EPS) * g.astype(jnp.float32) + b.astype(jnp.float32)
    return y.astype(x.dtype)


def rope(x, pos):
    half = ROPE_DIM // 2
    freqs = ROPE_THETA ** (-jnp.arange(half, dtype=jnp.float32) / half)
    ang = pos.astype(jnp.float32)[:, None] * freqs[None, :]
    shape = (1, ang.shape[0]) + (1,) * (x.ndim - 3) + (half,)
    cos = jnp.cos(ang).reshape(shape)
    sin = jnp.sin(ang).reshape(shape)
    xf = x.astype(jnp.float32)
    x1, x2 = xf[..., :half], xf[..., half:]
    return jnp.concatenate([x1 * cos - x2 * sin, x1 * sin + x2 * cos], axis=-1).astype(x.dtype)


def mixer_inputs(x, pos, w_in, q_norm_g, w_uq, kv_norm_g, w_gk2, b_gk):
    B, S, _ = x.shape
    z = x @ w_in
    cq, ckv, kr, gq, gk, gv, gate, ga = jnp.split(z, _split_points(), axis=-1)
    q = (rms_norm(cq, q_norm_g) @ w_uq).reshape(B, S, MLA_HEADS, NOPE_DIM + ROPE_DIM)
    q_nope = q[..., :NOPE_DIM]
    q_rope = rope(q[..., NOPE_DIM:], pos)
    c_kv = rms_norm(ckv, kv_norm_g)
    k_rope = rope(kr, pos)
    log_f = jax.nn.log_sigmoid((ga @ w_gk2 + b_gk).astype(jnp.float32)) / GATE_NORMALIZER
    gla = (gq.reshape(B, S, GLA_HEADS, GLA_DK), gk.reshape(B, S, GLA_HEADS, GLA_DK),
           gv.reshape(B, S, GLA_HEADS, GLA_DV), log_f.reshape(B, S, GLA_HEADS, GLA_DK), gate)
    return q_nope, q_rope, c_kv, k_rope, gla


def mla_expand(c_kv, w_ukv):
    B, L, _ = c_kv.shape
    kv = (c_kv @ w_ukv).reshape(B, L, MLA_HEADS, NOPE_DIM + V_DIM)
    return kv[..., :NOPE_DIM], kv[..., NOPE_DIM:]


def mla_prompt(q_nope, q_rope, k_nope, k_rope, v):
    B, S, H, _ = q_nope.shape
    nb = S // Q_BLOCK
    qn = q_nope.reshape(B, nb, Q_BLOCK, H, NOPE_DIM).transpose(1, 0, 2, 3, 4)
    qr = q_rope.reshape(B, nb, Q_BLOCK, H, ROPE_DIM).transpose(1, 0, 2, 3, 4)
    k_chunk = jnp.arange(S) // CHUNK

    def block(args):
        qn_b, qr_b, bi = args
        s = (jnp.einsum('bqhd,bkhd->bhqk', qn_b, k_nope)
             + jnp.einsum('bqhr,bkr->bhqk', qr_b, k_rope)).astype(jnp.float32) * MLA_SCALE
        q_chunk = (bi * Q_BLOCK + jnp.arange(Q_BLOCK)) // CHUNK
        mask = k_chunk[None, :] <= q_chunk[:, None]
        p = jax.nn.softmax(jnp.where(mask[None, None], s, -jnp.inf), axis=-1).astype(v.dtype)
        return jnp.einsum('bhqk,bkhd->bqhd', p, v)

    o = lax.map(block, (qn, qr, jnp.arange(nb)))
    return o.transpose(1, 0, 2, 3, 4).reshape(B, S, H * V_DIM)


def mla_sample(q_nope, q_rope, k_nope, k_rope, v):
    B, T, H, _ = q_nope.shape
    s = (jnp.einsum('bqhd,bkhd->bhqk', q_nope, k_nope)
         + jnp.einsum('bqhr,bkr->bhqk', q_rope, k_rope)).astype(jnp.float32) * MLA_SCALE
    p = jax.nn.softmax(s, axis=-1).astype(v.dtype)
    return jnp.einsum('bhqk,bkhd->bqhd', p, v).reshape(B, T, H * V_DIM)


def gla_chunk(S0, q, k, v, log_f):
    C = q.shape[1]
    qf = q.astype(jnp.float32) * GLA_DK ** -0.5
    kf = k.astype(jnp.float32)
    vf = v.astype(jnp.float32)
    b = jnp.cumsum(log_f.astype(jnp.float32), axis=1)
    o_inter = jnp.einsum('bchk,bhkv->bchv', qf * jnp.exp(b), S0)
    causal = jnp.tril(jnp.ones((C, C), dtype=bool))
    diff = b[:, :, None] - b[:, None, :]
    decay = jnp.exp(jnp.where(causal[None, :, :, None, None], diff, -jnp.inf))
    A = jnp.einsum('bihk,bjhk,bijhk->bhij', qf, kf, decay)
    o_intra = jnp.einsum('bhij,bjhv->bihv', A, vf)
    b_last = b[:, -1]
    S1 = jnp.exp(b_last)[..., None] * S0 + jnp.einsum('bjhk,bjhv->bhkv', kf * jnp.exp(b_last[:, None] - b), vf)
    return o_inter + o_intra, S1


def gla_prompt(q, k, v, log_f):
    B, S, H, _ = q.shape
    nc = S // CHUNK

    def to_chunks(a):
        return a.reshape((B, nc, CHUNK) + a.shape[2:]).transpose((1, 0, 2) + tuple(range(3, a.ndim + 1)))

    def step(state, inp):
        qc, kc, vc, fc = inp
        o, state = gla_chunk(state, qc, kc, vc, fc)
        return state, o

    S0 = jnp.zeros((B, H, GLA_DK, GLA_DV), jnp.float32)
    S_end, o = lax.scan(step, S0, (to_chunks(q), to_chunks(k), to_chunks(v), to_chunks(log_f)))
    return o.transpose(1, 0, 2, 3, 4).reshape(B, S, H, GLA_DV), S_end


def gla_output(o, gate, gla_norm_g):
    B, S = o.shape[:2]
    on = rms_norm(o, gla_norm_g).reshape(B, S, GLA_WIDTH)
    return (on * jax.nn.silu(gate.astype(jnp.float32))).astype(gate.dtype)


def peer(h, peer_wq, peer_keys, peer_u, peer_v):
    B, S, D = h.shape
    T = B * S
    nb = -(-T // PEER_BLOCK)
    xp = jnp.pad(h.reshape(T, D), ((0, nb * PEER_BLOCK - T), (0, 0))).reshape(nb, PEER_BLOCK, D)

    def block(xb):
        q = (xb @ peer_wq).reshape(PEER_BLOCK, PEER_HEADS, 2, PEER_HALF)
        s = jnp.einsum('thpd,pkd->thpk', q, peer_keys).astype(jnp.float32)
        s1, i1 = lax.top_k(s[:, :, 0], PEER_TOPK)
        s2, i2 = lax.top_k(s[:, :, 1], PEER_TOPK)
        cand = (s1[..., :, None] + s2[..., None, :]).reshape(PEER_BLOCK, PEER_HEADS, PEER_TOPK * PEER_TOPK)
        sc, ci = lax.top_k(cand, PEER_TOPK)
        e_idx = (jnp.take_along_axis(i1, ci // PEER_TOPK, axis=-1) * N_KEYS
                 + jnp.take_along_axis(i2, ci % PEER_TOPK, axis=-1))
        g = jax.nn.softmax(sc, axis=-1)
        u = peer_u[e_idx]
        a = jax.nn.gelu(jnp.einsum('td,thkd->thk', xb, u).astype(jnp.float32), approximate=False)
        return jnp.einsum('thk,thkd->td', (g * a).astype(xb.dtype), peer_v[e_idx])

    out = lax.map(block, xp).reshape(nb * PEER_BLOCK, D)[:T]
    return out.reshape(B, S, D)


def encoder_layer(x, pos, past_latent, past_rope, state0, w_in, q_norm_g, w_uq, kv_norm_g, w_ukv,
                  w_gk2, b_gk, gla_norm_g, w_o, ln1_g, ln1_b, peer_wq, peer_keys, peer_u, peer_v,
                  ln2_g, ln2_b):
    q_nope, q_rope, c_kv, k_rope, (gq, gk, gv, log_f, gate) = mixer_inputs(
        x, pos, w_in, q_norm_g, w_uq, kv_norm_g, w_gk2, b_gk)
    if past_latent is None:
        k_nope, v = mla_expand(c_kv, w_ukv)
        mla_o = mla_prompt(q_nope, q_rope, k_nope, k_rope, v)
        gla_o, S_end = gla_prompt(gq, gk, gv, log_f)
    else:
        lat_all = jnp.concatenate([past_latent.astype(c_kv.dtype), c_kv], axis=1)
        rope_all = jnp.concatenate([past_rope.astype(k_rope.dtype), k_rope], axis=1)
        k_nope, v = mla_expand(lat_all, w_ukv)
        mla_o = mla_sample(q_nope, q_rope, k_nope, rope_all, v)
        gla_o, S_end = gla_chunk(state0.astype(jnp.float32), gq, gk, gv, log_f)
    mix = jnp.concatenate([mla_o.astype(x.dtype), gla_output(gla_o, gate, gla_norm_g)], axis=-1) @ w_o
    h = layer_norm(DN_ALPHA * x + mix, ln1_g, ln1_b)
    y = layer_norm(DN_ALPHA * h + peer(h, peer_wq, peer_keys, peer_u, peer_v), ln2_g, ln2_b)
    return y, c_kv, k_rope, S_end.astype(x.dtype)


def setup_inputs(seed: int = 0) -> dict:
    key = jax.random.key(seed)
    ks = jax.random.split(key, 24)
    f32 = jnp.float32
    nrm = jax.random.normal

    def gain(k, n):
        return 1.0 + 0.01 * nrm(k, (n,), f32)

    return {
        'x_prompt': nrm(ks[0], (BATCH, SEQ, D_MODEL), f32),
        'x_sample': nrm(ks[1], (DEC_BATCH, DEC_SEQ, D_MODEL), f32),
        'cache_kv_latent': nrm(ks[2], (DEC_BATCH, PAST_LEN, KV_RANK), f32),
        'cache_k_rope': nrm(ks[3], (DEC_BATCH, PAST_LEN, ROPE_DIM), f32),
        'state_gla': 0.5 * nrm(ks[4], (DEC_BATCH, GLA_HEADS, GLA_DK, GLA_DV), f32),
        'w_in': nrm(ks[5], (D_MODEL, IN_WIDTH), f32) * D_MODEL ** -0.5,
        'q_norm_g': gain(ks[6], Q_RANK),
        'w_uq': nrm(ks[7], (Q_RANK, MLA_HEADS * (NOPE_DIM + ROPE_DIM)), f32) * Q_RANK ** -0.5,
        'kv_norm_g': gain(ks[8], KV_RANK),
        'w_ukv': nrm(ks[9], (KV_RANK, MLA_HEADS * (NOPE_DIM + V_DIM)), f32) * KV_RANK ** -0.5,
        'w_gk2': nrm(ks[10], (GATE_RANK, GLA_HEADS * GLA_DK), f32) * GATE_RANK ** -0.5,
        'b_gk': 0.01 * nrm(ks[11], (GLA_HEADS * GLA_DK,), f32),
        'gla_norm_g': gain(ks[12], GLA_DV),
        'w_o': nrm(ks[13], (MIX_WIDTH, D_MODEL), f32) * (MIX_WIDTH ** -0.5 * DN_BETA),
        'ln1_g': gain(ks[14], D_MODEL),
        'ln1_b': 0.01 * nrm(ks[15], (D_MODEL,), f32),
        'peer_wq': nrm(ks[16], (D_MODEL, PEER_HEADS * PEER_QDIM), f32) * D_MODEL ** -0.5,
        'peer_keys': nrm(ks[17], (2, N_KEYS, PEER_HALF), f32) * PEER_HALF ** -0.5,
        'peer_u': nrm(ks[18], (N_EXPERTS, D_MODEL), f32) * D_MODEL ** -0.5,
        'peer_v': nrm(ks[19], (N_EXPERTS, D_MODEL), f32) * DN_BETA,
        'ln2_g': gain(ks[20], D_MODEL),
        'ln2_b': 0.01 * nrm(ks[21], (D_MODEL,), f32),
    }


def reference(x_prompt, x_sample, cache_kv_latent, cache_k_rope, state_gla, w_in, q_norm_g, w_uq,
              kv_norm_g, w_ukv, w_gk2, b_gk, gla_norm_g, w_o, ln1_g, ln1_b, peer_wq, peer_keys,
              peer_u, peer_v, ln2_g, ln2_b):
    past = cache_kv_latent.shape[1]
    pos_p = jnp.arange(x_prompt.shape[1], dtype=jnp.int32)
    pos_s = past + jnp.arange(x_sample.shape[1], dtype=jnp.int32)
    y_prompt, lat_p, rope_p, st_p = encoder_layer(
        x_prompt, pos_p, None, None, None, w_in, q_norm_g, w_uq, kv_norm_g, w_ukv, w_gk2, b_gk,
        gla_norm_g, w_o, ln1_g, ln1_b, peer_wq, peer_keys, peer_u, peer_v, ln2_g, ln2_b)
    y_sample, lat_s, rope_s, st_s = encoder_layer(
        x_sample, pos_s, cache_kv_latent, cache_k_rope, state_gla, w_in, q_norm_g, w_uq, kv_norm_g,
        w_ukv, w_gk2, b_gk, gla_norm_g, w_o, ln1_g, ln1_b, peer_wq, peer_keys, peer_u, peer_v,
        ln2_g, ln2_b)
    return (y_prompt, y_sample, lat_p, rope_p, st_p, lat_s, rope_s, st_s)
```

```python
import functools
import math

import jax
import jax.numpy as jnp
from jax import lax
from jax.experimental import pallas as pl
from jax.experimental.pallas import tpu as pltpu

F32 = jnp.float32
BF16 = jnp.bfloat16

LANES = 128
SUBLANES = 8
VMEM_LIMIT_BYTES = 56 * 1024 * 1024

D_MODEL = 2048
CHUNK = 64
MLA_HEADS = 8
Q_RANK = 512
KV_RANK = 256
NOPE_DIM = 128
ROPE_DIM = 64
V_DIM = 128
ROPE_THETA = 10000.0
MLA_SCALE = (NOPE_DIM + ROPE_DIM) ** -0.5
QK_PAD = 256
GLA_HEADS = 4
GLA_DK = 128
GLA_DV = 256
GATE_RANK = 16
GATE_NORMALIZER = 16.0
GLA_KW = GLA_HEADS * GLA_DK
GLA_WIDTH = GLA_HEADS * GLA_DV
MLA_WIDTH = MLA_HEADS * V_DIM
DN_ALPHA = 2.0 ** 0.25
LN_EPS = 1e-5
RMS_EPS = 1e-6

Z_CQ = 0
Z_CKV = Z_CQ + Q_RANK
Z_KR = Z_CKV + KV_RANK
Z_KROT = Z_KR + LANES
Z_GQ = Z_KROT + LANES
Z_GK = Z_GQ + GLA_KW
Z_GV = Z_GK + GLA_KW
Z_GATE = Z_GV + GLA_WIDTH
Z_WIDTH = Z_GATE + GLA_WIDTH


def _const_spec(shape):
    nd = len(shape)
    return pl.BlockSpec(shape, lambda *_: (0,) * nd, pipeline_mode=pl.Buffered(1))


def _rms(x, g):
    return x * lax.rsqrt(jnp.mean(x * x, axis=-1, keepdims=True) + RMS_EPS) * g


MIX_TM = 256


def _mixer_kernel(x_ref, w_in_ref, wq_ref, wukv_ref, wgk_ref, qg_ref, kvg_ref, bgk_ref, cos_ref, sin_ref,
                  q_ref, k_ref, v_ref, ckv_ref, kr_ref, gq_ref, gk_ref, gv_ref, gate_ref, lf_ref):
    xb = x_ref[...].astype(BF16)
    cos = cos_ref[...]
    sin = sin_ref[...]

    z_lat = jnp.dot(xb, w_in_ref[:, Z_CQ:Z_GQ], preferred_element_type=F32)
    cq = _rms(z_lat[:, Z_CQ:Z_CKV], qg_ref[...])
    c_kv = _rms(z_lat[:, Z_CKV:Z_KR], kvg_ref[...])
    krga = z_lat[:, Z_KR:Z_KROT]
    k_rope = krga * cos + z_lat[:, Z_KROT:Z_GQ] * sin
    ckv_ref[...] = c_kv
    kr_ref[...] = k_rope
    k_rope_b = k_rope.astype(BF16)

    q = jnp.dot(cq.astype(BF16), wq_ref[...], preferred_element_type=F32)
    kv = jnp.dot(c_kv.astype(BF16), wukv_ref[...], preferred_element_type=F32)
    for h in range(MLA_HEADS):
        lo = h * QK_PAD
        q_ref[:, lo:lo + NOPE_DIM] = (q[:, lo:lo + NOPE_DIM] * MLA_SCALE).astype(BF16)
        rot = q[:, MLA_HEADS * QK_PAD + h * LANES:MLA_HEADS * QK_PAD + (h + 1) * LANES]
        q_ref[:, lo + NOPE_DIM:lo + QK_PAD] = (
            (q[:, lo + NOPE_DIM:lo + QK_PAD] * cos + rot * sin) * MLA_SCALE).astype(BF16)
        k_ref[:, lo:lo + NOPE_DIM] = kv[:, h * NOPE_DIM:(h + 1) * NOPE_DIM].astype(BF16)
        k_ref[:, lo + NOPE_DIM:lo + QK_PAD] = k_rope_b
    v_ref[...] = kv[:, MLA_HEADS * NOPE_DIM:].astype(BF16)

    pre = jnp.dot(krga.astype(BF16), wgk_ref[...], preferred_element_type=F32) + bgk_ref[...]
    lf_ref[...] = (jnp.minimum(pre, 0.0) - jnp.log1p(jnp.exp(-jnp.abs(pre)))) * (1.0 / GATE_NORMALIZER)

    gq_ref[...] = jnp.dot(xb, w_in_ref[:, Z_GQ:Z_GK], preferred_element_type=F32)
    gk_ref[...] = jnp.dot(xb, w_in_ref[:, Z_GK:Z_GV], preferred_element_type=F32)
    gv_ref[...] = jnp.dot(xb, w_in_ref[:, Z_GV:Z_GATE], preferred_element_type=F32)
    gate_ref[...] = jnp.dot(xb, w_in_ref[:, Z_GATE:Z_WIDTH], preferred_element_type=F32)


def _rotate_half_cols(w):
    half = ROPE_DIM // 2
    return jnp.concatenate([-w[..., half:], w[..., :half]], axis=-1)


def _mixer_weights(w_in, w_uq, w_ukv, w_gk2):
    pts = []
    acc = 0
    for s in (Q_RANK, KV_RANK, ROPE_DIM, GLA_KW, GLA_KW, GLA_WIDTH, GLA_WIDTH):
        acc += s
        pts.append(acc)
    cq, ckv, kr, gq, gk, gv, gate, ga = jnp.split(w_in, pts, axis=1)
    d = w_in.shape[0]
    w_in_r = jnp.concatenate(
        [cq, ckv, kr, ga, jnp.zeros((d, LANES - ROPE_DIM - GATE_RANK), F32),
         _rotate_half_cols(kr), jnp.zeros((d, LANES - ROPE_DIM), F32), gq, gk, gv, gate], axis=1).astype(BF16)

    wq = w_uq.reshape(Q_RANK, MLA_HEADS, NOPE_DIM + ROPE_DIM)
    wq_rope = wq[..., NOPE_DIM:]
    pad = jnp.zeros((Q_RANK, MLA_HEADS, QK_PAD - NOPE_DIM - ROPE_DIM), F32)
    wq_main = jnp.concatenate([wq, pad], axis=-1).reshape(Q_RANK, MLA_HEADS * QK_PAD)
    wq_rot = jnp.concatenate([_rotate_half_cols(wq_rope), pad], axis=-1).reshape(Q_RANK, MLA_HEADS * LANES)
    wq_r = jnp.concatenate([wq_main, wq_rot], axis=1).astype(BF16)

    wkv = w_ukv.reshape(KV_RANK, MLA_HEADS, NOPE_DIM + V_DIM)
    wukv_r = jnp.concatenate([wkv[..., :NOPE_DIM].reshape(KV_RANK, -1),
                              wkv[..., NOPE_DIM:].reshape(KV_RANK, -1)], axis=1).astype(BF16)

    wgk_r = jnp.zeros((LANES, GLA_KW), F32).at[ROPE_DIM:ROPE_DIM + GATE_RANK].set(w_gk2).astype(BF16)
    return w_in_r, wq_r, wukv_r, wgk_r


def _rope_tables(pos):
    half = ROPE_DIM // 2
    freqs = ROPE_THETA ** (-jnp.arange(half, dtype=F32) / half)
    ang = pos.astype(F32)[:, None] * freqs[None, :]
    zeros = jnp.zeros((pos.shape[0], LANES - ROPE_DIM), F32)
    cos = jnp.concatenate([jnp.cos(ang), jnp.cos(ang), zeros], axis=1)
    sin = jnp.concatenate([jnp.sin(ang), jnp.sin(ang), zeros], axis=1)
    return cos, sin


def _mixer(x, cos, sin, w_in_r, wq_r, wukv_r, wgk_r, q_norm_g, kv_norm_g, b_gk):
    t = x.shape[0]
    tm = MIX_TM
    assert t % tm == 0
    row = lambda w: pl.BlockSpec((tm, w), lambda i: (i, 0))
    out_widths = (MLA_HEADS * QK_PAD, MLA_HEADS * QK_PAD, MLA_WIDTH, KV_RANK, LANES,
                  GLA_KW, GLA_KW, GLA_WIDTH, GLA_WIDTH, GLA_KW)
    out_dtypes = (BF16, BF16, BF16, F32, F32, F32, F32, F32, F32, F32)
    return pl.pallas_call(
        _mixer_kernel,
        grid=(t // tm,),
        in_specs=[row(D_MODEL), _const_spec(w_in_r.shape), _const_spec(wq_r.shape), _const_spec(wukv_r.shape),
                  _const_spec(wgk_r.shape), _const_spec((1, Q_RANK)), _const_spec((1, KV_RANK)),
                  _const_spec((1, GLA_KW)), row(LANES), row(LANES)],
        out_specs=[row(w) for w in out_widths],
        out_shape=[jax.ShapeDtypeStruct((t, w), dt) for w, dt in zip(out_widths, out_dtypes)],
        compiler_params=pltpu.CompilerParams(dimension_semantics=("arbitrary",),
                                             vmem_limit_bytes=VMEM_LIMIT_BYTES),
        name="mixer",
    )(x, w_in_r, wq_r, wukv_r, wgk_r, q_norm_g.reshape(1, -1), kv_norm_g.reshape(1, -1),
      b_gk.reshape(1, -1), cos, sin)


MLA_TQ = 512
NEG_BIG = -1e30
_NT = (((1,), (1,)), ((), ()))


def _mla_prompt_kernel(qi_ref, ki_ref, q_ref, k_ref, v_ref, o_ref, m_ref, l_ref, acc_ref):
    step = pl.program_id(0)
    qi = qi_ref[step]
    ki = ki_ref[step]
    tq = q_ref.shape[0]
    tk = k_ref.shape[0]

    @pl.when(ki == 0)
    def _():
        m_ref[...] = jnp.full(m_ref.shape, NEG_BIG, F32)
        l_ref[...] = jnp.zeros(l_ref.shape, F32)
        acc_ref[...] = jnp.zeros(acc_ref.shape, F32)

    def update(masked):
        if masked:
            rows = lax.broadcasted_iota(jnp.int32, (tq, tk), 0) // CHUNK
            cols = lax.broadcasted_iota(jnp.int32, (tq, tk), 1) // CHUNK
            visible = cols <= rows
        for h in range(MLA_HEADS):
            s = lax.dot_general(q_ref[:, h * QK_PAD:(h + 1) * QK_PAD], k_ref[:, h * QK_PAD:(h + 1) * QK_PAD],
                                _NT, preferred_element_type=F32)
            if masked:
                s = jnp.where(visible, s, NEG_BIG)
            m_old = m_ref[h]
            m_new = jnp.maximum(m_old, jnp.max(s, axis=1, keepdims=True))
            p = jnp.exp(s - m_new)
            alpha = jnp.exp(m_old - m_new)
            l_ref[h] = alpha * l_ref[h] + jnp.sum(p, axis=1, keepdims=True)
            acc_ref[h] = alpha * acc_ref[h] + jnp.dot(p.astype(BF16), v_ref[:, h * V_DIM:(h + 1) * V_DIM],
                                                      preferred_element_type=F32)
            m_ref[h] = m_new

    @pl.when(ki < qi)
    def _():
        update(False)

    @pl.when(ki == qi)
    def _():
        update(True)
        for h in range(MLA_HEADS):
            o_ref[:, h * V_DIM:(h + 1) * V_DIM] = (acc_ref[h] / l_ref[h]).astype(BF16)


def _mla_prompt(q, k, v, n_tokens):
    tq = MLA_TQ
    assert n_tokens % tq == 0
    nq = n_tokens // tq
    pairs = [(a, b) for a in range(nq) for b in range(a + 1)]
    qi = jnp.asarray([p[0] for p in pairs], jnp.int32)
    ki = jnp.asarray([p[1] for p in pairs], jnp.int32)
    grid_spec = pltpu.PrefetchScalarGridSpec(
        num_scalar_prefetch=2,
        grid=(len(pairs),),
        in_specs=[pl.BlockSpec((tq, MLA_HEADS * QK_PAD), lambda s, qi, ki: (qi[s], 0)),
                  pl.BlockSpec((tq, MLA_HEADS * QK_PAD), lambda s, qi, ki: (ki[s], 0)),
                  pl.BlockSpec((tq, MLA_WIDTH), lambda s, qi, ki: (ki[s], 0))],
        out_specs=pl.BlockSpec((tq, MLA_WIDTH), lambda s, qi, ki: (qi[s], 0)),
        scratch_shapes=[pltpu.VMEM((MLA_HEADS, tq, 1), F32), pltpu.VMEM((MLA_HEADS, tq, 1), F32),
                        pltpu.VMEM((MLA_HEADS, tq, V_DIM), F32)],
    )
    return pl.pallas_call(
        _mla_prompt_kernel,
        grid_spec=grid_spec,
        out_shape=jax.ShapeDtypeStruct((n_tokens, MLA_WIDTH), BF16),
        compiler_params=pltpu.CompilerParams(dimension_semantics=("arbitrary",),
                                             vmem_limit_bytes=VMEM_LIMIT_BYTES),
        name="mla_prompt",
    )(qi, ki, q, k, v)


def _mla_sample_kernel(q_ref, kn_ref, vn_ref, lat_ref, krc_ref, wukv_ref, o_ref):
    kvc = jnp.dot(lat_ref[0].astype(BF16), wukv_ref[...], preferred_element_type=F32)
    krc = krc_ref[0]
    for h in range(MLA_HEADS):
        q = q_ref[:, h * QK_PAD:(h + 1) * QK_PAD]
        knc = kvc[:, h * NOPE_DIM:(h + 1) * NOPE_DIM].astype(BF16)
        vc = kvc[:, MLA_HEADS * NOPE_DIM + h * V_DIM:MLA_HEADS * NOPE_DIM + (h + 1) * V_DIM].astype(BF16)
        s_c = (lax.dot_general(q[:, :NOPE_DIM], knc, _NT, preferred_element_type=F32)
               + lax.dot_general(q[:, NOPE_DIM:], krc, _NT, preferred_element_type=F32))
        s_n = lax.dot_general(q, kn_ref[:, h * QK_PAD:(h + 1) * QK_PAD], _NT, preferred_element_type=F32)
        m = jnp.maximum(jnp.max(s_c, axis=1, keepdims=True), jnp.max(s_n, axis=1, keepdims=True))
        p_c = jnp.exp(s_c - m)
        p_n = jnp.exp(s_n - m)
        l = jnp.sum(p_c, axis=1, keepdims=True) + jnp.sum(p_n, axis=1, keepdims=True)
        o = (jnp.dot(p_c.astype(BF16), vc, preferred_element_type=F32)
             + jnp.dot(p_n.astype(BF16), vn_ref[:, h * V_DIM:(h + 1) * V_DIM], preferred_element_type=F32))
        o_ref[:, h * V_DIM:(h + 1) * V_DIM] = (o / l).astype(BF16)


def _mla_sample(q, k, v, cache_lat, cache_kr, wukv_r, row0, n_new):
    bs, past, _ = cache_lat.shape
    assert row0 % n_new == 0
    blk0 = row0 // n_new
    krc = jnp.pad(cache_kr, ((0, 0), (0, 0), (0, LANES - ROPE_DIM))).astype(BF16)
    new = lambda w: pl.BlockSpec((n_new, w), lambda b: (blk0 + b, 0))
    return pl.pallas_call(
        _mla_sample_kernel,
        grid=(bs,),
        in_specs=[new(MLA_HEADS * QK_PAD), new(MLA_HEADS * QK_PAD), new(MLA_WIDTH),
                  pl.BlockSpec((1, past, KV_RANK), lambda b: (b, 0, 0)),
                  pl.BlockSpec((1, past, LANES), lambda b: (b, 0, 0)),
                  _const_spec(wukv_r.shape)],
        out_specs=pl.BlockSpec((n_new, MLA_WIDTH), lambda b: (b, 0)),
        out_shape=jax.ShapeDtypeStruct((bs * n_new, MLA_WIDTH), BF16),
        compiler_params=pltpu.CompilerParams(dimension_semantics=("arbitrary",),
                                             vmem_limit_bytes=VMEM_LIMIT_BYTES),
        name="mla_sample",
    )(q, k, v, cache_lat, krc, wukv_r)


GLA_SUB = 16
_TN = (((0,), (0,)), ((), ()))


def _cumsum_rows(x):
    n = x.shape[0]
    row = lax.broadcasted_iota(jnp.int32, x.shape, 0)
    s = 1
    while s < n:
        x = x + jnp.where(row >= s, pltpu.roll(x, s, axis=0), 0.0)
        s *= 2
    return x


def _gla_kernel(q_ref, k_ref, v_ref, lf_ref, gate_ref, s0_ref, g_ref, o_ref, send_ref, st_ref):
    c = pl.program_id(1)
    n_rows = q_ref.shape[0]
    n_sub = n_rows // GLA_SUB

    @pl.when(c == 0)
    def _():
        for h in range(GLA_HEADS):
            st_ref[h] = s0_ref[0, h].T

    sub_row = lax.broadcasted_iota(jnp.int32, (GLA_SUB, GLA_DK), 0)
    for h in range(GLA_HEADS):
        ks = slice(h * GLA_DK, (h + 1) * GLA_DK)
        vs = slice(h * GLA_DV, (h + 1) * GLA_DV)
        q = q_ref[:, ks] * (GLA_DK ** -0.5)
        k = k_ref[:, ks]
        v = v_ref[:, vs]
        vb = v.astype(BF16)
        b = _cumsum_rows(lf_ref[:, ks])
        b_last = b[n_rows - 1:n_rows]
        st = st_ref[h]
        o_inter = lax.dot_general((q * jnp.exp(b)).astype(BF16), st.astype(BF16), _NT,
                                  preferred_element_type=F32)
        outs = []
        for i in range(n_sub):
            r0 = i * GLA_SUB
            bi = b[r0:r0 + GLA_SUB]
            qi = q[r0:r0 + GLA_SUB]
            o_i = o_inter[r0:r0 + GLA_SUB]
            if i > 0:
                b_ref = b[r0 - 1:r0]
                qh = (qi * jnp.exp(bi - b_ref)).astype(BF16)
                kh = (k[:r0] * jnp.exp(b_ref - b[:r0])).astype(BF16)
                a_off = lax.dot_general(qh, kh, _NT, preferred_element_type=F32)
                o_i = o_i + jnp.dot(a_off.astype(BF16), vb[:r0], preferred_element_type=F32)
            for j in range(GLA_SUB):
                r = r0 + j
                decay = jnp.exp(jnp.where(sub_row >= j, bi - b[r:r + 1], -jnp.inf))
                a_col = jnp.sum(qi * k[r:r + 1] * decay, axis=1, keepdims=True)
                o_i = o_i + a_col * v[r:r + 1]
            outs.append(o_i)
        o = jnp.concatenate(outs, axis=0)

        kd = (k * jnp.exp(b_last - b)).astype(BF16)
        st_new = st * jnp.exp(b_last) + lax.dot_general(vb, kd, _TN, preferred_element_type=F32)
        st_ref[h] = st_new

        gate = gate_ref[:, vs]
        on = _rms(o, g_ref[...])
        o_ref[:, vs] = (on * (gate / (1.0 + jnp.exp(-gate)))).astype(BF16)

    @pl.when(c == pl.num_programs(1) - 1)
    def _():
        for h in range(GLA_HEADS):
            send_ref[0, h] = st_ref[h].T


def _gla(gq, gk, gv, lf, gate, s0, gla_norm_g, row0, n_seq, n_chunks):
    assert row0 % CHUNK == 0
    blk0 = row0 // CHUNK
    row = lambda w: pl.BlockSpec((CHUNK, w), lambda b, c: (blk0 + b * n_chunks + c, 0))
    state = pl.BlockSpec((1, GLA_HEADS, GLA_DK, GLA_DV), lambda b, c: (b, 0, 0, 0))
    return pl.pallas_call(
        _gla_kernel,
        grid=(n_seq, n_chunks),
        in_specs=[row(GLA_KW), row(GLA_KW), row(GLA_WIDTH), row(GLA_KW), row(GLA_WIDTH), state,
                  pl.BlockSpec((1, GLA_DV), lambda b, c: (0, 0))],
        out_specs=[pl.BlockSpec((CHUNK, GLA_WIDTH), lambda b, c: (b * n_chunks + c, 0)), state],
        out_shape=[jax.ShapeDtypeStruct((n_seq * n_chunks * CHUNK, GLA_WIDTH), BF16),
                   jax.ShapeDtypeStruct((n_seq, GLA_HEADS, GLA_DK, GLA_DV), F32)],
        scratch_shapes=[pltpu.VMEM((GLA_HEADS, GLA_DV, GLA_DK), F32)],
        compiler_params=pltpu.CompilerParams(dimension_semantics=("arbitrary", "arbitrary"),
                                             vmem_limit_bytes=VMEM_LIMIT_BYTES),
        name="gla",
    )(gq, gk, gv, lf, gate, s0, gla_norm_g.reshape(1, -1))


OUT_TM = 256


def _layer_norm(y, g, b):
    mu = jnp.mean(y, axis=-1, keepdims=True)
    yc = y - mu
    var = jnp.mean(yc * yc, axis=-1, keepdims=True)
    return yc * lax.rsqrt(var + LN_EPS) * g + b


def _outproj_kernel(mla_ref, gla_ref, x_ref, wo_ref, g_ref, b_ref, wq_ref, h_ref, hb_ref, qp_ref):
    mix = (jnp.dot(mla_ref[...], wo_ref[:MLA_WIDTH], preferred_element_type=F32)
           + jnp.dot(gla_ref[...], wo_ref[MLA_WIDTH:], preferred_element_type=F32))
    h = _layer_norm(DN_ALPHA * x_ref[...] + mix, g_ref[...], b_ref[...])
    hb = h.astype(BF16)
    h_ref[...] = h
    hb_ref[...] = hb
    qp_ref[...] = jnp.dot(hb, wq_ref[...], preferred_element_type=F32).astype(BF16)


def _outproj(mla_o, gla_o, x, w_o_b, ln1_g, ln1_b, peer_wq_b):
    t = x.shape[0]
    tm = OUT_TM
    assert t % tm == 0
    row = lambda w: pl.BlockSpec((tm, w), lambda i: (i, 0))
    return pl.pallas_call(
        _outproj_kernel,
        grid=(t // tm,),
        in_specs=[row(MLA_WIDTH), row(GLA_WIDTH), row(D_MODEL), _const_spec(w_o_b.shape),
                  _const_spec((1, D_MODEL)), _const_spec((1, D_MODEL)), _const_spec(peer_wq_b.shape)],
        out_specs=[row(D_MODEL), row(D_MODEL), row(peer_wq_b.shape[1])],
        out_shape=[jax.ShapeDtypeStruct((t, D_MODEL), F32), jax.ShapeDtypeStruct((t, D_MODEL), BF16),
                   jax.ShapeDtypeStruct((t, peer_wq_b.shape[1]), BF16)],
        compiler_params=pltpu.CompilerParams(dimension_semantics=("arbitrary",),
                                             vmem_limit_bytes=VMEM_LIMIT_BYTES),
        name="outproj",
    )(mla_o, gla_o, x, w_o_b, ln1_g.reshape(1, -1), ln1_b.reshape(1, -1), peer_wq_b)


PEER_HEADS = 8
N_KEYS = 128
PEER_HALF = 128
PEER_TOPK = 16
ROUTE_TM = 256
CAND_WIDE_RANKS = 8
CAND_ROWS = PEER_TOPK + (CAND_WIDE_RANKS - 1) * SUBLANES + (PEER_TOPK - CAND_WIDE_RANKS)


def _extract_top(x, n_out, on_pick):
    n = x.shape[0]
    row = lax.broadcasted_iota(jnp.int32, x.shape, 0).astype(F32)
    for k in range(n_out):
        m = jnp.max(x, axis=0, keepdims=True)
        pick = jnp.min(jnp.where(x == m, row, float(n)), axis=0, keepdims=True)
        hit = row == pick
        on_pick(k, m, hit, pick)
        x = jnp.where(hit, -jnp.inf, x)


def _route_kernel(qp_ref, keys_ref, g_ref, i1_ref, i2_ref, sv_ref, si_ref, gt_ref, i1t_ref, i2t_ref):
    h = pl.program_id(1)
    tm = qp_ref.shape[0]

    for half in range(2):
        s = lax.dot_general(keys_ref[half], qp_ref[:, half * PEER_HALF:(half + 1) * PEER_HALF], _NT,
                            preferred_element_type=F32)

        def keep(k, m, hit, pick, half=half):
            sv_ref[half, k:k + 1, :] = m
            si_ref[half, k:k + 1, :] = pick

        _extract_top(s, PEER_TOPK, keep)

    s1, s2 = sv_ref[0], sv_ref[1]
    i1, i2 = si_ref[0], si_ref[1]
    bc = lambda r, n: jnp.broadcast_to(r, (n, tm))
    wide = range(1, CAND_WIDE_RANKS)
    cand = jnp.concatenate([bc(s1[0:1], PEER_TOPK) + s2]
                           + [bc(s1[a:a + 1], SUBLANES) + s2[:SUBLANES] for a in wide]
                           + [s1[CAND_WIDE_RANKS:] + bc(s2[0:1], PEER_TOPK - CAND_WIDE_RANKS)], axis=0)
    c_i1 = jnp.concatenate([bc(i1[0:1], PEER_TOPK)] + [bc(i1[a:a + 1], SUBLANES) for a in wide]
                           + [i1[CAND_WIDE_RANKS:]], axis=0)
    c_i2 = jnp.concatenate([i2] + [i2[:SUBLANES] for _ in wide]
                           + [bc(i2[0:1], PEER_TOPK - CAND_WIDE_RANKS)], axis=0)
    base = pl.multiple_of(h * PEER_TOPK, PEER_TOPK)

    def keep_pair(k, m, hit, pick):
        gt_ref[pl.ds(base + k, 1), :] = m
        i1t_ref[pl.ds(base + k, 1), :] = jnp.sum(jnp.where(hit, c_i1, 0.0), axis=0, keepdims=True)
        i2t_ref[pl.ds(base + k, 1), :] = jnp.sum(jnp.where(hit, c_i2, 0.0), axis=0, keepdims=True)

    _extract_top(cand, PEER_TOPK, keep_pair)

    sc = gt_ref[pl.ds(base, PEER_TOPK), :]
    e = jnp.exp(sc - sc[0:1])
    gt_ref[pl.ds(base, PEER_TOPK), :] = e / jnp.sum(e, axis=0, keepdims=True)

    @pl.when(h == pl.num_programs(1) - 1)
    def _():
        g_ref[...] = gt_ref[...].T
        i1_ref[...] = i1t_ref[...].T
        i2_ref[...] = i2t_ref[...].T


def _route(qp, keys_b):
    t = qp.shape[0]
    tm = ROUTE_TM
    assert t % tm == 0
    n_sel = PEER_HEADS * PEER_TOPK
    out = pl.BlockSpec((tm, n_sel), lambda i, h: (i, 0))
    return pl.pallas_call(
        _route_kernel,
        grid=(t // tm, PEER_HEADS),
        in_specs=[pl.BlockSpec((tm, 2 * PEER_HALF), lambda i, h: (i, h)),
                  pl.BlockSpec(keys_b.shape, lambda i, h: (0, 0, 0))],
        out_specs=[out, out, out],
        out_shape=[jax.ShapeDtypeStruct((t, n_sel), F32)] * 3,
        scratch_shapes=[pltpu.VMEM((2, PEER_TOPK, tm), F32), pltpu.VMEM((2, PEER_TOPK, tm), F32),
                        pltpu.VMEM((n_sel, tm), F32), pltpu.VMEM((n_sel, tm), F32), pltpu.VMEM((n_sel, tm), F32)],
        compiler_params=pltpu.CompilerParams(dimension_semantics=("arbitrary", "arbitrary"),
                                             vmem_limit_bytes=VMEM_LIMIT_BYTES),
        name="peer_route",
    )(qp, keys_b)


SCATTER_TB = 128
W_ROW_TILE = SUBLANES


def _scatter_kernel(g_ref, i1_ref, i2_ref, w_ref):
    tb = g_ref.shape[0]
    key = lax.broadcasted_iota(jnp.int32, (N_KEYS, g_ref.shape[1]), 0).astype(F32)

    def body(t, carry):
        g = jnp.broadcast_to(g_ref[pl.ds(t, 1), :], key.shape)
        i1 = jnp.broadcast_to(i1_ref[pl.ds(t, 1), :], key.shape)
        i2 = jnp.broadcast_to(i2_ref[pl.ds(t, 1), :], key.shape)
        a_t = jnp.where(key == i1, g, 0.0).astype(BF16)
        b_t = jnp.where(key == i2, 1.0, 0.0).astype(BF16)
        w = lax.dot_general(a_t, b_t, _NT, preferred_element_type=F32)
        w_ref[:, pl.ds(t, 1), :, :] = w.reshape(N_KEYS // W_ROW_TILE, 1, W_ROW_TILE, N_KEYS)
        return carry

    lax.fori_loop(0, tb, body, 0)


def _scatter(g, i1, i2):
    t = g.shape[0]
    tb = SCATTER_TB
    assert t % tb == 0
    sel = pl.BlockSpec((tb, g.shape[1]), lambda i: (i, 0))
    n_grp = N_KEYS // W_ROW_TILE
    return pl.pallas_call(
        _scatter_kernel,
        grid=(t // tb,),
        in_specs=[sel, sel, sel],
        out_specs=pl.BlockSpec((n_grp, tb, W_ROW_TILE, N_KEYS), lambda i: (0, i, 0, 0)),
        out_shape=jax.ShapeDtypeStruct((n_grp, t, W_ROW_TILE, N_KEYS), F32),
        compiler_params=pltpu.CompilerParams(dimension_semantics=("arbitrary",),
                                             vmem_limit_bytes=VMEM_LIMIT_BYTES),
        name="peer_scatter",
    )(g, i1, i2)


EXP_TM = 512
EXP_TE = W_ROW_TILE * N_KEYS
EXP_CHUNK = 2 * N_KEYS


def _gelu(x):
    return x * 0.5 * (1.0 + lax.erf(x * (2.0 ** -0.5)))


def _experts_kernel(hb_ref, w_ref, u_ref, v_ref, h_ref, g_ref, b_ref, y_ref, acc_ref):
    j = pl.program_id(1)
    tm = hb_ref.shape[0]

    @pl.when(j == 0)
    def _():
        acc_ref[...] = jnp.zeros(acc_ref.shape, F32)

    hb = hb_ref[...]
    ps = []
    for c in range(EXP_TE // EXP_CHUNK):
        a = lax.dot_general(hb, u_ref[c * EXP_CHUNK:(c + 1) * EXP_CHUNK, :], _NT, preferred_element_type=F32)
        rows = EXP_CHUNK // N_KEYS
        w = jnp.concatenate([w_ref[pl.ds(c * rows + r, tm, stride=W_ROW_TILE), :] for r in range(rows)], axis=1)
        ps.append((w * _gelu(a)).astype(BF16))
    p = jnp.concatenate(ps, axis=1)
    acc_ref[...] += jnp.dot(p, v_ref[...], preferred_element_type=F32)

    @pl.when(j == pl.num_programs(1) - 1)
    def _():
        y_ref[...] = _layer_norm(DN_ALPHA * h_ref[...] + acc_ref[...], g_ref[...], b_ref[...])


def _experts(hb, h, w_table, u_b, v_b, ln2_g, ln2_b):
    t = hb.shape[0]
    tm = EXP_TM
    assert t % tm == 0
    nt = t // tm
    ne = u_b.shape[0] // EXP_TE
    w2d = w_table.reshape(-1, N_KEYS)
    row = lambda: pl.BlockSpec((tm, D_MODEL), lambda i, j: (i, 0))
    return pl.pallas_call(
        _experts_kernel,
        grid=(nt, ne),
        in_specs=[row(), pl.BlockSpec((tm * W_ROW_TILE, N_KEYS), lambda i, j: (j * nt + i, 0)),
                  pl.BlockSpec((EXP_TE, D_MODEL), lambda i, j: (j, 0)),
                  pl.BlockSpec((EXP_TE, D_MODEL), lambda i, j: (j, 0)),
                  row(), pl.BlockSpec((1, D_MODEL), lambda i, j: (0, 0)),
                  pl.BlockSpec((1, D_MODEL), lambda i, j: (0, 0))],
        out_specs=row(),
        out_shape=jax.ShapeDtypeStruct((t, D_MODEL), F32),
        scratch_shapes=[pltpu.VMEM((tm, D_MODEL), F32)],
        compiler_params=pltpu.CompilerParams(dimension_semantics=("arbitrary", "arbitrary"),
                                             vmem_limit_bytes=VMEM_LIMIT_BYTES),
        name="peer_experts",
    )(hb, w2d, u_b, v_b, h, ln2_g.reshape(1, -1), ln2_b.reshape(1, -1))


def kernel(x_prompt, x_sample, cache_kv_latent, cache_k_rope, state_gla, w_in, q_norm_g, w_uq, kv_norm_g, w_ukv,
           w_gk2, b_gk, gla_norm_g, w_o, ln1_g, ln1_b, peer_wq, peer_keys, peer_u, peer_v, ln2_g, ln2_b):
    bp, sp, _ = x_prompt.shape
    bs, ss, _ = x_sample.shape
    past = cache_kv_latent.shape[1]
    tp, ts = bp * sp, bs * ss
    x = jnp.concatenate([x_prompt.reshape(tp, D_MODEL), x_sample.reshape(ts, D_MODEL)], axis=0)
    pos = jnp.concatenate([jnp.tile(jnp.arange(sp, dtype=jnp.int32), bp),
                           jnp.tile(past + jnp.arange(ss, dtype=jnp.int32), bs)])
    cos, sin = _rope_tables(pos)
    w_in_r, wq_r, wukv_r, wgk_r = _mixer_weights(w_in, w_uq, w_ukv, w_gk2)
    q, k, v, c_kv, kr, gq, gk, gv, gate, lf = _mixer(x, cos, sin, w_in_r, wq_r, wukv_r, wgk_r,
                                                     q_norm_g, kv_norm_g, b_gk)
    o_p = _mla_prompt(q, k, v, tp)
    o_s = _mla_sample(q, k, v, cache_kv_latent, cache_k_rope, wukv_r, tp, ss)
    g_p, st_p = _gla(gq, gk, gv, lf, gate, jnp.zeros((bp,) + state_gla.shape[1:], F32), gla_norm_g,
                     0, bp, sp // CHUNK)
    g_s, st_s = _gla(gq, gk, gv, lf, gate, state_gla, gla_norm_g, tp, bs, ss // CHUNK)

    h, hb, qp = _outproj(jnp.concatenate([o_p, o_s], axis=0), jnp.concatenate([g_p, g_s], axis=0), x,
                         w_o.astype(BF16), ln1_g, ln1_b, peer_wq.astype(BF16))
    gates, i1, i2 = _route(qp, peer_keys.astype(BF16))
    w_table = _scatter(gates, i1, i2)
    y = _experts(hb, h, w_table, peer_u.astype(BF16), peer_v.astype(BF16), ln2_g, ln2_b)

    dt = x_prompt.dtype
    return (y[:tp].reshape(bp, sp, D_MODEL), y[tp:].reshape(bs, ss, D_MODEL),
            c_kv[:tp].reshape(bp, sp, KV_RANK), kr[:tp, :ROPE_DIM].reshape(bp, sp, ROPE_DIM), st_p.astype(dt),
            c_kv[tp:].reshape(bs, ss, KV_RANK), kr[tp:, :ROPE_DIM].reshape(bs, ss, ROPE_DIM), st_s.astype(dt))
```

```python
import functools
import math

import jax
import jax.numpy as jnp
from jax import lax
from jax.experimental import pallas as pl
from jax.experimental.pallas import tpu as pltpu

F32 = jnp.float32
BF16 = jnp.bfloat16

LANES = 128
SUBLANES = 8
VMEM_LIMIT_BYTES = 60 * 1024 * 1024

D_MODEL = 2048
CHUNK = 64
MLA_HEADS = 8
Q_RANK = 512
KV_RANK = 256
NOPE_DIM = 128
ROPE_DIM = 64
V_DIM = 128
ROPE_THETA = 10000.0
MLA_SCALE = (NOPE_DIM + ROPE_DIM) ** -0.5
Q_SCALE = MLA_SCALE * math.log2(math.e)
QK_PAD = 256
GLA_HEADS = 4
GLA_DK = 128
GLA_DV = 256
GATE_RANK = 16
GATE_NORMALIZER = 16.0
GLA_KW = GLA_HEADS * GLA_DK
GLA_WIDTH = GLA_HEADS * GLA_DV
MLA_WIDTH = MLA_HEADS * V_DIM
DN_ALPHA = 2.0 ** 0.25
LN_EPS = 1e-5
RMS_EPS = 1e-6

Z_CQ = 0
Z_CKV = Z_CQ + Q_RANK
Z_KR = Z_CKV + KV_RANK
Z_KROT = Z_KR + LANES
Z_GQ = Z_KROT + LANES
Z_GK = Z_GQ + GLA_KW
Z_GV = Z_GK + GLA_KW
Z_GATE = Z_GV + GLA_WIDTH
Z_WIDTH = Z_GATE + GLA_WIDTH


def _const_spec(shape):
    nd = len(shape)
    return pl.BlockSpec(shape, lambda *_: (0,) * nd, pipeline_mode=pl.Buffered(1))


def _rms(x, g):
    return x * lax.rsqrt(jnp.mean(x * x, axis=-1, keepdims=True) + RMS_EPS) * g


MIX_TM = 256


def _two_stream_specs(tm, width, n_first):
    return (pl.BlockSpec((tm, width), lambda i: (jnp.minimum(i, n_first - 1), 0)),
            pl.BlockSpec((tm, width), lambda i: (jnp.maximum(i - n_first, 0), 0)))


def _mixer_kernel(xp_ref, xs_ref, w_in_ref, wq_ref, wukv_ref, wvt_ref, wgk_ref, qg_ref, kvg_ref, bgk_ref,
                  cos_ref, sin_ref, q_ref, k_ref, v_ref, vt_ref, ckvp_ref, ckvs_ref, krp_ref, krs_ref,
                  gq_ref, gk_ref, gv_ref, gate_ref, lf_ref, *, n_prompt_tiles):
    in_prompt = pl.program_id(0) < n_prompt_tiles
    xb = jnp.where(in_prompt, xp_ref[...], xs_ref[...]).astype(BF16)
    cos = cos_ref[...]
    sin = sin_ref[...]

    z_lat = jnp.dot(xb, w_in_ref[:, Z_CQ:Z_GQ], preferred_element_type=F32)
    cq = _rms(z_lat[:, Z_CQ:Z_CKV], qg_ref[...])
    c_kv = _rms(z_lat[:, Z_CKV:Z_KR], kvg_ref[...])
    krga = z_lat[:, Z_KR:Z_KROT]
    k_rope = krga * cos + z_lat[:, Z_KROT:Z_GQ] * sin

    @pl.when(in_prompt)
    def _():
        ckvp_ref[...] = c_kv
        krp_ref[...] = k_rope[:, :ROPE_DIM]

    @pl.when(jnp.logical_not(in_prompt))
    def _():
        ckvs_ref[...] = c_kv
        krs_ref[...] = k_rope[:, :ROPE_DIM]

    k_rope_b = k_rope.astype(BF16)

    q = jnp.dot(cq.astype(BF16), wq_ref[...], preferred_element_type=F32)
    kv = jnp.dot(c_kv.astype(BF16), wukv_ref[...], preferred_element_type=F32)
    for h in range(MLA_HEADS):
        lo = h * QK_PAD
        q_ref[:, lo:lo + NOPE_DIM] = (q[:, lo:lo + NOPE_DIM] * Q_SCALE).astype(BF16)
        rot = q[:, MLA_HEADS * QK_PAD + h * LANES:MLA_HEADS * QK_PAD + (h + 1) * LANES]
        q_ref[:, lo + NOPE_DIM:lo + QK_PAD] = (
            (q[:, lo + NOPE_DIM:lo + QK_PAD] * cos + rot * sin) * Q_SCALE).astype(BF16)
        k_ref[:, lo:lo + NOPE_DIM] = kv[:, h * NOPE_DIM:(h + 1) * NOPE_DIM].astype(BF16)
        k_ref[:, lo + NOPE_DIM:lo + QK_PAD] = k_rope_b
    v_ref[...] = kv[:, MLA_HEADS * NOPE_DIM:].astype(BF16)
    vt_ref[...] = jnp.dot(wvt_ref[...], c_kv.T.astype(BF16), preferred_element_type=F32).astype(BF16)

    pre = jnp.dot(krga.astype(BF16), wgk_ref[...], preferred_element_type=F32) + bgk_ref[...]
    lf_ref[...] = (jnp.minimum(pre, 0.0) - jnp.log1p(jnp.exp(-jnp.abs(pre)))) * (1.0 / GATE_NORMALIZER)

    gq_ref[...] = jnp.dot(xb, w_in_ref[:, Z_GQ:Z_GK], preferred_element_type=F32)
    gk_ref[...] = jnp.dot(xb, w_in_ref[:, Z_GK:Z_GV], preferred_element_type=F32)
    gv_ref[...] = jnp.dot(xb, w_in_ref[:, Z_GV:Z_GATE], preferred_element_type=F32)
    gate_ref[...] = jnp.dot(xb, w_in_ref[:, Z_GATE:Z_WIDTH], preferred_element_type=F32)


def _rotate_half_cols(w):
    half = ROPE_DIM // 2
    return jnp.concatenate([-w[..., half:], w[..., :half]], axis=-1)


def _mixer_weights(w_in, w_uq, w_ukv, w_gk2):
    pts = []
    acc = 0
    for s in (Q_RANK, KV_RANK, ROPE_DIM, GLA_KW, GLA_KW, GLA_WIDTH, GLA_WIDTH):
        acc += s
        pts.append(acc)
    cq, ckv, kr, gq, gk, gv, gate, ga = jnp.split(w_in, pts, axis=1)
    d = w_in.shape[0]
    w_in_r = jnp.concatenate(
        [cq, ckv, kr, ga, jnp.zeros((d, LANES - ROPE_DIM - GATE_RANK), F32),
         _rotate_half_cols(kr), jnp.zeros((d, LANES - ROPE_DIM), F32), gq, gk, gv, gate], axis=1).astype(BF16)

    wq = w_uq.reshape(Q_RANK, MLA_HEADS, NOPE_DIM + ROPE_DIM)
    wq_rope = wq[..., NOPE_DIM:]
    pad = jnp.zeros((Q_RANK, MLA_HEADS, QK_PAD - NOPE_DIM - ROPE_DIM), F32)
    wq_main = jnp.concatenate([wq, pad], axis=-1).reshape(Q_RANK, MLA_HEADS * QK_PAD)
    wq_rot = jnp.concatenate([_rotate_half_cols(wq_rope), pad], axis=-1).reshape(Q_RANK, MLA_HEADS * LANES)
    wq_r = jnp.concatenate([wq_main, wq_rot], axis=1).astype(BF16)

    wkv = w_ukv.reshape(KV_RANK, MLA_HEADS, NOPE_DIM + V_DIM)
    wukv_r = jnp.concatenate([wkv[..., :NOPE_DIM].reshape(KV_RANK, -1),
                              wkv[..., NOPE_DIM:].reshape(KV_RANK, -1)], axis=1).astype(BF16)

    wvt_r = wkv[..., NOPE_DIM:].reshape(KV_RANK, -1).T.astype(BF16)

    wgk_r = jnp.zeros((LANES, GLA_KW), F32).at[ROPE_DIM:ROPE_DIM + GATE_RANK].set(w_gk2).astype(BF16)
    return w_in_r, wq_r, wukv_r, wvt_r, wgk_r


def _rope_tables(pos):
    half = ROPE_DIM // 2
    freqs = ROPE_THETA ** (-jnp.arange(half, dtype=F32) / half)
    ang = pos.astype(F32)[:, None] * freqs[None, :]
    zeros = jnp.zeros((pos.shape[0], LANES - ROPE_DIM), F32)
    cos = jnp.concatenate([jnp.cos(ang), jnp.cos(ang), zeros], axis=1)
    sin = jnp.concatenate([jnp.sin(ang), jnp.sin(ang), zeros], axis=1)
    return cos, sin


def _mixer(xp, xs, cos, sin, w_in_r, wq_r, wukv_r, wvt_r, wgk_r, q_norm_g, kv_norm_g, b_gk):
    tp, ts = xp.shape[0], xs.shape[0]
    t = tp + ts
    tm = MIX_TM
    assert tp % tm == 0 and ts % tm == 0
    n_p = tp // tm
    row = lambda w: pl.BlockSpec((tm, w), lambda i: (i, 0))
    col = pl.BlockSpec((MLA_WIDTH, tm), lambda i: (0, i))
    full = lambda w, dt: (row(w), jax.ShapeDtypeStruct((t, w), dt))
    ckv_p, ckv_s = _two_stream_specs(tm, KV_RANK, n_p)
    kr_p, kr_s = _two_stream_specs(tm, ROPE_DIM, n_p)
    outs = [full(MLA_HEADS * QK_PAD, BF16), full(MLA_HEADS * QK_PAD, BF16), full(MLA_WIDTH, BF16),
            (col, jax.ShapeDtypeStruct((MLA_WIDTH, t), BF16)),
            (ckv_p, jax.ShapeDtypeStruct((tp, KV_RANK), F32)), (ckv_s, jax.ShapeDtypeStruct((ts, KV_RANK), F32)),
            (kr_p, jax.ShapeDtypeStruct((tp, ROPE_DIM), F32)), (kr_s, jax.ShapeDtypeStruct((ts, ROPE_DIM), F32)),
            full(GLA_KW, F32), full(GLA_KW, F32), full(GLA_WIDTH, F32), full(GLA_WIDTH, F32), full(GLA_KW, F32)]
    return pl.pallas_call(
        functools.partial(_mixer_kernel, n_prompt_tiles=n_p),
        grid=(t // tm,),
        in_specs=[*_two_stream_specs(tm, D_MODEL, n_p), _const_spec(w_in_r.shape), _const_spec(wq_r.shape),
                  _const_spec(wukv_r.shape), _const_spec(wvt_r.shape), _const_spec(wgk_r.shape),
                  _const_spec((1, Q_RANK)), _const_spec((1, KV_RANK)), _const_spec((1, GLA_KW)),
                  row(LANES), row(LANES)],
        out_specs=[o[0] for o in outs],
        out_shape=[o[1] for o in outs],
        compiler_params=pltpu.CompilerParams(dimension_semantics=("arbitrary",),
                                             vmem_limit_bytes=VMEM_LIMIT_BYTES),
        name="mixer",
    )(xp, xs, w_in_r, wq_r, wukv_r, wvt_r, wgk_r, q_norm_g.reshape(1, -1), kv_norm_g.reshape(1, -1),
      b_gk.reshape(1, -1), cos, sin)


MLA_TQ = 512
MLA_LOOKAHEAD = 2
NEG_BIG = -1e30
_NT = (((1,), (1,)), ((), ()))


def _mla_prompt_kernel(qi_ref, ki_ref, q_ref, k_ref, vt_ref, o_ref, m_ref, l_ref, acc_ref):
    step = pl.program_id(0)
    qi = qi_ref[step]
    ki = ki_ref[step]
    tq = q_ref.shape[0]
    tk = k_ref.shape[0]

    @pl.when(ki == 0)
    def _():
        m_ref[...] = jnp.full(m_ref.shape, NEG_BIG, F32)
        l_ref[...] = jnp.zeros(l_ref.shape, F32)
        acc_ref[...] = jnp.zeros(acc_ref.shape, F32)

    def update(masked):
        if masked:
            key_chunk = lax.broadcasted_iota(jnp.int32, (tk, tq), 0) // CHUNK
            qry_chunk = lax.broadcasted_iota(jnp.int32, (tk, tq), 1) // CHUNK
            visible = key_chunk <= qry_chunk
        def scores(h):
            return lax.dot_general(k_ref[:, h * QK_PAD:(h + 1) * QK_PAD], q_ref[:, h * QK_PAD:(h + 1) * QK_PAD],
                                   _NT, preferred_element_type=F32)

        ahead = [scores(h) for h in range(MLA_LOOKAHEAD)]
        for h in range(MLA_HEADS):
            s = ahead.pop(0)
            if h + MLA_LOOKAHEAD < MLA_HEADS:
                ahead.append(scores(h + MLA_LOOKAHEAD))
            if masked:
                s = jnp.where(visible, s, NEG_BIG)
            m_old = m_ref[h]
            m_new = jnp.maximum(m_old, jnp.max(s, axis=0, keepdims=True))
            p = jnp.exp2(s - m_new)
            alpha = jnp.exp2(m_old - m_new)
            l_ref[h] = alpha * l_ref[h] + jnp.sum(p, axis=0, keepdims=True)
            acc_ref[h] = alpha * acc_ref[h] + jnp.dot(vt_ref[h * V_DIM:(h + 1) * V_DIM, :], p.astype(BF16),
                                                      preferred_element_type=F32)
            m_ref[h] = m_new

    @pl.when(ki < qi)
    def _():
        update(False)

    @pl.when(ki == qi)
    def _():
        update(True)
        for h in range(MLA_HEADS):
            o_ref[:, h * V_DIM:(h + 1) * V_DIM] = (acc_ref[h] / l_ref[h]).T.astype(BF16)


def _mla_prompt(q, k, vt, n_tokens):
    tq = MLA_TQ
    assert n_tokens % tq == 0
    nq = n_tokens // tq
    pairs = [(a, b) for a in range(nq) for b in range(a + 1)]
    qi = jnp.asarray([p[0] for p in pairs], jnp.int32)
    ki = jnp.asarray([p[1] for p in pairs], jnp.int32)
    grid_spec = pltpu.PrefetchScalarGridSpec(
        num_scalar_prefetch=2,
        grid=(len(pairs),),
        in_specs=[pl.BlockSpec((tq, MLA_HEADS * QK_PAD), lambda s, qi, ki: (qi[s], 0)),
                  pl.BlockSpec((tq, MLA_HEADS * QK_PAD), lambda s, qi, ki: (ki[s], 0)),
                  pl.BlockSpec((MLA_WIDTH, tq), lambda s, qi, ki: (0, ki[s]))],
        out_specs=pl.BlockSpec((tq, MLA_WIDTH), lambda s, qi, ki: (qi[s], 0)),
        scratch_shapes=[pltpu.VMEM((MLA_HEADS, 1, tq), F32), pltpu.VMEM((MLA_HEADS, 1, tq), F32),
                        pltpu.VMEM((MLA_HEADS, V_DIM, tq), F32)],
    )
    return pl.pallas_call(
        _mla_prompt_kernel,
        grid_spec=grid_spec,
        out_shape=jax.ShapeDtypeStruct((n_tokens, MLA_WIDTH), BF16),
        compiler_params=pltpu.CompilerParams(dimension_semantics=("arbitrary",),
                                             vmem_limit_bytes=VMEM_LIMIT_BYTES),
        name="mla_prompt",
    )(qi, ki, q, k, vt)


def _mla_sample_kernel(q_ref, kn_ref, vn_ref, lat_ref, krc_ref, wukv_ref, o_ref):
    kvc = jnp.dot(lat_ref[0].astype(BF16), wukv_ref[...], preferred_element_type=F32)
    krc = krc_ref[0]
    for h in range(MLA_HEADS):
        q = q_ref[:, h * QK_PAD:(h + 1) * QK_PAD]
        knc = kvc[:, h * NOPE_DIM:(h + 1) * NOPE_DIM].astype(BF16)
        vc = kvc[:, MLA_HEADS * NOPE_DIM + h * V_DIM:MLA_HEADS * NOPE_DIM + (h + 1) * V_DIM].astype(BF16)
        s_c = (lax.dot_general(q[:, :NOPE_DIM], knc, _NT, preferred_element_type=F32)
               + lax.dot_general(q[:, NOPE_DIM:], krc, _NT, preferred_element_type=F32))
        s_n = lax.dot_general(q, kn_ref[:, h * QK_PAD:(h + 1) * QK_PAD], _NT, preferred_element_type=F32)
        m = jnp.maximum(jnp.max(s_c, axis=1, keepdims=True), jnp.max(s_n, axis=1, keepdims=True))
        p_c = jnp.exp2(s_c - m)
        p_n = jnp.exp2(s_n - m)
        l = jnp.sum(p_c, axis=1, keepdims=True) + jnp.sum(p_n, axis=1, keepdims=True)
        o = (jnp.dot(p_c.astype(BF16), vc, preferred_element_type=F32)
             + jnp.dot(p_n.astype(BF16), vn_ref[:, h * V_DIM:(h + 1) * V_DIM], preferred_element_type=F32))
        o_ref[:, h * V_DIM:(h + 1) * V_DIM] = (o / l).astype(BF16)


def _mla_sample(q, k, v, cache_lat, cache_kr, wukv_r, row0, n_new):
    bs, past, _ = cache_lat.shape
    assert row0 % n_new == 0
    blk0 = row0 // n_new
    krc = jnp.pad(cache_kr, ((0, 0), (0, 0), (0, LANES - ROPE_DIM))).astype(BF16)
    new = lambda w: pl.BlockSpec((n_new, w), lambda b: (blk0 + b, 0))
    return pl.pallas_call(
        _mla_sample_kernel,
        grid=(bs,),
        in_specs=[new(MLA_HEADS * QK_PAD), new(MLA_HEADS * QK_PAD), new(MLA_WIDTH),
                  pl.BlockSpec((1, past, KV_RANK), lambda b: (b, 0, 0)),
                  pl.BlockSpec((1, past, LANES), lambda b: (b, 0, 0)),
                  _const_spec(wukv_r.shape)],
        out_specs=pl.BlockSpec((n_new, MLA_WIDTH), lambda b: (b, 0)),
        out_shape=jax.ShapeDtypeStruct((bs * n_new, MLA_WIDTH), BF16),
        compiler_params=pltpu.CompilerParams(dimension_semantics=("arbitrary",),
                                             vmem_limit_bytes=VMEM_LIMIT_BYTES),
        name="mla_sample",
    )(q, k, v, cache_lat, krc, wukv_r)


GLA_SUB = 16
_TN = (((0,), (0,)), ((), ()))


def _cumsum_rows(x):
    n = x.shape[0]
    row = lax.broadcasted_iota(jnp.int32, x.shape, 0)
    s = 1
    while s < n:
        x = x + jnp.where(row >= s, pltpu.roll(x, s, axis=0), 0.0)
        s *= 2
    return x


def _gla_kernel(q_ref, k_ref, v_ref, lf_ref, gate_ref, s0_ref, g_ref, o_ref, send_ref, st_ref, *, n_chunks):
    c = pl.program_id(1)
    n_rows = q_ref.shape[0]
    n_sub = n_rows // GLA_SUB

    @pl.when(c == 0)
    def _():
        for h in range(GLA_HEADS):
            st_ref[h] = s0_ref[0, h].T

    sub_row = lax.broadcasted_iota(jnp.int32, (GLA_SUB, GLA_DK), 0)
    for h in range(GLA_HEADS):
        ks = slice(h * GLA_DK, (h + 1) * GLA_DK)
        vs = slice(h * GLA_DV, (h + 1) * GLA_DV)
        q = q_ref[:, ks] * (GLA_DK ** -0.5)
        k = k_ref[:, ks]
        v = v_ref[:, vs]
        vb = v.astype(BF16)
        b = _cumsum_rows(lf_ref[:, ks])
        b_last = b[n_rows - 1:n_rows]
        st = st_ref[h]
        o_inter = lax.dot_general((q * jnp.exp(b)).astype(BF16), st.astype(BF16), _NT,
                                  preferred_element_type=F32)
        outs = []
        for i in range(n_sub):
            r0 = i * GLA_SUB
            bi = b[r0:r0 + GLA_SUB]
            qi = q[r0:r0 + GLA_SUB]
            o_i = o_inter[r0:r0 + GLA_SUB]
            if i > 0:
                b_ref = b[r0 - 1:r0]
                qh = (qi * jnp.exp(bi - b_ref)).astype(BF16)
                kh = (k[:r0] * jnp.exp(b_ref - b[:r0])).astype(BF16)
                a_off = lax.dot_general(qh, kh, _NT, preferred_element_type=F32)
                o_i = o_i + jnp.dot(a_off.astype(BF16), vb[:r0], preferred_element_type=F32)
            for j in range(GLA_SUB):
                r = r0 + j
                decay = jnp.exp(jnp.where(sub_row >= j, bi - b[r:r + 1], -jnp.inf))
                a_col = jnp.sum(qi * k[r:r + 1] * decay, axis=1, keepdims=True)
                o_i = o_i + a_col * v[r:r + 1]
            outs.append(o_i)
        o = jnp.concatenate(outs, axis=0)

        kd = (k * jnp.exp(b_last - b)).astype(BF16)
        st_new = st * jnp.exp(b_last) + lax.dot_general(vb, kd, _TN, preferred_element_type=F32)
        st_ref[h] = st_new

        gate = gate_ref[:, vs]
        on = _rms(o, g_ref[...])
        o_ref[:, vs] = (on * (gate / (1.0 + jnp.exp(-gate)))).astype(BF16)

    @pl.when(c == n_chunks - 1)
    def _():
        for h in range(GLA_HEADS):
            send_ref[0, h] = st_ref[h].T


def _gla(gq, gk, gv, lf, gate, s0, gla_norm_g, row0, n_seq, n_chunks):
    assert row0 % CHUNK == 0
    blk0 = row0 // CHUNK
    row = lambda w: pl.BlockSpec((CHUNK, w), lambda b, c: (blk0 + b * n_chunks + c, 0))
    state = pl.BlockSpec((1, GLA_HEADS, GLA_DK, GLA_DV), lambda b, c: (b, 0, 0, 0))
    return pl.pallas_call(
        functools.partial(_gla_kernel, n_chunks=n_chunks),
        grid=(n_seq, n_chunks),
        in_specs=[row(GLA_KW), row(GLA_KW), row(GLA_WIDTH), row(GLA_KW), row(GLA_WIDTH), state,
                  pl.BlockSpec((1, GLA_DV), lambda b, c: (0, 0))],
        out_specs=[pl.BlockSpec((CHUNK, GLA_WIDTH), lambda b, c: (b * n_chunks + c, 0)), state],
        out_shape=[jax.ShapeDtypeStruct((n_seq * n_chunks * CHUNK, GLA_WIDTH), BF16),
                   jax.ShapeDtypeStruct((n_seq, GLA_HEADS, GLA_DK, GLA_DV), F32)],
        scratch_shapes=[pltpu.VMEM((GLA_HEADS, GLA_DV, GLA_DK), F32)],
        compiler_params=pltpu.CompilerParams(dimension_semantics=("arbitrary", "arbitrary"),
                                             vmem_limit_bytes=VMEM_LIMIT_BYTES),
        name="gla",
    )(gq, gk, gv, lf, gate, s0, gla_norm_g.reshape(1, -1))


OUT_TM = 256


def _layer_norm(y, g, b):
    mu = jnp.mean(y, axis=-1, keepdims=True)
    yc = y - mu
    var = jnp.mean(yc * yc, axis=-1, keepdims=True)
    return yc * lax.rsqrt(var + LN_EPS) * g + b


def _outproj_kernel(mlap_ref, mlas_ref, glap_ref, glas_ref, xp_ref, xs_ref, wo_ref, g_ref, b_ref, wq_ref,
                    h_ref, hb_ref, qp_ref, *, n_prompt_tiles):
    in_prompt = pl.program_id(0) < n_prompt_tiles
    mla = jnp.where(in_prompt, mlap_ref[...], mlas_ref[...])
    gla = jnp.where(in_prompt, glap_ref[...], glas_ref[...])
    x = jnp.where(in_prompt, xp_ref[...], xs_ref[...])
    mix = (jnp.dot(mla, wo_ref[:MLA_WIDTH], preferred_element_type=F32)
           + jnp.dot(gla, wo_ref[MLA_WIDTH:], preferred_element_type=F32))
    h = _layer_norm(DN_ALPHA * x + mix, g_ref[...], b_ref[...])
    hb = h.astype(BF16)
    h_ref[...] = h
    hb_ref[...] = hb
    qp = jnp.dot(hb, wq_ref[...], preferred_element_type=F32).astype(BF16)
    per_head = qp_ref.shape[2]
    for head in range(qp_ref.shape[0]):
        qp_ref[head] = qp[:, head * per_head:(head + 1) * per_head]


def _outproj(mla_p, mla_s, gla_p, gla_s, xp, xs, w_o_b, ln1_g, ln1_b, peer_wq_b):
    tp, ts = xp.shape[0], xs.shape[0]
    t = tp + ts
    tm = OUT_TM
    assert tp % tm == 0 and ts % tm == 0
    n_p = tp // tm
    row = lambda w: pl.BlockSpec((tm, w), lambda i: (i, 0))
    return pl.pallas_call(
        functools.partial(_outproj_kernel, n_prompt_tiles=n_p),
        grid=(t // tm,),
        in_specs=[*_two_stream_specs(tm, MLA_WIDTH, n_p), *_two_stream_specs(tm, GLA_WIDTH, n_p),
                  *_two_stream_specs(tm, D_MODEL, n_p), _const_spec(w_o_b.shape),
                  _const_spec((1, D_MODEL)), _const_spec((1, D_MODEL)), _const_spec(peer_wq_b.shape)],
        out_specs=[row(D_MODEL), row(D_MODEL),
                   pl.BlockSpec((PEER_HEADS, tm, 2 * PEER_HALF), lambda i: (0, i, 0))],
        out_shape=[jax.ShapeDtypeStruct((t, D_MODEL), F32), jax.ShapeDtypeStruct((t, D_MODEL), BF16),
                   jax.ShapeDtypeStruct((PEER_HEADS, t, 2 * PEER_HALF), BF16)],
        compiler_params=pltpu.CompilerParams(dimension_semantics=("arbitrary",),
                                             vmem_limit_bytes=VMEM_LIMIT_BYTES),
        name="outproj",
    )(mla_p, mla_s, gla_p, gla_s, xp, xs, w_o_b, ln1_g.reshape(1, -1), ln1_b.reshape(1, -1), peer_wq_b)


PEER_HEADS = 8
N_KEYS = 128
PEER_HALF = 128
PEER_TOPK = 16
ROUTE_TM = 256
CAND_WIDE_RANKS = 8
CAND_ROWS = PEER_TOPK + (CAND_WIDE_RANKS - 1) * SUBLANES + (PEER_TOPK - CAND_WIDE_RANKS)


def _first_max(x, row):
    n = x.shape[0]
    blocks = [(x[r:r + SUBLANES], row[r:r + SUBLANES]) for r in range(0, n, SUBLANES)]
    while len(blocks) > 1:
        merged = []
        for j in range(0, len(blocks) - 1, 2):
            (va, ia), (vb, ib) = blocks[j], blocks[j + 1]
            take = va >= vb
            merged.append((jnp.where(take, va, vb), jnp.where(take, ia, ib)))
        if len(blocks) % 2:
            merged.append(blocks[-1])
        blocks = merged
    v, i = blocks[0]
    m = jnp.max(v, axis=0, keepdims=True)
    pick = jnp.min(jnp.where(v == m, i, float(n)), axis=0, keepdims=True)
    return m, pick


def _extract_top(xs, row, n_out, on_pick):
    xs = list(xs)
    for k in range(n_out):
        for p, x in enumerate(xs):
            m, pick = _first_max(x, row)
            hit = row == pick
            on_pick(p, k, m, hit, pick)
            xs[p] = jnp.where(hit, -jnp.inf, x)


def _route_kernel(qp_ref, keys_ref, row_ref, g_ref, i1_ref, i2_ref, sv_ref, si_ref, gt_ref, et_ref):
    lax.fori_loop(0, PEER_HEADS,
                  functools.partial(_route_head, qp_ref, keys_ref, row_ref, sv_ref, si_ref, gt_ref, et_ref), 0)
    g_ref[...] = gt_ref[...].T
    expert = et_ref[...]
    key0 = jnp.floor(expert * (1.0 / N_KEYS))
    i1_ref[...] = key0.T
    i2_ref[...] = (expert - key0 * N_KEYS).T


def _route_head(qp_ref, keys_ref, row_ref, sv_ref, si_ref, gt_ref, et_ref, h, carry):
    tm = qp_ref.shape[1]
    q = qp_ref[h]
    scores = [lax.dot_general(keys_ref[half], q[:, half * PEER_HALF:(half + 1) * PEER_HALF], _NT,
                              preferred_element_type=F32) for half in range(2)]

    def keep(half, k, m, hit, pick):
        sv_ref[half, k:k + 1, :] = m
        si_ref[half, k:k + 1, :] = pick

    _extract_top(scores, row_ref[...], PEER_TOPK, keep)

    s1, s2 = sv_ref[0], sv_ref[1]
    i1, i2 = si_ref[0], si_ref[1]
    bc = lambda r, n: jnp.broadcast_to(r, (n, tm))
    wide = range(1, CAND_WIDE_RANKS)
    cand = jnp.concatenate([bc(s1[0:1], PEER_TOPK) + s2]
                           + [bc(s1[a:a + 1], SUBLANES) + s2[:SUBLANES] for a in wide]
                           + [s1[CAND_WIDE_RANKS:] + bc(s2[0:1], PEER_TOPK - CAND_WIDE_RANKS)], axis=0)
    i1 = i1 * N_KEYS
    c_expert = jnp.concatenate([bc(i1[0:1], PEER_TOPK) + i2]
                               + [bc(i1[a:a + 1], SUBLANES) + i2[:SUBLANES] for a in wide]
                               + [i1[CAND_WIDE_RANKS:] + bc(i2[0:1], PEER_TOPK - CAND_WIDE_RANKS)], axis=0)
    base = pl.multiple_of(h * PEER_TOPK, PEER_TOPK)

    def keep_pair(_, k, m, hit, pick):
        gt_ref[pl.ds(base + k, 1), :] = m
        et_ref[pl.ds(base + k, 1), :] = jnp.sum(jnp.where(hit, c_expert, 0.0), axis=0, keepdims=True)

    _extract_top([cand], row_ref[:CAND_ROWS], PEER_TOPK, keep_pair)

    sc = gt_ref[pl.ds(base, PEER_TOPK), :]
    e = jnp.exp(sc - sc[0:1])
    gt_ref[pl.ds(base, PEER_TOPK), :] = e / jnp.sum(e, axis=0, keepdims=True)
    return carry


def _route(qp, keys_b):
    t = qp.shape[1]
    tm = ROUTE_TM
    assert t % tm == 0
    n_sel = PEER_HEADS * PEER_TOPK
    out = pl.BlockSpec((tm, n_sel), lambda i: (i, 0))
    row_index = jnp.broadcast_to(jnp.arange(N_KEYS, dtype=F32)[:, None], (N_KEYS, tm))
    return pl.pallas_call(
        _route_kernel,
        grid=(t // tm,),
        in_specs=[pl.BlockSpec((PEER_HEADS, tm, 2 * PEER_HALF), lambda i: (0, i, 0)),
                  pl.BlockSpec(keys_b.shape, lambda i: (0, 0, 0)),
                  pl.BlockSpec((N_KEYS, tm), lambda i: (0, 0))],
        out_specs=[out, out, out],
        out_shape=[jax.ShapeDtypeStruct((t, n_sel), F32)] * 3,
        scratch_shapes=[pltpu.VMEM((2, PEER_TOPK, tm), F32), pltpu.VMEM((2, PEER_TOPK, tm), F32),
                        pltpu.VMEM((n_sel, tm), F32), pltpu.VMEM((n_sel, tm), F32)],
        compiler_params=pltpu.CompilerParams(dimension_semantics=("arbitrary",),
                                             vmem_limit_bytes=VMEM_LIMIT_BYTES),
        name="peer_route",
    )(qp, keys_b, row_index)


SCATTER_TB = 128
SCATTER_UNROLL = SUBLANES
W_ROW_TILE = SUBLANES


def _scatter_kernel(g_ref, i1_ref, i2_ref, w_ref):
    tb = g_ref.shape[0]
    key = lax.broadcasted_iota(jnp.int32, (N_KEYS, g_ref.shape[1]), 0).astype(F32)

    def body(grp, carry):
        t0 = pl.multiple_of(grp * SCATTER_UNROLL, SCATTER_UNROLL)
        g8 = g_ref[pl.ds(t0, SCATTER_UNROLL), :]
        i18 = i1_ref[pl.ds(t0, SCATTER_UNROLL), :]
        i28 = i2_ref[pl.ds(t0, SCATTER_UNROLL), :]
        for u in range(SCATTER_UNROLL):
            g = jnp.broadcast_to(g8[u:u + 1], key.shape)
            i1 = jnp.broadcast_to(i18[u:u + 1], key.shape)
            i2 = jnp.broadcast_to(i28[u:u + 1], key.shape)
            a_t = jnp.where(key == i1, g, 0.0).astype(BF16)
            b_t = jnp.where(key == i2, 1.0, 0.0).astype(BF16)
            w = lax.dot_general(a_t, b_t, _NT, preferred_element_type=F32)
            w_ref[:, pl.ds(t0 + u, 1), :, :] = w.reshape(N_KEYS // W_ROW_TILE, 1, W_ROW_TILE, N_KEYS)
        return carry

    lax.fori_loop(0, tb // SCATTER_UNROLL, body, 0)


def _scatter(g, i1, i2):
    t = g.shape[0]
    tb = SCATTER_TB
    assert t % tb == 0
    sel = pl.BlockSpec((tb, g.shape[1]), lambda i: (i, 0))
    n_grp = N_KEYS // W_ROW_TILE
    return pl.pallas_call(
        _scatter_kernel,
        grid=(t // tb,),
        in_specs=[sel, sel, sel],
        out_specs=pl.BlockSpec((n_grp, tb, W_ROW_TILE, N_KEYS), lambda i: (0, i, 0, 0)),
        out_shape=jax.ShapeDtypeStruct((n_grp, t, W_ROW_TILE, N_KEYS), F32),
        compiler_params=pltpu.CompilerParams(dimension_semantics=("arbitrary",),
                                             vmem_limit_bytes=VMEM_LIMIT_BYTES),
        name="peer_scatter",
    )(g, i1, i2)


EXP_TOKEN_TILES = 8
EXP_ROWS = 4
EXP_TE = EXP_ROWS * N_KEYS
EXP_SUBTILES = 4


def _gelu(x):
    return x * 0.5 * (1.0 + lax.erf(x * (2.0 ** -0.5)))


def _experts_kernel(hb_ref, w_ref, u_ref, v_ref, h_ref, g_ref, b_ref, y_ref):
    j = pl.program_id(1)
    tm = hb_ref.shape[0]

    @pl.when(j == 0)
    def _():
        y_ref[...] = DN_ALPHA * h_ref[...]

    row0 = (j % (W_ROW_TILE // EXP_ROWS)) * EXP_ROWS
    sub = tm // EXP_SUBTILES
    acts = [lax.dot_general(hb_ref[s * sub:(s + 1) * sub, :], u_ref[...], _NT, preferred_element_type=F32)
            for s in range(EXP_SUBTILES)]
    for s in range(EXP_SUBTILES):
        w = jnp.concatenate([w_ref[pl.ds(s * sub * W_ROW_TILE + row0 + r, sub, stride=W_ROW_TILE), :]
                             for r in range(EXP_ROWS)], axis=1)
        p = (w * _gelu(acts[s])).astype(BF16)
        y_ref[s * sub:(s + 1) * sub, :] += jnp.dot(p, v_ref[...], preferred_element_type=F32)

    @pl.when(j == pl.num_programs(1) - 1)
    def _():
        y_ref[...] = _layer_norm(y_ref[...], g_ref[...], b_ref[...])


def _experts(hb, h, w_table, u_b, v_b, ln2_g, ln2_b):
    t = hb.shape[0]
    nt = EXP_TOKEN_TILES
    assert t % (nt * 2 * SUBLANES) == 0
    tm = t // nt
    ne = u_b.shape[0] // EXP_TE
    steps_per_group = W_ROW_TILE // EXP_ROWS
    w2d = w_table.reshape(-1, N_KEYS)
    once = lambda: pl.BlockSpec((tm, D_MODEL), lambda i, j: (i, 0), pipeline_mode=pl.Buffered(1))
    return pl.pallas_call(
        _experts_kernel,
        grid=(nt, ne),
        in_specs=[once(),
                  pl.BlockSpec((tm * W_ROW_TILE, N_KEYS), lambda i, j: ((j // steps_per_group) * nt + i, 0)),
                  pl.BlockSpec((EXP_TE, D_MODEL), lambda i, j: (j, 0)),
                  pl.BlockSpec((EXP_TE, D_MODEL), lambda i, j: (j, 0)),
                  once(), pl.BlockSpec((1, D_MODEL), lambda i, j: (0, 0)),
                  pl.BlockSpec((1, D_MODEL), lambda i, j: (0, 0))],
        out_specs=pl.BlockSpec((tm, D_MODEL), lambda i, j: (i, 0)),
        out_shape=jax.ShapeDtypeStruct((t, D_MODEL), F32),
        compiler_params=pltpu.CompilerParams(dimension_semantics=("arbitrary", "arbitrary"),
                                             vmem_limit_bytes=VMEM_LIMIT_BYTES),
        name="peer_experts",
    )(hb, w2d, u_b, v_b, h, ln2_g.reshape(1, -1), ln2_b.reshape(1, -1))


def kernel(x_prompt, x_sample, cache_kv_latent, cache_k_rope, state_gla, w_in, q_norm_g, w_uq, kv_norm_g, w_ukv,
           w_gk2, b_gk, gla_norm_g, w_o, ln1_g, ln1_b, peer_wq, peer_keys, peer_u, peer_v, ln2_g, ln2_b):
    bp, sp, _ = x_prompt.shape
    bs, ss, _ = x_sample.shape
    past = cache_kv_latent.shape[1]
    tp, ts = bp * sp, bs * ss
    xp = x_prompt.reshape(tp, D_MODEL)
    xs = x_sample.reshape(ts, D_MODEL)
    pos = jnp.concatenate([jnp.tile(jnp.arange(sp, dtype=jnp.int32), bp),
                           jnp.tile(past + jnp.arange(ss, dtype=jnp.int32), bs)])
    cos, sin = _rope_tables(pos)
    w_in_r, wq_r, wukv_r, wvt_r, wgk_r = _mixer_weights(w_in, w_uq, w_ukv, w_gk2)
    q, k, v, vt, ckv_p, ckv_s, kr_p, kr_s, gq, gk, gv, gate, lf = _mixer(
        xp, xs, cos, sin, w_in_r, wq_r, wukv_r, wvt_r, wgk_r, q_norm_g, kv_norm_g, b_gk)
    o_p = _mla_prompt(q, k, vt, tp)
    o_s = _mla_sample(q, k, v, cache_kv_latent, cache_k_rope, wukv_r, tp, ss)
    g_p, st_p = _gla(gq, gk, gv, lf, gate, jnp.zeros((bp,) + state_gla.shape[1:], F32), gla_norm_g,
                     0, bp, sp // CHUNK)
    g_s, st_s = _gla(gq, gk, gv, lf, gate, state_gla, gla_norm_g, tp, bs, ss // CHUNK)

    h, hb, qp = _outproj(o_p, o_s, g_p, g_s, xp, xs, w_o.astype(BF16), ln1_g, ln1_b, peer_wq.astype(BF16))
    gates, i1, i2 = _route(qp, peer_keys.astype(BF16))
    w_table = _scatter(gates, i1, i2)
    y = _experts(hb, h, w_table, peer_u.astype(BF16), peer_v.astype(BF16), ln2_g, ln2_b)

    dt = x_prompt.dtype
    return (y[:tp].reshape(bp, sp, D_MODEL), y[tp:].reshape(bs, ss, D_MODEL),
            ckv_p.reshape(bp, sp, KV_RANK), kr_p.reshape(bp, sp, ROPE_DIM), st_p.astype(dt),
            ckv_s.reshape(bs, ss, KV_RANK), kr_s.reshape(bs, ss, ROPE_DIM), st_s.astype(dt))
```

```python
import functools
import math

import jax
import jax.numpy as jnp
from jax import lax
from jax.experimental import pallas as pl
from jax.experimental.pallas import tpu as pltpu

F32 = jnp.float32
BF16 = jnp.bfloat16

LANES = 128
SUBLANES = 8
VMEM_LIMIT_BYTES = 60 * 1024 * 1024

D_MODEL = 2048
CHUNK = 64
MLA_HEADS = 8
Q_RANK = 512
KV_RANK = 256
NOPE_DIM = 128
ROPE_DIM = 64
V_DIM = 128
ROPE_THETA = 10000.0
MLA_SCALE = (NOPE_DIM + ROPE_DIM) ** -0.5
Q_SCALE = MLA_SCALE * math.log2(math.e)
QK_PAD = 256
GLA_HEADS = 4
GLA_DK = 128
GLA_DV = 256
GATE_RANK = 16
GATE_NORMALIZER = 16.0
GLA_KW = GLA_HEADS * GLA_DK
GLA_WIDTH = GLA_HEADS * GLA_DV
MLA_WIDTH = MLA_HEADS * V_DIM
DN_ALPHA = 2.0 ** 0.25
LN_EPS = 1e-5
RMS_EPS = 1e-6

Z_CQ = 0
Z_CKV = Z_CQ + Q_RANK
Z_KR = Z_CKV + KV_RANK
Z_KROT = Z_KR + LANES
Z_GQ = Z_KROT + LANES
Z_GK = Z_GQ + GLA_KW
Z_GV = Z_GK + GLA_KW
Z_GATE = Z_GV + GLA_WIDTH
Z_WIDTH = Z_GATE + GLA_WIDTH


def _const_spec(shape):
    nd = len(shape)
    return pl.BlockSpec(shape, lambda *_: (0,) * nd, pipeline_mode=pl.Buffered(1))


def _rms(x, g):
    return x * lax.rsqrt(jnp.mean(x * x, axis=-1, keepdims=True) + RMS_EPS) * g


MIX_TM = 256


def _two_stream_specs(tm, width, n_first):
    return (pl.BlockSpec((tm, width), lambda i: (jnp.minimum(i, n_first - 1), 0)),
            pl.BlockSpec((tm, width), lambda i: (jnp.maximum(i - n_first, 0), 0)))


def _mixer_kernel(xp_ref, xs_ref, w_in_ref, wq_ref, wukv_ref, wvt_ref, wgk_ref, qg_ref, kvg_ref, bgk_ref,
                  cos_ref, sin_ref, q_ref, k_ref, v_ref, vt_ref, ckvp_ref, ckvs_ref, krp_ref, krs_ref,
                  gq_ref, gk_ref, gv_ref, gate_ref, lf_ref, *, n_prompt_tiles):
    in_prompt = pl.program_id(0) < n_prompt_tiles
    xb = jnp.where(in_prompt, xp_ref[...], xs_ref[...]).astype(BF16)
    cos = cos_ref[...]
    sin = sin_ref[...]

    z_lat = jnp.dot(xb, w_in_ref[:, Z_CQ:Z_GQ], preferred_element_type=F32)
    cq = _rms(z_lat[:, Z_CQ:Z_CKV], qg_ref[...])
    c_kv = _rms(z_lat[:, Z_CKV:Z_KR], kvg_ref[...])
    krga = z_lat[:, Z_KR:Z_KROT]
    k_rope = krga * cos + z_lat[:, Z_KROT:Z_GQ] * sin

    @pl.when(in_prompt)
    def _():
        ckvp_ref[...] = c_kv
        krp_ref[...] = k_rope[:, :ROPE_DIM]

    @pl.when(jnp.logical_not(in_prompt))
    def _():
        ckvs_ref[...] = c_kv
        krs_ref[...] = k_rope[:, :ROPE_DIM]

    k_rope_b = k_rope.astype(BF16)

    q = jnp.dot(cq.astype(BF16), wq_ref[...], preferred_element_type=F32)
    kv = jnp.dot(c_kv.astype(BF16), wukv_ref[...], preferred_element_type=F32)
    for h in range(MLA_HEADS):
        lo = h * QK_PAD
        q_ref[:, lo:lo + NOPE_DIM] = (q[:, lo:lo + NOPE_DIM] * Q_SCALE).astype(BF16)
        rot = q[:, MLA_HEADS * QK_PAD + h * LANES:MLA_HEADS * QK_PAD + (h + 1) * LANES]
        q_ref[:, lo + NOPE_DIM:lo + QK_PAD] = (
            (q[:, lo + NOPE_DIM:lo + QK_PAD] * cos + rot * sin) * Q_SCALE).astype(BF16)
        k_ref[:, lo:lo + NOPE_DIM] = kv[:, h * NOPE_DIM:(h + 1) * NOPE_DIM].astype(BF16)
        k_ref[:, lo + NOPE_DIM:lo + QK_PAD] = k_rope_b
    v_ref[...] = kv[:, MLA_HEADS * NOPE_DIM:].astype(BF16)
    vt_ref[...] = jnp.dot(wvt_ref[...], c_kv.T.astype(BF16), preferred_element_type=F32).astype(BF16)

    pre = jnp.dot(krga.astype(BF16), wgk_ref[...], preferred_element_type=F32) + bgk_ref[...]
    lf_ref[...] = (jnp.minimum(pre, 0.0) - jnp.log1p(jnp.exp(-jnp.abs(pre)))) * (1.0 / GATE_NORMALIZER)

    gq_ref[...] = jnp.dot(xb, w_in_ref[:, Z_GQ:Z_GK], preferred_element_type=F32)
    gk_ref[...] = jnp.dot(xb, w_in_ref[:, Z_GK:Z_GV], preferred_element_type=F32)
    gv_ref[...] = jnp.dot(xb, w_in_ref[:, Z_GV:Z_GATE], preferred_element_type=F32)
    gate_ref[...] = jnp.dot(xb, w_in_ref[:, Z_GATE:Z_WIDTH], preferred_element_type=F32)


def _rotate_half_cols(w):
    half = ROPE_DIM // 2
    return jnp.concatenate([-w[..., half:], w[..., :half]], axis=-1)


def _mixer_weights(w_in, w_uq, w_ukv, w_gk2):
    pts = []
    acc = 0
    for s in (Q_RANK, KV_RANK, ROPE_DIM, GLA_KW, GLA_KW, GLA_WIDTH, GLA_WIDTH):
        acc += s
        pts.append(acc)
    cq, ckv, kr, gq, gk, gv, gate, ga = jnp.split(w_in, pts, axis=1)
    d = w_in.shape[0]
    w_in_r = jnp.concatenate(
        [cq, ckv, kr, ga, jnp.zeros((d, LANES - ROPE_DIM - GATE_RANK), F32),
         _rotate_half_cols(kr), jnp.zeros((d, LANES - ROPE_DIM), F32), gq, gk, gv, gate], axis=1).astype(BF16)

    wq = w_uq.reshape(Q_RANK, MLA_HEADS, NOPE_DIM + ROPE_DIM)
    wq_rope = wq[..., NOPE_DIM:]
    pad = jnp.zeros((Q_RANK, MLA_HEADS, QK_PAD - NOPE_DIM - ROPE_DIM), F32)
    wq_main = jnp.concatenate([wq, pad], axis=-1).reshape(Q_RANK, MLA_HEADS * QK_PAD)
    wq_rot = jnp.concatenate([_rotate_half_cols(wq_rope), pad], axis=-1).reshape(Q_RANK, MLA_HEADS * LANES)
    wq_r = jnp.concatenate([wq_main, wq_rot], axis=1).astype(BF16)

    wkv = w_ukv.reshape(KV_RANK, MLA_HEADS, NOPE_DIM + V_DIM)
    wukv_r = jnp.concatenate([wkv[..., :NOPE_DIM].reshape(KV_RANK, -1),
                              wkv[..., NOPE_DIM:].reshape(KV_RANK, -1)], axis=1).astype(BF16)

    wvt_r = wkv[..., NOPE_DIM:].reshape(KV_RANK, -1).T.astype(BF16)

    wgk_r = jnp.zeros((LANES, GLA_KW), F32).at[ROPE_DIM:ROPE_DIM + GATE_RANK].set(w_gk2).astype(BF16)
    return w_in_r, wq_r, wukv_r, wvt_r, wgk_r


def _rope_tables(pos):
    half = ROPE_DIM // 2
    freqs = ROPE_THETA ** (-jnp.arange(half, dtype=F32) / half)
    ang = pos.astype(F32)[:, None] * freqs[None, :]
    zeros = jnp.zeros((pos.shape[0], LANES - ROPE_DIM), F32)
    cos = jnp.concatenate([jnp.cos(ang), jnp.cos(ang), zeros], axis=1)
    sin = jnp.concatenate([jnp.sin(ang), jnp.sin(ang), zeros], axis=1)
    return cos, sin


def _mixer(xp, xs, cos, sin, w_in_r, wq_r, wukv_r, wvt_r, wgk_r, q_norm_g, kv_norm_g, b_gk):
    tp, ts = xp.shape[0], xs.shape[0]
    t = tp + ts
    tm = MIX_TM
    assert tp % tm == 0 and ts % tm == 0
    n_p = tp // tm
    row = lambda w: pl.BlockSpec((tm, w), lambda i: (i, 0))
    col = pl.BlockSpec((MLA_WIDTH, tm), lambda i: (0, i))
    full = lambda w, dt: (row(w), jax.ShapeDtypeStruct((t, w), dt))
    ckv_p, ckv_s = _two_stream_specs(tm, KV_RANK, n_p)
    kr_p, kr_s = _two_stream_specs(tm, ROPE_DIM, n_p)
    outs = [full(MLA_HEADS * QK_PAD, BF16), full(MLA_HEADS * QK_PAD, BF16), full(MLA_WIDTH, BF16),
            (col, jax.ShapeDtypeStruct((MLA_WIDTH, t), BF16)),
            (ckv_p, jax.ShapeDtypeStruct((tp, KV_RANK), F32)), (ckv_s, jax.ShapeDtypeStruct((ts, KV_RANK), F32)),
            (kr_p, jax.ShapeDtypeStruct((tp, ROPE_DIM), F32)), (kr_s, jax.ShapeDtypeStruct((ts, ROPE_DIM), F32)),
            full(GLA_KW, F32), full(GLA_KW, F32), full(GLA_WIDTH, F32), full(GLA_WIDTH, F32), full(GLA_KW, F32)]
    return pl.pallas_call(
        functools.partial(_mixer_kernel, n_prompt_tiles=n_p),
        grid=(t // tm,),
        in_specs=[*_two_stream_specs(tm, D_MODEL, n_p), _const_spec(w_in_r.shape), _const_spec(wq_r.shape),
                  _const_spec(wukv_r.shape), _const_spec(wvt_r.shape), _const_spec(wgk_r.shape),
                  _const_spec((1, Q_RANK)), _const_spec((1, KV_RANK)), _const_spec((1, GLA_KW)),
                  row(LANES), row(LANES)],
        out_specs=[o[0] for o in outs],
        out_shape=[o[1] for o in outs],
        compiler_params=pltpu.CompilerParams(dimension_semantics=("arbitrary",),
                                             vmem_limit_bytes=VMEM_LIMIT_BYTES),
        name="mixer",
    )(xp, xs, w_in_r, wq_r, wukv_r, wvt_r, wgk_r, q_norm_g.reshape(1, -1), kv_norm_g.reshape(1, -1),
      b_gk.reshape(1, -1), cos, sin)


MLA_TQ = 512
MLA_LOOKAHEAD = 2
MLA_SIDE_ROWS = 128
NEG_BIG = -1e30
_NT = (((1,), (1,)), ((), ()))


def _mla_prompt_kernel(qi_ref, ki_ref, q_ref, k_ref, vt_ref, *rest, n_side, side_steps):
    side_in = rest[:n_side]
    o_ref = rest[n_side]
    side_out = rest[n_side + 1:2 * n_side + 1]
    m_ref, l_ref, acc_ref = rest[2 * n_side + 1:]
    step = pl.program_id(0)
    qi = qi_ref[step]
    ki = ki_ref[step]
    tq = q_ref.shape[0]
    tk = k_ref.shape[0]

    @pl.when(step < side_steps)
    def _():
        for src, dst in zip(side_in, side_out):
            dst[...] = src[...].astype(dst.dtype)

    @pl.when(ki == 0)
    def _():
        m_ref[...] = jnp.full(m_ref.shape, NEG_BIG, F32)
        l_ref[...] = jnp.zeros(l_ref.shape, F32)
        acc_ref[...] = jnp.zeros(acc_ref.shape, F32)

    def update(masked):
        if masked:
            key_chunk = lax.broadcasted_iota(jnp.int32, (tk, tq), 0) // CHUNK
            qry_chunk = lax.broadcasted_iota(jnp.int32, (tk, tq), 1) // CHUNK
            visible = key_chunk <= qry_chunk
        def scores(h):
            return lax.dot_general(k_ref[:, h * QK_PAD:(h + 1) * QK_PAD], q_ref[:, h * QK_PAD:(h + 1) * QK_PAD],
                                   _NT, preferred_element_type=F32)

        ahead = [scores(h) for h in range(MLA_LOOKAHEAD)]
        for h in range(MLA_HEADS):
            s = ahead.pop(0)
            if h + MLA_LOOKAHEAD < MLA_HEADS:
                ahead.append(scores(h + MLA_LOOKAHEAD))
            if masked:
                s = jnp.where(visible, s, NEG_BIG)
            m_old = m_ref[h]
            m_new = jnp.maximum(m_old, jnp.max(s, axis=0, keepdims=True))
            p = jnp.exp2(s - m_new)
            alpha = jnp.exp2(m_old - m_new)
            l_ref[h] = alpha * l_ref[h] + jnp.sum(p, axis=0, keepdims=True)
            acc_ref[h] = alpha * acc_ref[h] + jnp.dot(vt_ref[h * V_DIM:(h + 1) * V_DIM, :], p.astype(BF16),
                                                      preferred_element_type=F32)
            m_ref[h] = m_new

    @pl.when(ki < qi)
    def _():
        update(False)

    @pl.when(ki == qi)
    def _():
        update(True)
        for h in range(MLA_HEADS):
            o_ref[:, h * V_DIM:(h + 1) * V_DIM] = (acc_ref[h] / l_ref[h]).T.astype(BF16)


def _mla_prompt(q, k, vt, n_tokens, side_tables=()):
    tq = MLA_TQ
    assert n_tokens % tq == 0
    nq = n_tokens // tq
    pairs = [(a, b) for a in range(nq) for b in range(a + 1)]
    qi = jnp.asarray([p[0] for p in pairs], jnp.int32)
    ki = jnp.asarray([p[1] for p in pairs], jnp.int32)
    side_specs, side_shapes, side_steps = [], [], 0
    if side_tables:
        rows, cols = side_tables[0].shape
        assert all(s.shape == (rows, cols) for s in side_tables) and rows % MLA_SIDE_ROWS == 0
        side_steps = rows // MLA_SIDE_ROWS
        assert side_steps <= len(pairs)
        side_specs = [pl.BlockSpec((MLA_SIDE_ROWS, cols), lambda s, qi, ki: (jnp.minimum(s, side_steps - 1), 0))
                      for _ in side_tables]
        side_shapes = [jax.ShapeDtypeStruct((rows, cols), BF16) for _ in side_tables]
    grid_spec = pltpu.PrefetchScalarGridSpec(
        num_scalar_prefetch=2,
        grid=(len(pairs),),
        in_specs=[pl.BlockSpec((tq, MLA_HEADS * QK_PAD), lambda s, qi, ki: (qi[s], 0)),
                  pl.BlockSpec((tq, MLA_HEADS * QK_PAD), lambda s, qi, ki: (ki[s], 0)),
                  pl.BlockSpec((MLA_WIDTH, tq), lambda s, qi, ki: (0, ki[s])), *side_specs],
        out_specs=[pl.BlockSpec((tq, MLA_WIDTH), lambda s, qi, ki: (qi[s], 0)), *side_specs],
        scratch_shapes=[pltpu.VMEM((MLA_HEADS, 1, tq), F32), pltpu.VMEM((MLA_HEADS, 1, tq), F32),
                        pltpu.VMEM((MLA_HEADS, V_DIM, tq), F32)],
    )
    outs = pl.pallas_call(
        functools.partial(_mla_prompt_kernel, n_side=len(side_tables), side_steps=side_steps),
        grid_spec=grid_spec,
        out_shape=[jax.ShapeDtypeStruct((n_tokens, MLA_WIDTH), BF16), *side_shapes],
        compiler_params=pltpu.CompilerParams(dimension_semantics=("arbitrary",),
                                             vmem_limit_bytes=VMEM_LIMIT_BYTES),
        name="mla_prompt",
    )(qi, ki, q, k, vt, *side_tables)
    return outs[0], tuple(outs[1:])


def _mla_sample_kernel(q_ref, kn_ref, vn_ref, lat_ref, krc_ref, wukv_ref, o_ref):
    kvc = jnp.dot(lat_ref[0].astype(BF16), wukv_ref[...], preferred_element_type=F32)
    krc = krc_ref[0]
    for h in range(MLA_HEADS):
        q = q_ref[:, h * QK_PAD:(h + 1) * QK_PAD]
        knc = kvc[:, h * NOPE_DIM:(h + 1) * NOPE_DIM].astype(BF16)
        vc = kvc[:, MLA_HEADS * NOPE_DIM + h * V_DIM:MLA_HEADS * NOPE_DIM + (h + 1) * V_DIM].astype(BF16)
        s_c = (lax.dot_general(q[:, :NOPE_DIM], knc, _NT, preferred_element_type=F32)
               + lax.dot_general(q[:, NOPE_DIM:], krc, _NT, preferred_element_type=F32))
        s_n = lax.dot_general(q, kn_ref[:, h * QK_PAD:(h + 1) * QK_PAD], _NT, preferred_element_type=F32)
        m = jnp.maximum(jnp.max(s_c, axis=1, keepdims=True), jnp.max(s_n, axis=1, keepdims=True))
        p_c = jnp.exp2(s_c - m)
        p_n = jnp.exp2(s_n - m)
        l = jnp.sum(p_c, axis=1, keepdims=True) + jnp.sum(p_n, axis=1, keepdims=True)
        o = (jnp.dot(p_c.astype(BF16), vc, preferred_element_type=F32)
             + jnp.dot(p_n.astype(BF16), vn_ref[:, h * V_DIM:(h + 1) * V_DIM], preferred_element_type=F32))
        o_ref[:, h * V_DIM:(h + 1) * V_DIM] = (o / l).astype(BF16)


def _mla_sample(q, k, v, cache_lat, cache_kr, wukv_r, row0, n_new):
    bs, past, _ = cache_lat.shape
    assert row0 % n_new == 0
    blk0 = row0 // n_new
    krc = jnp.pad(cache_kr, ((0, 0), (0, 0), (0, LANES - ROPE_DIM))).astype(BF16)
    new = lambda w: pl.BlockSpec((n_new, w), lambda b: (blk0 + b, 0))
    return pl.pallas_call(
        _mla_sample_kernel,
        grid=(bs,),
        in_specs=[new(MLA_HEADS * QK_PAD), new(MLA_HEADS * QK_PAD), new(MLA_WIDTH),
                  pl.BlockSpec((1, past, KV_RANK), lambda b: (b, 0, 0)),
                  pl.BlockSpec((1, past, LANES), lambda b: (b, 0, 0)),
                  _const_spec(wukv_r.shape)],
        out_specs=pl.BlockSpec((n_new, MLA_WIDTH), lambda b: (b, 0)),
        out_shape=jax.ShapeDtypeStruct((bs * n_new, MLA_WIDTH), BF16),
        compiler_params=pltpu.CompilerParams(dimension_semantics=("arbitrary",),
                                             vmem_limit_bytes=VMEM_LIMIT_BYTES),
        name="mla_sample",
    )(q, k, v, cache_lat, krc, wukv_r)


GLA_SUB = 16
_TN = (((0,), (0,)), ((), ()))


def _cumsum_rows(x):
    n = x.shape[0]
    row = lax.broadcasted_iota(jnp.int32, x.shape, 0)
    s = 1
    while s < n:
        x = x + jnp.where(row >= s, pltpu.roll(x, s, axis=0), 0.0)
        s *= 2
    return x


def _gla_kernel(q_ref, k_ref, v_ref, lf_ref, gate_ref, s0_ref, g_ref, o_ref, send_ref, st_ref, *, n_chunks):
    c = pl.program_id(1)
    n_rows = q_ref.shape[0]
    n_sub = n_rows // GLA_SUB

    @pl.when(c == 0)
    def _():
        for h in range(GLA_HEADS):
            st_ref[h] = s0_ref[0, h].T

    sub_row = lax.broadcasted_iota(jnp.int32, (GLA_SUB, GLA_DK), 0)
    for h in range(GLA_HEADS):
        ks = slice(h * GLA_DK, (h + 1) * GLA_DK)
        vs = slice(h * GLA_DV, (h + 1) * GLA_DV)
        q = q_ref[:, ks] * (GLA_DK ** -0.5)
        k = k_ref[:, ks]
        v = v_ref[:, vs]
        vb = v.astype(BF16)
        b = _cumsum_rows(lf_ref[:, ks])
        b_last = b[n_rows - 1:n_rows]
        st = st_ref[h]
        o_inter = lax.dot_general((q * jnp.exp(b)).astype(BF16), st.astype(BF16), _NT,
                                  preferred_element_type=F32)
        outs = []
        for i in range(n_sub):
            r0 = i * GLA_SUB
            bi = b[r0:r0 + GLA_SUB]
            qi = q[r0:r0 + GLA_SUB]
            o_i = o_inter[r0:r0 + GLA_SUB]
            if i > 0:
                b_ref = b[r0 - 1:r0]
                qh = (qi * jnp.exp(bi - b_ref)).astype(BF16)
                kh = (k[:r0] * jnp.exp(b_ref - b[:r0])).astype(BF16)
                a_off = lax.dot_general(qh, kh, _NT, preferred_element_type=F32)
                o_i = o_i + jnp.dot(a_off.astype(BF16), vb[:r0], preferred_element_type=F32)
            for j in range(GLA_SUB):
                r = r0 + j
                decay = jnp.exp(jnp.where(sub_row >= j, bi - b[r:r + 1], -jnp.inf))
                a_col = jnp.sum(qi * k[r:r + 1] * decay, axis=1, keepdims=True)
                o_i = o_i + a_col * v[r:r + 1]
            outs.append(o_i)
        o = jnp.concatenate(outs, axis=0)

        kd = (k * jnp.exp(b_last - b)).astype(BF16)
        st_new = st * jnp.exp(b_last) + lax.dot_general(vb, kd, _TN, preferred_element_type=F32)
        st_ref[h] = st_new

        gate = gate_ref[:, vs]
        on = _rms(o, g_ref[...])
        o_ref[:, vs] = (on * (gate / (1.0 + jnp.exp(-gate)))).astype(BF16)

    @pl.when(c == n_chunks - 1)
    def _():
        for h in range(GLA_HEADS):
            send_ref[0, h] = st_ref[h].T


def _gla(gq, gk, gv, lf, gate, s0, gla_norm_g, row0, n_seq, n_chunks):
    assert row0 % CHUNK == 0
    blk0 = row0 // CHUNK
    row = lambda w: pl.BlockSpec((CHUNK, w), lambda b, c: (blk0 + b * n_chunks + c, 0))
    state = pl.BlockSpec((1, GLA_HEADS, GLA_DK, GLA_DV), lambda b, c: (b, 0, 0, 0))
    return pl.pallas_call(
        functools.partial(_gla_kernel, n_chunks=n_chunks),
        grid=(n_seq, n_chunks),
        in_specs=[row(GLA_KW), row(GLA_KW), row(GLA_WIDTH), row(GLA_KW), row(GLA_WIDTH), state,
                  pl.BlockSpec((1, GLA_DV), lambda b, c: (0, 0))],
        out_specs=[pl.BlockSpec((CHUNK, GLA_WIDTH), lambda b, c: (b * n_chunks + c, 0)), state],
        out_shape=[jax.ShapeDtypeStruct((n_seq * n_chunks * CHUNK, GLA_WIDTH), BF16),
                   jax.ShapeDtypeStruct((n_seq, GLA_HEADS, GLA_DK, GLA_DV), F32)],
        scratch_shapes=[pltpu.VMEM((GLA_HEADS, GLA_DV, GLA_DK), F32)],
        compiler_params=pltpu.CompilerParams(dimension_semantics=("arbitrary", "arbitrary"),
                                             vmem_limit_bytes=VMEM_LIMIT_BYTES),
        name="gla",
    )(gq, gk, gv, lf, gate, s0, gla_norm_g.reshape(1, -1))


OUT_TM = 256


def _layer_norm(y, g, b):
    mu = jnp.mean(y, axis=-1, keepdims=True)
    yc = y - mu
    var = jnp.mean(yc * yc, axis=-1, keepdims=True)
    return yc * lax.rsqrt(var + LN_EPS) * g + b


def _outproj_kernel(mlap_ref, mlas_ref, glap_ref, glas_ref, xp_ref, xs_ref, wo_ref, g_ref, b_ref, wq_ref,
                    h_ref, hb_ref, qp_ref, *, n_prompt_tiles):
    in_prompt = pl.program_id(0) < n_prompt_tiles
    mla = jnp.where(in_prompt, mlap_ref[...], mlas_ref[...])
    gla = jnp.where(in_prompt, glap_ref[...], glas_ref[...])
    x = jnp.where(in_prompt, xp_ref[...], xs_ref[...])
    mix = (jnp.dot(mla, wo_ref[:MLA_WIDTH], preferred_element_type=F32)
           + jnp.dot(gla, wo_ref[MLA_WIDTH:], preferred_element_type=F32))
    h = _layer_norm(DN_ALPHA * x + mix, g_ref[...], b_ref[...])
    hb = h.astype(BF16)
    h_ref[...] = h
    hb_ref[...] = hb
    qp = jnp.dot(hb, wq_ref[...], preferred_element_type=F32).astype(BF16)
    per_head = qp_ref.shape[2]
    for head in range(qp_ref.shape[0]):
        qp_ref[head] = qp[:, head * per_head:(head + 1) * per_head]


def _outproj(mla_p, mla_s, gla_p, gla_s, xp, xs, w_o_b, ln1_g, ln1_b, peer_wq_b):
    tp, ts = xp.shape[0], xs.shape[0]
    t = tp + ts
    tm = OUT_TM
    assert tp % tm == 0 and ts % tm == 0
    n_p = tp // tm
    row = lambda w: pl.BlockSpec((tm, w), lambda i: (i, 0))
    return pl.pallas_call(
        functools.partial(_outproj_kernel, n_prompt_tiles=n_p),
        grid=(t // tm,),
        in_specs=[*_two_stream_specs(tm, MLA_WIDTH, n_p), *_two_stream_specs(tm, GLA_WIDTH, n_p),
                  *_two_stream_specs(tm, D_MODEL, n_p), _const_spec(w_o_b.shape),
                  _const_spec((1, D_MODEL)), _const_spec((1, D_MODEL)), _const_spec(peer_wq_b.shape)],
        out_specs=[row(D_MODEL), row(D_MODEL),
                   pl.BlockSpec((PEER_HEADS, tm, 2 * PEER_HALF), lambda i: (0, i, 0))],
        out_shape=[jax.ShapeDtypeStruct((t, D_MODEL), F32), jax.ShapeDtypeStruct((t, D_MODEL), BF16),
                   jax.ShapeDtypeStruct((PEER_HEADS, t, 2 * PEER_HALF), BF16)],
        compiler_params=pltpu.CompilerParams(dimension_semantics=("arbitrary",),
                                             vmem_limit_bytes=VMEM_LIMIT_BYTES),
        name="outproj",
    )(mla_p, mla_s, gla_p, gla_s, xp, xs, w_o_b, ln1_g.reshape(1, -1), ln1_b.reshape(1, -1), peer_wq_b)


PEER_HEADS = 8
N_KEYS = 128
PEER_HALF = 128
PEER_TOPK = 16
ROUTE_TM = 256
CAND_WIDE_RANKS = 8
CAND_ROWS = PEER_TOPK + (CAND_WIDE_RANKS - 1) * SUBLANES + (PEER_TOPK - CAND_WIDE_RANKS)


def _first_max(x, row):
    n = x.shape[0]
    blocks = [(x[r:r + SUBLANES], row[r:r + SUBLANES]) for r in range(0, n, SUBLANES)]
    while len(blocks) > 1:
        merged = []
        for j in range(0, len(blocks) - 1, 2):
            (va, ia), (vb, ib) = blocks[j], blocks[j + 1]
            take = va >= vb
            merged.append((jnp.where(take, va, vb), jnp.where(take, ia, ib)))
        if len(blocks) % 2:
            merged.append(blocks[-1])
        blocks = merged
    v, i = blocks[0]
    m = jnp.max(v, axis=0, keepdims=True)
    pick = jnp.min(jnp.where(v == m, i, float(n)), axis=0, keepdims=True)
    return m, pick


def _extract_top(xs, row, n_out, on_pick):
    xs = list(xs)
    for k in range(n_out):
        for p, x in enumerate(xs):
            m, pick = _first_max(x, row)
            hit = row == pick
            on_pick(p, k, m, hit, pick)
            xs[p] = jnp.where(hit, -jnp.inf, x)


def _route_kernel(qp_ref, keys_ref, row_ref, g_ref, i1_ref, i2_ref, sv_ref, si_ref, gt_ref, et_ref):
    lax.fori_loop(0, PEER_HEADS,
                  functools.partial(_route_head, qp_ref, keys_ref, row_ref, sv_ref, si_ref, gt_ref, et_ref), 0)
    g_ref[...] = gt_ref[...].T
    expert = et_ref[...]
    key0 = jnp.floor(expert * (1.0 / N_KEYS))
    i1_ref[...] = key0.T
    i2_ref[...] = (expert - key0 * N_KEYS).T


def _route_head(qp_ref, keys_ref, row_ref, sv_ref, si_ref, gt_ref, et_ref, h, carry):
    tm = qp_ref.shape[1]
    q = qp_ref[h]
    scores = [lax.dot_general(keys_ref[half], q[:, half * PEER_HALF:(half + 1) * PEER_HALF], _NT,
                              preferred_element_type=F32) for half in range(2)]

    def keep(half, k, m, hit, pick):
        sv_ref[half, k:k + 1, :] = m
        si_ref[half, k:k + 1, :] = pick

    _extract_top(scores, row_ref[...], PEER_TOPK, keep)

    s1, s2 = sv_ref[0], sv_ref[1]
    i1, i2 = si_ref[0], si_ref[1]
    bc = lambda r, n: jnp.broadcast_to(r, (n, tm))
    wide = range(1, CAND_WIDE_RANKS)
    cand = jnp.concatenate([bc(s1[0:1], PEER_TOPK) + s2]
                           + [bc(s1[a:a + 1], SUBLANES) + s2[:SUBLANES] for a in wide]
                           + [s1[CAND_WIDE_RANKS:] + bc(s2[0:1], PEER_TOPK - CAND_WIDE_RANKS)], axis=0)
    i1 = i1 * N_KEYS
    c_expert = jnp.concatenate([bc(i1[0:1], PEER_TOPK) + i2]
                               + [bc(i1[a:a + 1], SUBLANES) + i2[:SUBLANES] for a in wide]
                               + [i1[CAND_WIDE_RANKS:] + bc(i2[0:1], PEER_TOPK - CAND_WIDE_RANKS)], axis=0)
    base = pl.multiple_of(h * PEER_TOPK, PEER_TOPK)

    def keep_pair(_, k, m, hit, pick):
        gt_ref[pl.ds(base + k, 1), :] = m
        et_ref[pl.ds(base + k, 1), :] = jnp.sum(jnp.where(hit, c_expert, 0.0), axis=0, keepdims=True)

    _extract_top([cand], row_ref[:CAND_ROWS], PEER_TOPK, keep_pair)

    sc = gt_ref[pl.ds(base, PEER_TOPK), :]
    e = jnp.exp(sc - sc[0:1])
    gt_ref[pl.ds(base, PEER_TOPK), :] = e / jnp.sum(e, axis=0, keepdims=True)
    return carry


def _route(qp, keys_b):
    t = qp.shape[1]
    tm = ROUTE_TM
    assert t % tm == 0
    n_sel = PEER_HEADS * PEER_TOPK
    out = pl.BlockSpec((tm, n_sel), lambda i: (i, 0))
    row_index = jnp.broadcast_to(jnp.arange(N_KEYS, dtype=F32)[:, None], (N_KEYS, tm))
    return pl.pallas_call(
        _route_kernel,
        grid=(t // tm,),
        in_specs=[pl.BlockSpec((PEER_HEADS, tm, 2 * PEER_HALF), lambda i: (0, i, 0)),
                  pl.BlockSpec(keys_b.shape, lambda i: (0, 0, 0)),
                  pl.BlockSpec((N_KEYS, tm), lambda i: (0, 0))],
        out_specs=[out, out, out],
        out_shape=[jax.ShapeDtypeStruct((t, n_sel), F32)] * 3,
        scratch_shapes=[pltpu.VMEM((2, PEER_TOPK, tm), F32), pltpu.VMEM((2, PEER_TOPK, tm), F32),
                        pltpu.VMEM((n_sel, tm), F32), pltpu.VMEM((n_sel, tm), F32)],
        compiler_params=pltpu.CompilerParams(dimension_semantics=("arbitrary",),
                                             vmem_limit_bytes=VMEM_LIMIT_BYTES),
        name="peer_route",
    )(qp, keys_b, row_index)


SCATTER_TB = 128
SCATTER_UNROLL = SUBLANES
W_ROW_TILE = SUBLANES


def _w_table_tokens(t):
    return -(-t // EXP_TM) * EXP_TM


def _scatter_kernel(g_ref, i1_ref, i2_ref, w_ref):
    tb = g_ref.shape[0]
    key = lax.broadcasted_iota(jnp.int32, (N_KEYS, g_ref.shape[1]), 0).astype(F32)

    def body(grp, carry):
        t0 = pl.multiple_of(grp * SCATTER_UNROLL, SCATTER_UNROLL)
        g8 = g_ref[pl.ds(t0, SCATTER_UNROLL), :]
        i18 = i1_ref[pl.ds(t0, SCATTER_UNROLL), :]
        i28 = i2_ref[pl.ds(t0, SCATTER_UNROLL), :]
        for u in range(SCATTER_UNROLL):
            g = jnp.broadcast_to(g8[u:u + 1], key.shape)
            i1 = jnp.broadcast_to(i18[u:u + 1], key.shape)
            i2 = jnp.broadcast_to(i28[u:u + 1], key.shape)
            a_t = jnp.where(key == i1, g, 0.0).astype(BF16)
            b_t = jnp.where(key == i2, 1.0, 0.0).astype(BF16)
            w = lax.dot_general(a_t, b_t, _NT, preferred_element_type=F32)
            w_ref[:, pl.ds(t0 + u, 1), :, :] = w.reshape(N_KEYS // W_ROW_TILE, 1, W_ROW_TILE, N_KEYS)
        return carry

    lax.fori_loop(0, tb // SCATTER_UNROLL, body, 0)


def _scatter(g, i1, i2):
    t = g.shape[0]
    tb = SCATTER_TB
    assert t % tb == 0
    sel = pl.BlockSpec((tb, g.shape[1]), lambda i: (i, 0))
    n_grp = N_KEYS // W_ROW_TILE
    return pl.pallas_call(
        _scatter_kernel,
        grid=(t // tb,),
        in_specs=[sel, sel, sel],
        out_specs=pl.BlockSpec((n_grp, tb, W_ROW_TILE, N_KEYS), lambda i: (0, i, 0, 0)),
        out_shape=jax.ShapeDtypeStruct((n_grp, _w_table_tokens(t), W_ROW_TILE, N_KEYS), F32),
        compiler_params=pltpu.CompilerParams(dimension_semantics=("arbitrary",),
                                             vmem_limit_bytes=VMEM_LIMIT_BYTES),
        name="peer_scatter",
    )(g, i1, i2)


EXP_TM = 1024
EXP_ROWS = 4
EXP_TE = EXP_ROWS * N_KEYS
EXP_SUBTILES = 4


def _gelu(x):
    return x * 0.5 * (1.0 + lax.erf(x * (2.0 ** -0.5)))


def _experts_kernel(hb_ref, w_ref, u_ref, v_ref, h_ref, g_ref, b_ref, y_ref):
    j = pl.program_id(1)
    tm = hb_ref.shape[0]

    @pl.when(j == 0)
    def _():
        y_ref[...] = DN_ALPHA * h_ref[...]

    row0 = (j % (W_ROW_TILE // EXP_ROWS)) * EXP_ROWS
    sub = tm // EXP_SUBTILES
    acts = [lax.dot_general(hb_ref[s * sub:(s + 1) * sub, :], u_ref[...], _NT, preferred_element_type=F32)
            for s in range(EXP_SUBTILES)]
    for s in range(EXP_SUBTILES):
        w = jnp.concatenate([w_ref[pl.ds(s * sub * W_ROW_TILE + row0 + r, sub, stride=W_ROW_TILE), :]
                             for r in range(EXP_ROWS)], axis=1)
        p = (w * _gelu(acts[s])).astype(BF16)
        y_ref[s * sub:(s + 1) * sub, :] += jnp.dot(p, v_ref[...], preferred_element_type=F32)

    @pl.when(j == pl.num_programs(1) - 1)
    def _():
        y_ref[...] = _layer_norm(y_ref[...], g_ref[...], b_ref[...])


def _experts(hb, h, w_table, u_b, v_b, ln2_g, ln2_b, row0, n_rows):
    tm = min(EXP_TM, n_rows)
    assert n_rows % tm == 0 and row0 % tm == 0 and tm % (EXP_SUBTILES * 2 * SUBLANES) == 0
    nt = n_rows // tm
    blk0 = row0 // tm
    ne = u_b.shape[0] // EXP_TE
    steps_per_group = W_ROW_TILE // EXP_ROWS
    tiles_per_group = w_table.shape[1] // tm
    w2d = w_table.reshape(-1, N_KEYS)
    once = lambda: pl.BlockSpec((tm, D_MODEL), lambda i, j: (blk0 + i, 0), pipeline_mode=pl.Buffered(1))
    return pl.pallas_call(
        _experts_kernel,
        grid=(nt, ne),
        in_specs=[once(),
                  pl.BlockSpec((tm * W_ROW_TILE, N_KEYS),
                               lambda i, j: ((j // steps_per_group) * tiles_per_group + blk0 + i, 0)),
                  pl.BlockSpec((EXP_TE, D_MODEL), lambda i, j: (j, 0)),
                  pl.BlockSpec((EXP_TE, D_MODEL), lambda i, j: (j, 0)),
                  once(), pl.BlockSpec((1, D_MODEL), lambda i, j: (0, 0)),
                  pl.BlockSpec((1, D_MODEL), lambda i, j: (0, 0))],
        out_specs=pl.BlockSpec((tm, D_MODEL), lambda i, j: (i, 0)),
        out_shape=jax.ShapeDtypeStruct((n_rows, D_MODEL), F32),
        compiler_params=pltpu.CompilerParams(dimension_semantics=("arbitrary", "arbitrary"),
                                             vmem_limit_bytes=VMEM_LIMIT_BYTES),
        name="peer_experts",
    )(hb, w2d, u_b, v_b, h, ln2_g.reshape(1, -1), ln2_b.reshape(1, -1))


def kernel(x_prompt, x_sample, cache_kv_latent, cache_k_rope, state_gla, w_in, q_norm_g, w_uq, kv_norm_g, w_ukv,
           w_gk2, b_gk, gla_norm_g, w_o, ln1_g, ln1_b, peer_wq, peer_keys, peer_u, peer_v, ln2_g, ln2_b):
    bp, sp, _ = x_prompt.shape
    bs, ss, _ = x_sample.shape
    past = cache_kv_latent.shape[1]
    tp, ts = bp * sp, bs * ss
    xp = x_prompt.reshape(tp, D_MODEL)
    xs = x_sample.reshape(ts, D_MODEL)
    pos = jnp.concatenate([jnp.tile(jnp.arange(sp, dtype=jnp.int32), bp),
                           jnp.tile(past + jnp.arange(ss, dtype=jnp.int32), bs)])
    cos, sin = _rope_tables(pos)
    w_in_r, wq_r, wukv_r, wvt_r, wgk_r = _mixer_weights(w_in, w_uq, w_ukv, w_gk2)
    q, k, v, vt, ckv_p, ckv_s, kr_p, kr_s, gq, gk, gv, gate, lf = _mixer(
        xp, xs, cos, sin, w_in_r, wq_r, wukv_r, wvt_r, wgk_r, q_norm_g, kv_norm_g, b_gk)
    o_p, (u_b, v_b) = _mla_prompt(q, k, vt, tp, side_tables=(peer_u, peer_v))
    o_s = _mla_sample(q, k, v, cache_kv_latent, cache_k_rope, wukv_r, tp, ss)
    g_p, st_p = _gla(gq, gk, gv, lf, gate, jnp.zeros((bp,) + state_gla.shape[1:], F32), gla_norm_g,
                     0, bp, sp // CHUNK)
    g_s, st_s = _gla(gq, gk, gv, lf, gate, state_gla, gla_norm_g, tp, bs, ss // CHUNK)

    h, hb, qp = _outproj(o_p, o_s, g_p, g_s, xp, xs, w_o.astype(BF16), ln1_g, ln1_b, peer_wq.astype(BF16))
    gates, i1, i2 = _route(qp, peer_keys.astype(BF16))
    w_table = _scatter(gates, i1, i2)
    y_p = _experts(hb, h, w_table, u_b, v_b, ln2_g, ln2_b, 0, tp)
    y_s = _experts(hb, h, w_table, u_b, v_b, ln2_g, ln2_b, tp, ts)

    dt = x_prompt.dtype
    return (y_p.reshape(bp, sp, D_MODEL), y_s.reshape(bs, ss, D_MODEL),
            ckv_p.reshape(bp, sp, KV_RANK), kr_p.reshape(bp, sp, ROPE_DIM), st_p.astype(dt),
            ckv_s.reshape(bs, ss, KV_RANK), kr_s.reshape(bs, ss, ROPE_DIM), st_s.astype(dt))
```

```python
import functools
import math

import jax
import jax.numpy as jnp
from jax import lax
from jax.experimental import pallas as pl
from jax.experimental.pallas import tpu as pltpu

F32 = jnp.float32
BF16 = jnp.bfloat16
FP8 = jnp.float8_e4m3fn
FP8_ROW_TILE = 32
FP8_TARGET = 128.0
FP8_TINY = 1e-30


def _fp8_scaled(x, amax):
    return (x * (FP8_TARGET / amax)).astype(FP8)

LANES = 128
SUBLANES = 8
VMEM_LIMIT_BYTES = 60 * 1024 * 1024

D_MODEL = 2048
CHUNK = 64
MLA_HEADS = 8
Q_RANK = 512
KV_RANK = 256
NOPE_DIM = 128
ROPE_DIM = 64
V_DIM = 128
ROPE_THETA = 10000.0
MLA_SCALE = (NOPE_DIM + ROPE_DIM) ** -0.5
Q_SCALE = MLA_SCALE * math.log2(math.e)
QK_PAD = 256
GLA_HEADS = 4
GLA_DK = 128
GLA_DV = 256
GATE_RANK = 16
GATE_NORMALIZER = 16.0
GLA_KW = GLA_HEADS * GLA_DK
GLA_WIDTH = GLA_HEADS * GLA_DV
MLA_WIDTH = MLA_HEADS * V_DIM
DN_ALPHA = 2.0 ** 0.25
LN_EPS = 1e-5
RMS_EPS = 1e-6

Z_CQ = 0
Z_CKV = Z_CQ + Q_RANK
Z_KR = Z_CKV + KV_RANK
Z_KROT = Z_KR + LANES
Z_GQ = Z_KROT + LANES
Z_GK = Z_GQ + GLA_KW
Z_GV = Z_GK + GLA_KW
Z_GATE = Z_GV + GLA_WIDTH
Z_WIDTH = Z_GATE + GLA_WIDTH


def _const_spec(shape):
    nd = len(shape)
    return pl.BlockSpec(shape, lambda *_: (0,) * nd, pipeline_mode=pl.Buffered(1))


def _rms(x, g):
    return x * lax.rsqrt(jnp.mean(x * x, axis=-1, keepdims=True) + RMS_EPS) * g


MIX_TM = 256


def _two_stream_specs(tm, width, n_first):
    return (pl.BlockSpec((tm, width), lambda i: (jnp.minimum(i, n_first - 1), 0)),
            pl.BlockSpec((tm, width), lambda i: (jnp.maximum(i - n_first, 0), 0)))


def _mixer_kernel(xp_ref, xs_ref, w_in_ref, wq_ref, wukv_ref, wvt_ref, wgk_ref, qg_ref, kvg_ref, bgk_ref,
                  cos_ref, sin_ref, q_ref, k_ref, v_ref, vt_ref, ckvp_ref, ckvs_ref, krp_ref, krs_ref,
                  gq_ref, gk_ref, gv_ref, gate_ref, lf_ref, *, n_prompt_tiles):
    in_prompt = pl.program_id(0) < n_prompt_tiles
    xb = jnp.where(in_prompt, xp_ref[...], xs_ref[...]).astype(BF16)
    cos = cos_ref[...]
    sin = sin_ref[...]

    z_lat = jnp.dot(xb, w_in_ref[:, Z_CQ:Z_GQ], preferred_element_type=F32)
    cq = _rms(z_lat[:, Z_CQ:Z_CKV], qg_ref[...])
    c_kv = _rms(z_lat[:, Z_CKV:Z_KR], kvg_ref[...])
    krga = z_lat[:, Z_KR:Z_KROT]
    k_rope = krga * cos + z_lat[:, Z_KROT:Z_GQ] * sin

    @pl.when(in_prompt)
    def _():
        ckvp_ref[...] = c_kv
        krp_ref[...] = k_rope[:, :ROPE_DIM]

    @pl.when(jnp.logical_not(in_prompt))
    def _():
        ckvs_ref[...] = c_kv
        krs_ref[...] = k_rope[:, :ROPE_DIM]

    k_rope_b = k_rope.astype(BF16)

    q = jnp.dot(cq.astype(BF16), wq_ref[...], preferred_element_type=F32)
    kv = jnp.dot(c_kv.astype(BF16), wukv_ref[...], preferred_element_type=F32)
    for h in range(MLA_HEADS):
        lo = h * QK_PAD
        q_ref[:, lo:lo + NOPE_DIM] = (q[:, lo:lo + NOPE_DIM] * Q_SCALE).astype(BF16)
        rot = q[:, MLA_HEADS * QK_PAD + h * LANES:MLA_HEADS * QK_PAD + (h + 1) * LANES]
        q_ref[:, lo + NOPE_DIM:lo + QK_PAD] = (
            (q[:, lo + NOPE_DIM:lo + QK_PAD] * cos + rot * sin) * Q_SCALE).astype(BF16)
        k_ref[:, lo:lo + NOPE_DIM] = kv[:, h * NOPE_DIM:(h + 1) * NOPE_DIM].astype(BF16)
        k_ref[:, lo + NOPE_DIM:lo + QK_PAD] = k_rope_b
    v_ref[...] = kv[:, MLA_HEADS * NOPE_DIM:].astype(BF16)
    vt_ref[...] = jnp.dot(wvt_ref[...], c_kv.T.astype(BF16), preferred_element_type=F32).astype(BF16)

    pre = jnp.dot(krga.astype(BF16), wgk_ref[...], preferred_element_type=F32) + bgk_ref[...]
    lf_ref[...] = (jnp.minimum(pre, 0.0) - jnp.log1p(jnp.exp(-jnp.abs(pre)))) * (1.0 / GATE_NORMALIZER)

    gq_ref[...] = jnp.dot(xb, w_in_ref[:, Z_GQ:Z_GK], preferred_element_type=F32)
    gk_ref[...] = jnp.dot(xb, w_in_ref[:, Z_GK:Z_GV], preferred_element_type=F32)
    gv_ref[...] = jnp.dot(xb, w_in_ref[:, Z_GV:Z_GATE], preferred_element_type=F32)
    gate_ref[...] = jnp.dot(xb, w_in_ref[:, Z_GATE:Z_WIDTH], preferred_element_type=F32)


def _rotate_half_cols(w):
    half = ROPE_DIM // 2
    return jnp.concatenate([-w[..., half:], w[..., :half]], axis=-1)


def _mixer_weights(w_in, w_uq, w_ukv, w_gk2):
    pts = []
    acc = 0
    for s in (Q_RANK, KV_RANK, ROPE_DIM, GLA_KW, GLA_KW, GLA_WIDTH, GLA_WIDTH):
        acc += s
        pts.append(acc)
    cq, ckv, kr, gq, gk, gv, gate, ga = jnp.split(w_in, pts, axis=1)
    d = w_in.shape[0]
    w_in_r = jnp.concatenate(
        [cq, ckv, kr, ga, jnp.zeros((d, LANES - ROPE_DIM - GATE_RANK), F32),
         _rotate_half_cols(kr), jnp.zeros((d, LANES - ROPE_DIM), F32), gq, gk, gv, gate], axis=1).astype(BF16)

    wq = w_uq.reshape(Q_RANK, MLA_HEADS, NOPE_DIM + ROPE_DIM)
    wq_rope = wq[..., NOPE_DIM:]
    pad = jnp.zeros((Q_RANK, MLA_HEADS, QK_PAD - NOPE_DIM - ROPE_DIM), F32)
    wq_main = jnp.concatenate([wq, pad], axis=-1).reshape(Q_RANK, MLA_HEADS * QK_PAD)
    wq_rot = jnp.concatenate([_rotate_half_cols(wq_rope), pad], axis=-1).reshape(Q_RANK, MLA_HEADS * LANES)
    wq_r = jnp.concatenate([wq_main, wq_rot], axis=1).astype(BF16)

    wkv = w_ukv.reshape(KV_RANK, MLA_HEADS, NOPE_DIM + V_DIM)
    wukv_r = jnp.concatenate([wkv[..., :NOPE_DIM].reshape(KV_RANK, -1),
                              wkv[..., NOPE_DIM:].reshape(KV_RANK, -1)], axis=1).astype(BF16)

    wvt_r = wkv[..., NOPE_DIM:].reshape(KV_RANK, -1).T.astype(BF16)

    wgk_r = jnp.zeros((LANES, GLA_KW), F32).at[ROPE_DIM:ROPE_DIM + GATE_RANK].set(w_gk2).astype(BF16)
    return w_in_r, wq_r, wukv_r, wvt_r, wgk_r


def _rope_tables(pos):
    half = ROPE_DIM // 2
    freqs = ROPE_THETA ** (-jnp.arange(half, dtype=F32) / half)
    ang = pos.astype(F32)[:, None] * freqs[None, :]
    zeros = jnp.zeros((pos.shape[0], LANES - ROPE_DIM), F32)
    cos = jnp.concatenate([jnp.cos(ang), jnp.cos(ang), zeros], axis=1)
    sin = jnp.concatenate([jnp.sin(ang), jnp.sin(ang), zeros], axis=1)
    return cos, sin


def _mixer(xp, xs, cos, sin, w_in_r, wq_r, wukv_r, wvt_r, wgk_r, q_norm_g, kv_norm_g, b_gk):
    tp, ts = xp.shape[0], xs.shape[0]
    t = tp + ts
    tm = MIX_TM
    assert tp % tm == 0 and ts % tm == 0
    n_p = tp // tm
    row = lambda w: pl.BlockSpec((tm, w), lambda i: (i, 0))
    col = pl.BlockSpec((MLA_WIDTH, tm), lambda i: (0, i))
    full = lambda w, dt: (row(w), jax.ShapeDtypeStruct((t, w), dt))
    ckv_p, ckv_s = _two_stream_specs(tm, KV_RANK, n_p)
    kr_p, kr_s = _two_stream_specs(tm, ROPE_DIM, n_p)
    outs = [full(MLA_HEADS * QK_PAD, BF16), full(MLA_HEADS * QK_PAD, BF16), full(MLA_WIDTH, BF16),
            (col, jax.ShapeDtypeStruct((MLA_WIDTH, t), BF16)),
            (ckv_p, jax.ShapeDtypeStruct((tp, KV_RANK), F32)), (ckv_s, jax.ShapeDtypeStruct((ts, KV_RANK), F32)),
            (kr_p, jax.ShapeDtypeStruct((tp, ROPE_DIM), F32)), (kr_s, jax.ShapeDtypeStruct((ts, ROPE_DIM), F32)),
            full(GLA_KW, F32), full(GLA_KW, F32), full(GLA_WIDTH, F32), full(GLA_WIDTH, F32), full(GLA_KW, F32)]
    return pl.pallas_call(
        functools.partial(_mixer_kernel, n_prompt_tiles=n_p),
        grid=(t // tm,),
        in_specs=[*_two_stream_specs(tm, D_MODEL, n_p), _const_spec(w_in_r.shape), _const_spec(wq_r.shape),
                  _const_spec(wukv_r.shape), _const_spec(wvt_r.shape), _const_spec(wgk_r.shape),
                  _const_spec((1, Q_RANK)), _const_spec((1, KV_RANK)), _const_spec((1, GLA_KW)),
                  row(LANES), row(LANES)],
        out_specs=[o[0] for o in outs],
        out_shape=[o[1] for o in outs],
        compiler_params=pltpu.CompilerParams(dimension_semantics=("arbitrary",),
                                             vmem_limit_bytes=VMEM_LIMIT_BYTES),
        name="mixer",
    )(xp, xs, w_in_r, wq_r, wukv_r, wvt_r, wgk_r, q_norm_g.reshape(1, -1), kv_norm_g.reshape(1, -1),
      b_gk.reshape(1, -1), cos, sin)


MLA_TQ = 512
MLA_LOOKAHEAD = 2
MLA_SIDE_ROWS = 128
NEG_BIG = -1e30
_NT = (((1,), (1,)), ((), ()))


def _mla_prompt_kernel(qi_ref, ki_ref, q_ref, k_ref, vt_ref, u_ref, v_ref, o_ref, u8_ref, uinv_ref, vb_ref,
                       m_ref, l_ref, acc_ref, *, side_steps):
    step = pl.program_id(0)
    qi = qi_ref[step]
    ki = ki_ref[step]
    tq = q_ref.shape[0]
    tk = k_ref.shape[0]

    @pl.when(step < side_steps)
    def _():
        u = u_ref[...]
        amax = jnp.max(jnp.max(jnp.abs(u), axis=0, keepdims=True), axis=1, keepdims=True)
        amax = jnp.maximum(amax, FP8_TINY)
        u8_ref[...] = _fp8_scaled(u, amax)
        uinv_ref[0] = jnp.broadcast_to(amax * (1.0 / FP8_TARGET), uinv_ref.shape[1:])
        vb_ref[...] = v_ref[...].astype(BF16)

    @pl.when(ki == 0)
    def _():
        m_ref[...] = jnp.full(m_ref.shape, NEG_BIG, F32)
        l_ref[...] = jnp.zeros(l_ref.shape, F32)
        acc_ref[...] = jnp.zeros(acc_ref.shape, F32)

    def update(masked):
        if masked:
            key_chunk = lax.broadcasted_iota(jnp.int32, (tk, tq), 0) // CHUNK
            qry_chunk = lax.broadcasted_iota(jnp.int32, (tk, tq), 1) // CHUNK
            visible = key_chunk <= qry_chunk
        def scores(h):
            return lax.dot_general(k_ref[:, h * QK_PAD:(h + 1) * QK_PAD], q_ref[:, h * QK_PAD:(h + 1) * QK_PAD],
                                   _NT, preferred_element_type=F32)

        ahead = [scores(h) for h in range(MLA_LOOKAHEAD)]
        for h in range(MLA_HEADS):
            s = ahead.pop(0)
            if h + MLA_LOOKAHEAD < MLA_HEADS:
                ahead.append(scores(h + MLA_LOOKAHEAD))
            if masked:
                s = jnp.where(visible, s, NEG_BIG)
            m_old = m_ref[h]
            m_new = jnp.maximum(m_old, jnp.max(s, axis=0, keepdims=True))
            p = jnp.exp2(s - m_new)
            alpha = jnp.exp2(m_old - m_new)
            l_ref[h] = alpha * l_ref[h] + jnp.sum(p, axis=0, keepdims=True)
            acc_ref[h] = alpha * acc_ref[h] + jnp.dot(vt_ref[h * V_DIM:(h + 1) * V_DIM, :], p.astype(BF16),
                                                      preferred_element_type=F32)
            m_ref[h] = m_new

    @pl.when(ki < qi)
    def _():
        update(False)

    @pl.when(ki == qi)
    def _():
        update(True)
        for h in range(MLA_HEADS):
            o_ref[:, h * V_DIM:(h + 1) * V_DIM] = (acc_ref[h] / l_ref[h]).T.astype(BF16)


def _mla_prompt(q, k, vt, n_tokens, table_u, table_v):
    tq = MLA_TQ
    assert n_tokens % tq == 0
    nq = n_tokens // tq
    pairs = [(a, b) for a in range(nq) for b in range(a + 1)]
    qi = jnp.asarray([p[0] for p in pairs], jnp.int32)
    ki = jnp.asarray([p[1] for p in pairs], jnp.int32)
    rows, cols = table_u.shape
    assert table_v.shape == (rows, cols) and rows % MLA_SIDE_ROWS == 0
    side_steps = rows // MLA_SIDE_ROWS
    assert side_steps <= len(pairs)
    side = lambda: pl.BlockSpec((MLA_SIDE_ROWS, cols), lambda s, qi, ki: (jnp.minimum(s, side_steps - 1), 0))
    grid_spec = pltpu.PrefetchScalarGridSpec(
        num_scalar_prefetch=2,
        grid=(len(pairs),),
        in_specs=[pl.BlockSpec((tq, MLA_HEADS * QK_PAD), lambda s, qi, ki: (qi[s], 0)),
                  pl.BlockSpec((tq, MLA_HEADS * QK_PAD), lambda s, qi, ki: (ki[s], 0)),
                  pl.BlockSpec((MLA_WIDTH, tq), lambda s, qi, ki: (0, ki[s])), side(), side()],
        out_specs=[pl.BlockSpec((tq, MLA_WIDTH), lambda s, qi, ki: (qi[s], 0)), side(),
                   pl.BlockSpec((1, SUBLANES, LANES), lambda s, qi, ki: (jnp.minimum(s, side_steps - 1), 0, 0)),
                   side()],
        scratch_shapes=[pltpu.VMEM((MLA_HEADS, 1, tq), F32), pltpu.VMEM((MLA_HEADS, 1, tq), F32),
                        pltpu.VMEM((MLA_HEADS, V_DIM, tq), F32)],
    )
    o, u8, u_inv, v_b = pl.pallas_call(
        functools.partial(_mla_prompt_kernel, side_steps=side_steps),
        grid_spec=grid_spec,
        out_shape=[jax.ShapeDtypeStruct((n_tokens, MLA_WIDTH), BF16), jax.ShapeDtypeStruct((rows, cols), FP8),
                   jax.ShapeDtypeStruct((side_steps, SUBLANES, LANES), F32),
                   jax.ShapeDtypeStruct((rows, cols), BF16)],
        compiler_params=pltpu.CompilerParams(dimension_semantics=("arbitrary",),
                                             vmem_limit_bytes=VMEM_LIMIT_BYTES),
        name="mla_prompt",
    )(qi, ki, q, k, vt, table_u, table_v)
    return o, (u8, u_inv, v_b)


def _mla_sample_kernel(q_ref, kn_ref, vn_ref, lat_ref, krc_ref, wukv_ref, o_ref):
    kvc = jnp.dot(lat_ref[0].astype(BF16), wukv_ref[...], preferred_element_type=F32)
    krc = krc_ref[0]
    for h in range(MLA_HEADS):
        q = q_ref[:, h * QK_PAD:(h + 1) * QK_PAD]
        knc = kvc[:, h * NOPE_DIM:(h + 1) * NOPE_DIM].astype(BF16)
        vc = kvc[:, MLA_HEADS * NOPE_DIM + h * V_DIM:MLA_HEADS * NOPE_DIM + (h + 1) * V_DIM].astype(BF16)
        s_c = (lax.dot_general(q[:, :NOPE_DIM], knc, _NT, preferred_element_type=F32)
               + lax.dot_general(q[:, NOPE_DIM:], krc, _NT, preferred_element_type=F32))
        s_n = lax.dot_general(q, kn_ref[:, h * QK_PAD:(h + 1) * QK_PAD], _NT, preferred_element_type=F32)
        m = jnp.maximum(jnp.max(s_c, axis=1, keepdims=True), jnp.max(s_n, axis=1, keepdims=True))
        p_c = jnp.exp2(s_c - m)
        p_n = jnp.exp2(s_n - m)
        l = jnp.sum(p_c, axis=1, keepdims=True) + jnp.sum(p_n, axis=1, keepdims=True)
        o = (jnp.dot(p_c.astype(BF16), vc, preferred_element_type=F32)
             + jnp.dot(p_n.astype(BF16), vn_ref[:, h * V_DIM:(h + 1) * V_DIM], preferred_element_type=F32))
        o_ref[:, h * V_DIM:(h + 1) * V_DIM] = (o / l).astype(BF16)


def _mla_sample(q, k, v, cache_lat, cache_kr, wukv_r, row0, n_new):
    bs, past, _ = cache_lat.shape
    assert row0 % n_new == 0
    blk0 = row0 // n_new
    krc = jnp.pad(cache_kr, ((0, 0), (0, 0), (0, LANES - ROPE_DIM))).astype(BF16)
    new = lambda w: pl.BlockSpec((n_new, w), lambda b: (blk0 + b, 0))
    return pl.pallas_call(
        _mla_sample_kernel,
        grid=(bs,),
        in_specs=[new(MLA_HEADS * QK_PAD), new(MLA_HEADS * QK_PAD), new(MLA_WIDTH),
                  pl.BlockSpec((1, past, KV_RANK), lambda b: (b, 0, 0)),
                  pl.BlockSpec((1, past, LANES), lambda b: (b, 0, 0)),
                  _const_spec(wukv_r.shape)],
        out_specs=pl.BlockSpec((n_new, MLA_WIDTH), lambda b: (b, 0)),
        out_shape=jax.ShapeDtypeStruct((bs * n_new, MLA_WIDTH), BF16),
        compiler_params=pltpu.CompilerParams(dimension_semantics=("arbitrary",),
                                             vmem_limit_bytes=VMEM_LIMIT_BYTES),
        name="mla_sample",
    )(q, k, v, cache_lat, krc, wukv_r)


GLA_SUB = 16
_TN = (((0,), (0,)), ((), ()))


def _cumsum_rows(x):
    n = x.shape[0]
    row = lax.broadcasted_iota(jnp.int32, x.shape, 0)
    s = 1
    while s < n:
        x = x + jnp.where(row >= s, pltpu.roll(x, s, axis=0), 0.0)
        s *= 2
    return x


def _gla_kernel(q_ref, k_ref, v_ref, lf_ref, gate_ref, s0_ref, g_ref, o_ref, send_ref, st_ref, *, n_chunks):
    c = pl.program_id(1)
    n_rows = q_ref.shape[0]
    n_sub = n_rows // GLA_SUB

    @pl.when(c == 0)
    def _():
        for h in range(GLA_HEADS):
            st_ref[h] = s0_ref[0, h].T

    sub_row = lax.broadcasted_iota(jnp.int32, (GLA_SUB, GLA_DK), 0)
    for h in range(GLA_HEADS):
        ks = slice(h * GLA_DK, (h + 1) * GLA_DK)
        vs = slice(h * GLA_DV, (h + 1) * GLA_DV)
        q = q_ref[:, ks] * (GLA_DK ** -0.5)
        k = k_ref[:, ks]
        v = v_ref[:, vs]
        vb = v.astype(BF16)
        b = _cumsum_rows(lf_ref[:, ks])
        b_last = b[n_rows - 1:n_rows]
        st = st_ref[h]
        o_inter = lax.dot_general((q * jnp.exp(b)).astype(BF16), st.astype(BF16), _NT,
                                  preferred_element_type=F32)
        outs = []
        for i in range(n_sub):
            r0 = i * GLA_SUB
            bi = b[r0:r0 + GLA_SUB]
            qi = q[r0:r0 + GLA_SUB]
            o_i = o_inter[r0:r0 + GLA_SUB]
            if i > 0:
                b_ref = b[r0 - 1:r0]
                qh = (qi * jnp.exp(bi - b_ref)).astype(BF16)
                kh = (k[:r0] * jnp.exp(b_ref - b[:r0])).astype(BF16)
                a_off = lax.dot_general(qh, kh, _NT, preferred_element_type=F32)
                o_i = o_i + jnp.dot(a_off.astype(BF16), vb[:r0], preferred_element_type=F32)
            for j in range(GLA_SUB):
                r = r0 + j
                decay = jnp.exp(jnp.where(sub_row >= j, bi - b[r:r + 1], -jnp.inf))
                a_col = jnp.sum(qi * k[r:r + 1] * decay, axis=1, keepdims=True)
                o_i = o_i + a_col * v[r:r + 1]
            outs.append(o_i)
        o = jnp.concatenate(outs, axis=0)

        kd = (k * jnp.exp(b_last - b)).astype(BF16)
        st_new = st * jnp.exp(b_last) + lax.dot_general(vb, kd, _TN, preferred_element_type=F32)
        st_ref[h] = st_new

        gate = gate_ref[:, vs]
        on = _rms(o, g_ref[...])
        o_ref[:, vs] = (on * (gate / (1.0 + jnp.exp(-gate)))).astype(BF16)

    @pl.when(c == n_chunks - 1)
    def _():
        for h in range(GLA_HEADS):
            send_ref[0, h] = st_ref[h].T


def _gla(gq, gk, gv, lf, gate, s0, gla_norm_g, row0, n_seq, n_chunks):
    assert row0 % CHUNK == 0
    blk0 = row0 // CHUNK
    row = lambda w: pl.BlockSpec((CHUNK, w), lambda b, c: (blk0 + b * n_chunks + c, 0))
    state = pl.BlockSpec((1, GLA_HEADS, GLA_DK, GLA_DV), lambda b, c: (b, 0, 0, 0))
    return pl.pallas_call(
        functools.partial(_gla_kernel, n_chunks=n_chunks),
        grid=(n_seq, n_chunks),
        in_specs=[row(GLA_KW), row(GLA_KW), row(GLA_WIDTH), row(GLA_KW), row(GLA_WIDTH), state,
                  pl.BlockSpec((1, GLA_DV), lambda b, c: (0, 0))],
        out_specs=[pl.BlockSpec((CHUNK, GLA_WIDTH), lambda b, c: (b * n_chunks + c, 0)), state],
        out_shape=[jax.ShapeDtypeStruct((n_seq * n_chunks * CHUNK, GLA_WIDTH), BF16),
                   jax.ShapeDtypeStruct((n_seq, GLA_HEADS, GLA_DK, GLA_DV), F32)],
        scratch_shapes=[pltpu.VMEM((GLA_HEADS, GLA_DV, GLA_DK), F32)],
        compiler_params=pltpu.CompilerParams(dimension_semantics=("arbitrary", "arbitrary"),
                                             vmem_limit_bytes=VMEM_LIMIT_BYTES),
        name="gla",
    )(gq, gk, gv, lf, gate, s0, gla_norm_g.reshape(1, -1))


OUT_TM = 256


def _layer_norm(y, g, b):
    mu = jnp.mean(y, axis=-1, keepdims=True)
    yc = y - mu
    var = jnp.mean(yc * yc, axis=-1, keepdims=True)
    return yc * lax.rsqrt(var + LN_EPS) * g + b


def _outproj_kernel(mlap_ref, mlas_ref, glap_ref, glas_ref, xp_ref, xs_ref, wo_ref, g_ref, b_ref, wq_ref,
                    h_ref, h8_ref, hinv_ref, qp_ref, *, n_prompt_tiles):
    in_prompt = pl.program_id(0) < n_prompt_tiles
    mla = jnp.where(in_prompt, mlap_ref[...], mlas_ref[...])
    gla = jnp.where(in_prompt, glap_ref[...], glas_ref[...])
    x = jnp.where(in_prompt, xp_ref[...], xs_ref[...])
    mix = (jnp.dot(mla, wo_ref[:MLA_WIDTH], preferred_element_type=F32)
           + jnp.dot(gla, wo_ref[MLA_WIDTH:], preferred_element_type=F32))
    h = _layer_norm(DN_ALPHA * x + mix, g_ref[...], b_ref[...])
    h_ref[...] = h
    amax = jnp.maximum(jnp.max(jnp.abs(h), axis=1, keepdims=True), FP8_TINY)
    h8_ref[...] = _fp8_scaled(h, amax)
    hinv_ref[...] = jnp.broadcast_to(amax * (1.0 / FP8_TARGET), hinv_ref.shape)
    qp = jnp.dot(h.astype(BF16), wq_ref[...], preferred_element_type=F32).astype(BF16)
    per_head = qp_ref.shape[2]
    for head in range(qp_ref.shape[0]):
        qp_ref[head] = qp[:, head * per_head:(head + 1) * per_head]


def _outproj(mla_p, mla_s, gla_p, gla_s, xp, xs, w_o_b, ln1_g, ln1_b, peer_wq_b):
    tp, ts = xp.shape[0], xs.shape[0]
    t = tp + ts
    tm = OUT_TM
    assert tp % tm == 0 and ts % tm == 0
    n_p = tp // tm
    row = lambda w: pl.BlockSpec((tm, w), lambda i: (i, 0))
    return pl.pallas_call(
        functools.partial(_outproj_kernel, n_prompt_tiles=n_p),
        grid=(t // tm,),
        in_specs=[*_two_stream_specs(tm, MLA_WIDTH, n_p), *_two_stream_specs(tm, GLA_WIDTH, n_p),
                  *_two_stream_specs(tm, D_MODEL, n_p), _const_spec(w_o_b.shape),
                  _const_spec((1, D_MODEL)), _const_spec((1, D_MODEL)), _const_spec(peer_wq_b.shape)],
        out_specs=[row(D_MODEL), row(D_MODEL), row(LANES),
                   pl.BlockSpec((PEER_HEADS, tm, 2 * PEER_HALF), lambda i: (0, i, 0))],
        out_shape=[jax.ShapeDtypeStruct((t, D_MODEL), F32), jax.ShapeDtypeStruct((t, D_MODEL), FP8),
                   jax.ShapeDtypeStruct((t, LANES), F32),
                   jax.ShapeDtypeStruct((PEER_HEADS, t, 2 * PEER_HALF), BF16)],
        compiler_params=pltpu.CompilerParams(dimension_semantics=("arbitrary",),
                                             vmem_limit_bytes=VMEM_LIMIT_BYTES),
        name="outproj",
    )(mla_p, mla_s, gla_p, gla_s, xp, xs, w_o_b, ln1_g.reshape(1, -1), ln1_b.reshape(1, -1), peer_wq_b)


PEER_HEADS = 8
N_KEYS = 128
PEER_HALF = 128
PEER_TOPK = 16
ROUTE_TM = 256
CAND_WIDE_RANKS = 8
CAND_ROWS = PEER_TOPK + (CAND_WIDE_RANKS - 1) * SUBLANES + (PEER_TOPK - CAND_WIDE_RANKS)


def _first_max(x, row):
    n = x.shape[0]
    blocks = [(x[r:r + SUBLANES], row[r:r + SUBLANES]) for r in range(0, n, SUBLANES)]
    while len(blocks) > 1:
        merged = []
        for j in range(0, len(blocks) - 1, 2):
            (va, ia), (vb, ib) = blocks[j], blocks[j + 1]
            take = va >= vb
            merged.append((jnp.where(take, va, vb), jnp.where(take, ia, ib)))
        if len(blocks) % 2:
            merged.append(blocks[-1])
        blocks = merged
    v, i = blocks[0]
    m = jnp.max(v, axis=0, keepdims=True)
    pick = jnp.min(jnp.where(v == m, i, float(n)), axis=0, keepdims=True)
    return m, pick


def _extract_top(xs, row, n_out, on_pick):
    xs = list(xs)
    for k in range(n_out):
        for p, x in enumerate(xs):
            m, pick = _first_max(x, row)
            hit = row == pick
            on_pick(p, k, m, hit, pick)
            xs[p] = jnp.where(hit, -jnp.inf, x)


def _route_kernel(qp_ref, keys_ref, row_ref, g_ref, i1_ref, i2_ref, sv_ref, si_ref, gt_ref, et_ref):
    lax.fori_loop(0, PEER_HEADS,
                  functools.partial(_route_head, qp_ref, keys_ref, row_ref, sv_ref, si_ref, gt_ref, et_ref), 0)
    g_ref[...] = gt_ref[...].T
    expert = et_ref[...]
    key0 = jnp.floor(expert * (1.0 / N_KEYS))
    i1_ref[...] = key0.T
    i2_ref[...] = (expert - key0 * N_KEYS).T


def _route_head(qp_ref, keys_ref, row_ref, sv_ref, si_ref, gt_ref, et_ref, h, carry):
    tm = qp_ref.shape[1]
    q = qp_ref[h]
    scores = [lax.dot_general(keys_ref[half], q[:, half * PEER_HALF:(half + 1) * PEER_HALF], _NT,
                              preferred_element_type=F32) for half in range(2)]

    def keep(half, k, m, hit, pick):
        sv_ref[half, k:k + 1, :] = m
        si_ref[half, k:k + 1, :] = pick

    _extract_top(scores, row_ref[...], PEER_TOPK, keep)

    s1, s2 = sv_ref[0], sv_ref[1]
    i1, i2 = si_ref[0], si_ref[1]
    bc = lambda r, n: jnp.broadcast_to(r, (n, tm))
    wide = range(1, CAND_WIDE_RANKS)
    cand = jnp.concatenate([bc(s1[0:1], PEER_TOPK) + s2]
                           + [bc(s1[a:a + 1], SUBLANES) + s2[:SUBLANES] for a in wide]
                           + [s1[CAND_WIDE_RANKS:] + bc(s2[0:1], PEER_TOPK - CAND_WIDE_RANKS)], axis=0)
    i1 = i1 * N_KEYS
    c_expert = jnp.concatenate([bc(i1[0:1], PEER_TOPK) + i2]
                               + [bc(i1[a:a + 1], SUBLANES) + i2[:SUBLANES] for a in wide]
                               + [i1[CAND_WIDE_RANKS:] + bc(i2[0:1], PEER_TOPK - CAND_WIDE_RANKS)], axis=0)
    base = pl.multiple_of(h * PEER_TOPK, PEER_TOPK)

    def keep_pair(_, k, m, hit, pick):
        gt_ref[pl.ds(base + k, 1), :] = m
        et_ref[pl.ds(base + k, 1), :] = jnp.sum(jnp.where(hit, c_expert, 0.0), axis=0, keepdims=True)

    _extract_top([cand], row_ref[:CAND_ROWS], PEER_TOPK, keep_pair)

    sc = gt_ref[pl.ds(base, PEER_TOPK), :]
    e = jnp.exp(sc - sc[0:1])
    gt_ref[pl.ds(base, PEER_TOPK), :] = e / jnp.sum(e, axis=0, keepdims=True)
    return carry


def _route(qp, keys_b):
    t = qp.shape[1]
    tm = ROUTE_TM
    assert t % tm == 0
    n_sel = PEER_HEADS * PEER_TOPK
    out = pl.BlockSpec((tm, n_sel), lambda i: (i, 0))
    row_index = jnp.broadcast_to(jnp.arange(N_KEYS, dtype=F32)[:, None], (N_KEYS, tm))
    return pl.pallas_call(
        _route_kernel,
        grid=(t // tm,),
        in_specs=[pl.BlockSpec((PEER_HEADS, tm, 2 * PEER_HALF), lambda i: (0, i, 0)),
                  pl.BlockSpec(keys_b.shape, lambda i: (0, 0, 0)),
                  pl.BlockSpec((N_KEYS, tm), lambda i: (0, 0))],
        out_specs=[out, out, out],
        out_shape=[jax.ShapeDtypeStruct((t, n_sel), F32)] * 3,
        scratch_shapes=[pltpu.VMEM((2, PEER_TOPK, tm), F32), pltpu.VMEM((2, PEER_TOPK, tm), F32),
                        pltpu.VMEM((n_sel, tm), F32), pltpu.VMEM((n_sel, tm), F32)],
        compiler_params=pltpu.CompilerParams(dimension_semantics=("arbitrary",),
                                             vmem_limit_bytes=VMEM_LIMIT_BYTES),
        name="peer_route",
    )(qp, keys_b, row_index)


SCATTER_TB = 128
SCATTER_UNROLL = SUBLANES
W_ROW_TILE = SUBLANES


def _w_table_tokens(t):
    return -(-t // EXP_TM) * EXP_TM


def _scatter_kernel(g_ref, i1_ref, i2_ref, w_ref):
    tb = g_ref.shape[0]
    key = lax.broadcasted_iota(jnp.int32, (N_KEYS, g_ref.shape[1]), 0).astype(F32)

    def body(grp, carry):
        t0 = pl.multiple_of(grp * SCATTER_UNROLL, SCATTER_UNROLL)
        g8 = g_ref[pl.ds(t0, SCATTER_UNROLL), :]
        i18 = i1_ref[pl.ds(t0, SCATTER_UNROLL), :]
        i28 = i2_ref[pl.ds(t0, SCATTER_UNROLL), :]
        for u in range(SCATTER_UNROLL):
            g = jnp.broadcast_to(g8[u:u + 1], key.shape)
            i1 = jnp.broadcast_to(i18[u:u + 1], key.shape)
            i2 = jnp.broadcast_to(i28[u:u + 1], key.shape)
            a_t = jnp.where(key == i1, g, 0.0).astype(BF16)
            b_t = jnp.where(key == i2, 1.0, 0.0).astype(BF16)
            w = lax.dot_general(a_t, b_t, _NT, preferred_element_type=F32)
            w_ref[:, pl.ds(t0 + u, 1), :, :] = w.reshape(N_KEYS // W_ROW_TILE, 1, W_ROW_TILE, N_KEYS)
        return carry

    lax.fori_loop(0, tb // SCATTER_UNROLL, body, 0)


def _scatter(g, i1, i2):
    t = g.shape[0]
    tb = SCATTER_TB
    assert t % tb == 0
    sel = pl.BlockSpec((tb, g.shape[1]), lambda i: (i, 0))
    n_grp = N_KEYS // W_ROW_TILE
    return pl.pallas_call(
        _scatter_kernel,
        grid=(t // tb,),
        in_specs=[sel, sel, sel],
        out_specs=pl.BlockSpec((n_grp, tb, W_ROW_TILE, N_KEYS), lambda i: (0, i, 0, 0)),
        out_shape=jax.ShapeDtypeStruct((n_grp, _w_table_tokens(t), W_ROW_TILE, N_KEYS), F32),
        compiler_params=pltpu.CompilerParams(dimension_semantics=("arbitrary",),
                                             vmem_limit_bytes=VMEM_LIMIT_BYTES),
        name="peer_scatter",
    )(g, i1, i2)


EXP_TM = 1024
EXP_ROWS = 4
EXP_TE = EXP_ROWS * N_KEYS
EXP_SUBTILES = 4


def _gelu(x):
    return x * 0.5 * (1.0 + lax.erf(x * (2.0 ** -0.5)))


def _experts_kernel(h8_ref, hinv_ref, w_ref, u_ref, uinv_ref, v_ref, h_ref, g_ref, b_ref, y_ref):
    j = pl.program_id(1)
    tm = h8_ref.shape[0]

    @pl.when(j == 0)
    def _():
        y_ref[...] = DN_ALPHA * h_ref[...]

    row0 = (j % (W_ROW_TILE // EXP_ROWS)) * EXP_ROWS
    sub = tm // EXP_SUBTILES
    u_inv = jnp.concatenate([uinv_ref[r, 0:1, :] for r in range(EXP_ROWS)], axis=1)
    acts = []
    for s in range(EXP_SUBTILES):
        h_inv = hinv_ref[s * sub:(s + 1) * sub, :]
        a8 = lax.dot_general(h8_ref[s * sub:(s + 1) * sub, :], u_ref[...], _NT, preferred_element_type=F32)
        acts.append(a8 * jnp.concatenate([h_inv] * EXP_ROWS, axis=1) * u_inv)
    for s in range(EXP_SUBTILES):
        w = jnp.concatenate([w_ref[pl.ds(s * sub * W_ROW_TILE + row0 + r, sub, stride=W_ROW_TILE), :]
                             for r in range(EXP_ROWS)], axis=1)
        p = (w * _gelu(acts[s])).astype(BF16)
        y_ref[s * sub:(s + 1) * sub, :] += jnp.dot(p, v_ref[...], preferred_element_type=F32)

    @pl.when(j == pl.num_programs(1) - 1)
    def _():
        y_ref[...] = _layer_norm(y_ref[...], g_ref[...], b_ref[...])


def _experts(h8, h_inv, h, w_table, u8, u_inv, v_b, ln2_g, ln2_b, row0, n_rows):
    tm = min(EXP_TM, n_rows)
    assert n_rows % tm == 0 and row0 % tm == 0 and tm % (EXP_SUBTILES * FP8_ROW_TILE) == 0
    nt = n_rows // tm
    blk0 = row0 // tm
    ne = u8.shape[0] // EXP_TE
    steps_per_group = W_ROW_TILE // EXP_ROWS
    tiles_per_group = w_table.shape[1] // tm
    w2d = w_table.reshape(-1, N_KEYS)
    once = lambda w: pl.BlockSpec((tm, w), lambda i, j: (blk0 + i, 0), pipeline_mode=pl.Buffered(1))
    return pl.pallas_call(
        _experts_kernel,
        grid=(nt, ne),
        in_specs=[once(D_MODEL), once(LANES),
                  pl.BlockSpec((tm * W_ROW_TILE, N_KEYS),
                               lambda i, j: ((j // steps_per_group) * tiles_per_group + blk0 + i, 0)),
                  pl.BlockSpec((EXP_TE, D_MODEL), lambda i, j: (j, 0)),
                  pl.BlockSpec((EXP_ROWS, SUBLANES, LANES), lambda i, j: (j, 0, 0)),
                  pl.BlockSpec((EXP_TE, D_MODEL), lambda i, j: (j, 0)),
                  once(D_MODEL), pl.BlockSpec((1, D_MODEL), lambda i, j: (0, 0)),
                  pl.BlockSpec((1, D_MODEL), lambda i, j: (0, 0))],
        out_specs=pl.BlockSpec((tm, D_MODEL), lambda i, j: (i, 0)),
        out_shape=jax.ShapeDtypeStruct((n_rows, D_MODEL), F32),
        compiler_params=pltpu.CompilerParams(dimension_semantics=("arbitrary", "arbitrary"),
                                             vmem_limit_bytes=VMEM_LIMIT_BYTES),
        name="peer_experts",
    )(h8, h_inv, w2d, u8, u_inv, v_b, h, ln2_g.reshape(1, -1), ln2_b.reshape(1, -1))


def kernel(x_prompt, x_sample, cache_kv_latent, cache_k_rope, state_gla, w_in, q_norm_g, w_uq, kv_norm_g, w_ukv,
           w_gk2, b_gk, gla_norm_g, w_o, ln1_g, ln1_b, peer_wq, peer_keys, peer_u, peer_v, ln2_g, ln2_b):
    bp, sp, _ = x_prompt.shape
    bs, ss, _ = x_sample.shape
    past = cache_kv_latent.shape[1]
    tp, ts = bp * sp, bs * ss
    xp = x_prompt.reshape(tp, D_MODEL)
    xs = x_sample.reshape(ts, D_MODEL)
    cos_p, sin_p = _rope_tables(jnp.arange(sp, dtype=jnp.int32))
    cos_s, sin_s = _rope_tables(past + jnp.arange(ss, dtype=jnp.int32))
    cos = jnp.concatenate([jnp.tile(cos_p, (bp, 1)), jnp.tile(cos_s, (bs, 1))])
    sin = jnp.concatenate([jnp.tile(sin_p, (bp, 1)), jnp.tile(sin_s, (bs, 1))])
    w_in_r, wq_r, wukv_r, wvt_r, wgk_r = _mixer_weights(w_in, w_uq, w_ukv, w_gk2)
    q, k, v, vt, ckv_p, ckv_s, kr_p, kr_s, gq, gk, gv, gate, lf = _mixer(
        xp, xs, cos, sin, w_in_r, wq_r, wukv_r, wvt_r, wgk_r, q_norm_g, kv_norm_g, b_gk)
    o_p, (u8, u_inv, v_b) = _mla_prompt(q, k, vt, tp, peer_u, peer_v)
    o_s = _mla_sample(q, k, v, cache_kv_latent, cache_k_rope, wukv_r, tp, ss)
    g_p, st_p = _gla(gq, gk, gv, lf, gate, jnp.zeros((bp,) + state_gla.shape[1:], F32), gla_norm_g,
                     0, bp, sp // CHUNK)
    g_s, st_s = _gla(gq, gk, gv, lf, gate, state_gla, gla_norm_g, tp, bs, ss // CHUNK)

    h, h8, h_inv, qp = _outproj(o_p, o_s, g_p, g_s, xp, xs, w_o.astype(BF16), ln1_g, ln1_b, peer_wq.astype(BF16))
    gates, i1, i2 = _route(qp, peer_keys.astype(BF16))
    w_table = _scatter(gates, i1, i2)
    y_p = _experts(h8, h_inv, h, w_table, u8, u_inv, v_b, ln2_g, ln2_b, 0, tp)
    y_s = _experts(h8, h_inv, h, w_table, u8, u_inv, v_b, ln2_g, ln2_b, tp, ts)

    dt = x_prompt.dtype
    return (y_p.reshape(bp, sp, D_MODEL), y_s.reshape(bs, ss, D_MODEL),
            ckv_p.reshape(bp, sp, KV_RANK), kr_p.reshape(bp, sp, ROPE_DIM), st_p.astype(dt),
            ckv_s.reshape(bs, ss, KV_RANK), kr_s.reshape(bs, ss, ROPE_DIM), st_s.astype(dt))
```

```python
import functools
import math

import jax
import jax.numpy as jnp
from jax import lax
from jax.experimental import pallas as pl
from jax.experimental.pallas import tpu as pltpu

F32 = jnp.float32
BF16 = jnp.bfloat16
FP8 = jnp.float8_e4m3fn
FP8_ROW_TILE = 32
FP8_TARGET = 128.0
FP8_TINY = 1e-30


def _fp8_scaled(x, amax):
    return (x * (FP8_TARGET / amax)).astype(FP8)

LANES = 128
SUBLANES = 8
VMEM_LIMIT_BYTES = 60 * 1024 * 1024

D_MODEL = 2048
CHUNK = 64
MLA_HEADS = 8
Q_RANK = 512
KV_RANK = 256
NOPE_DIM = 128
ROPE_DIM = 64
V_DIM = 128
ROPE_THETA = 10000.0
MLA_SCALE = (NOPE_DIM + ROPE_DIM) ** -0.5
Q_SCALE = MLA_SCALE * math.log2(math.e)
QK_PAD = 256
GLA_HEADS = 4
GLA_DK = 128
GLA_DV = 256
GATE_RANK = 16
GATE_NORMALIZER = 16.0
GLA_KW = GLA_HEADS * GLA_DK
GLA_WIDTH = GLA_HEADS * GLA_DV
MLA_WIDTH = MLA_HEADS * V_DIM
DN_ALPHA = 2.0 ** 0.25
LN_EPS = 1e-5
RMS_EPS = 1e-6

Z_CQ = 0
Z_CKV = Z_CQ + Q_RANK
Z_KR = Z_CKV + KV_RANK
Z_KROT = Z_KR + LANES
Z_GQ = Z_KROT + LANES
Z_GK = Z_GQ + GLA_KW
Z_GV = Z_GK + GLA_KW
Z_GATE = Z_GV + GLA_WIDTH
Z_WIDTH = Z_GATE + GLA_WIDTH


def _const_spec(shape):
    nd = len(shape)
    return pl.BlockSpec(shape, lambda *_: (0,) * nd, pipeline_mode=pl.Buffered(1))


def _rms(x, g):
    return x * lax.rsqrt(jnp.mean(x * x, axis=-1, keepdims=True) + RMS_EPS) * g


MIX_TM = 256


def _two_stream_specs(tm, width, n_first):
    return (pl.BlockSpec((tm, width), lambda i: (jnp.minimum(i, n_first - 1), 0)),
            pl.BlockSpec((tm, width), lambda i: (jnp.maximum(i - n_first, 0), 0)))


def _mixer_kernel(xp_ref, xs_ref, w_in_ref, wq_ref, wukv_ref, wvt_ref, wgk_ref, qg_ref, kvg_ref, bgk_ref,
                  cos_ref, sin_ref, q_ref, k_ref, v_ref, vt_ref, ckvp_ref, ckvs_ref, krp_ref, krs_ref,
                  gq_ref, gk_ref, gv_ref, gate_ref, lf_ref, *, n_prompt_tiles):
    in_prompt = pl.program_id(0) < n_prompt_tiles
    xb = jnp.where(in_prompt, xp_ref[...], xs_ref[...]).astype(BF16)
    cos = cos_ref[...]
    sin = sin_ref[...]

    z_lat = jnp.dot(xb, w_in_ref[:, Z_CQ:Z_GQ], preferred_element_type=F32)
    cq = _rms(z_lat[:, Z_CQ:Z_CKV], qg_ref[...])
    c_kv = _rms(z_lat[:, Z_CKV:Z_KR], kvg_ref[...])
    krga = z_lat[:, Z_KR:Z_KROT]
    k_rope = krga * cos + z_lat[:, Z_KROT:Z_GQ] * sin

    @pl.when(in_prompt)
    def _():
        ckvp_ref[...] = c_kv
        krp_ref[...] = k_rope[:, :ROPE_DIM]

    @pl.when(jnp.logical_not(in_prompt))
    def _():
        ckvs_ref[...] = c_kv
        krs_ref[...] = k_rope[:, :ROPE_DIM]

    k_rope_b = k_rope.astype(BF16)

    q = jnp.dot(cq.astype(BF16), wq_ref[...], preferred_element_type=F32)
    kv = jnp.dot(c_kv.astype(BF16), wukv_ref[...], preferred_element_type=F32)
    for h in range(MLA_HEADS):
        lo = h * QK_PAD
        q_ref[:, lo:lo + NOPE_DIM] = (q[:, lo:lo + NOPE_DIM] * Q_SCALE).astype(BF16)
        rot = q[:, MLA_HEADS * QK_PAD + h * LANES:MLA_HEADS * QK_PAD + (h + 1) * LANES]
        q_ref[:, lo + NOPE_DIM:lo + QK_PAD] = (
            (q[:, lo + NOPE_DIM:lo + QK_PAD] * cos + rot * sin) * Q_SCALE).astype(BF16)
        k_ref[:, lo:lo + NOPE_DIM] = kv[:, h * NOPE_DIM:(h + 1) * NOPE_DIM].astype(BF16)
        k_ref[:, lo + NOPE_DIM:lo + QK_PAD] = k_rope_b
    v_ref[...] = kv[:, MLA_HEADS * NOPE_DIM:].astype(BF16)
    vt_ref[...] = jnp.dot(wvt_ref[...], c_kv.T.astype(BF16), preferred_element_type=F32).astype(BF16)

    pre = jnp.dot(krga.astype(BF16), wgk_ref[...], preferred_element_type=F32) + bgk_ref[...]
    lf_ref[...] = (jnp.minimum(pre, 0.0) - jnp.log1p(jnp.exp(-jnp.abs(pre)))) * (1.0 / GATE_NORMALIZER)

    gq_ref[...] = jnp.dot(xb, w_in_ref[:, Z_GQ:Z_GK], preferred_element_type=F32)
    gk_ref[...] = jnp.dot(xb, w_in_ref[:, Z_GK:Z_GV], preferred_element_type=F32)
    gv_ref[...] = jnp.dot(xb, w_in_ref[:, Z_GV:Z_GATE], preferred_element_type=F32)
    gate_ref[...] = jnp.dot(xb, w_in_ref[:, Z_GATE:Z_WIDTH], preferred_element_type=F32)


def _rotate_half_cols(w):
    half = ROPE_DIM // 2
    return jnp.concatenate([-w[..., half:], w[..., :half]], axis=-1)


def _mixer_weights(w_in, w_uq, w_ukv, w_gk2):
    pts = []
    acc = 0
    for s in (Q_RANK, KV_RANK, ROPE_DIM, GLA_KW, GLA_KW, GLA_WIDTH, GLA_WIDTH):
        acc += s
        pts.append(acc)
    cq, ckv, kr, gq, gk, gv, gate, ga = jnp.split(w_in, pts, axis=1)
    d = w_in.shape[0]
    w_in_r = jnp.concatenate(
        [cq, ckv, kr, ga, jnp.zeros((d, LANES - ROPE_DIM - GATE_RANK), F32),
         _rotate_half_cols(kr), jnp.zeros((d, LANES - ROPE_DIM), F32), gq, gk, gv, gate], axis=1).astype(BF16)

    wq = w_uq.reshape(Q_RANK, MLA_HEADS, NOPE_DIM + ROPE_DIM)
    wq_rope = wq[..., NOPE_DIM:]
    pad = jnp.zeros((Q_RANK, MLA_HEADS, QK_PAD - NOPE_DIM - ROPE_DIM), F32)
    wq_main = jnp.concatenate([wq, pad], axis=-1).reshape(Q_RANK, MLA_HEADS * QK_PAD)
    wq_rot = jnp.concatenate([_rotate_half_cols(wq_rope), pad], axis=-1).reshape(Q_RANK, MLA_HEADS * LANES)
    wq_r = jnp.concatenate([wq_main, wq_rot], axis=1).astype(BF16)

    wkv = w_ukv.reshape(KV_RANK, MLA_HEADS, NOPE_DIM + V_DIM)
    wukv_r = jnp.concatenate([wkv[..., :NOPE_DIM].reshape(KV_RANK, -1),
                              wkv[..., NOPE_DIM:].reshape(KV_RANK, -1)], axis=1).astype(BF16)

    wvt_r = wkv[..., NOPE_DIM:].reshape(KV_RANK, -1).T.astype(BF16)

    wgk_r = jnp.zeros((LANES, GLA_KW), F32).at[ROPE_DIM:ROPE_DIM + GATE_RANK].set(w_gk2).astype(BF16)
    return w_in_r, wq_r, wukv_r, wvt_r, wgk_r


def _rope_tables(pos):
    half = ROPE_DIM // 2
    freqs = ROPE_THETA ** (-jnp.arange(half, dtype=F32) / half)
    ang = pos.astype(F32)[:, None] * freqs[None, :]
    zeros = jnp.zeros((pos.shape[0], LANES - ROPE_DIM), F32)
    cos = jnp.concatenate([jnp.cos(ang), jnp.cos(ang), zeros], axis=1)
    sin = jnp.concatenate([jnp.sin(ang), jnp.sin(ang), zeros], axis=1)
    return cos, sin


def _mixer(xp, xs, cos, sin, w_in_r, wq_r, wukv_r, wvt_r, wgk_r, q_norm_g, kv_norm_g, b_gk):
    tp, ts = xp.shape[0], xs.shape[0]
    t = tp + ts
    tm = MIX_TM
    assert tp % tm == 0 and ts % tm == 0
    n_p = tp // tm
    row = lambda w: pl.BlockSpec((tm, w), lambda i: (i, 0))
    col = pl.BlockSpec((MLA_WIDTH, tm), lambda i: (0, i))
    full = lambda w, dt: (row(w), jax.ShapeDtypeStruct((t, w), dt))
    ckv_p, ckv_s = _two_stream_specs(tm, KV_RANK, n_p)
    kr_p, kr_s = _two_stream_specs(tm, ROPE_DIM, n_p)
    outs = [full(MLA_HEADS * QK_PAD, BF16), full(MLA_HEADS * QK_PAD, BF16), full(MLA_WIDTH, BF16),
            (col, jax.ShapeDtypeStruct((MLA_WIDTH, t), BF16)),
            (ckv_p, jax.ShapeDtypeStruct((tp, KV_RANK), F32)), (ckv_s, jax.ShapeDtypeStruct((ts, KV_RANK), F32)),
            (kr_p, jax.ShapeDtypeStruct((tp, ROPE_DIM), F32)), (kr_s, jax.ShapeDtypeStruct((ts, ROPE_DIM), F32)),
            full(GLA_KW, F32), full(GLA_KW, F32), full(GLA_WIDTH, F32), full(GLA_WIDTH, F32), full(GLA_KW, F32)]
    return pl.pallas_call(
        functools.partial(_mixer_kernel, n_prompt_tiles=n_p),
        grid=(t // tm,),
        in_specs=[*_two_stream_specs(tm, D_MODEL, n_p), _const_spec(w_in_r.shape), _const_spec(wq_r.shape),
                  _const_spec(wukv_r.shape), _const_spec(wvt_r.shape), _const_spec(wgk_r.shape),
                  _const_spec((1, Q_RANK)), _const_spec((1, KV_RANK)), _const_spec((1, GLA_KW)),
                  row(LANES), row(LANES)],
        out_specs=[o[0] for o in outs],
        out_shape=[o[1] for o in outs],
        compiler_params=pltpu.CompilerParams(dimension_semantics=("arbitrary",),
                                             vmem_limit_bytes=VMEM_LIMIT_BYTES),
        name="mixer",
    )(xp, xs, w_in_r, wq_r, wukv_r, wvt_r, wgk_r, q_norm_g.reshape(1, -1), kv_norm_g.reshape(1, -1),
      b_gk.reshape(1, -1), cos, sin)


MLA_TQ = 512
MLA_LOOKAHEAD = 2
MLA_SIDE_ROWS = 128
NEG_BIG = -1e30
_NT = (((1,), (1,)), ((), ()))


def _mla_prompt_kernel(qi_ref, ki_ref, q_ref, k_ref, vt_ref, u_ref, v_ref, o_ref, u8_ref, uinv_ref, v8_ref, vinv_ref,
                       m_ref, l_ref, acc_ref, *, side_steps):
    step = pl.program_id(0)
    qi = qi_ref[step]
    ki = ki_ref[step]
    tq = q_ref.shape[0]
    tk = k_ref.shape[0]

    @pl.when(step < side_steps)
    def _():
        for src, dst, inv in ((u_ref, u8_ref, uinv_ref), (v_ref, v8_ref, vinv_ref)):
            x = src[...]
            amax = jnp.max(jnp.max(jnp.abs(x), axis=0, keepdims=True), axis=1, keepdims=True)
            amax = jnp.maximum(amax, FP8_TINY)
            dst[...] = _fp8_scaled(x, amax)
            inv[0] = jnp.broadcast_to(amax * (1.0 / FP8_TARGET), inv.shape[1:])

    @pl.when(ki == 0)
    def _():
        m_ref[...] = jnp.full(m_ref.shape, NEG_BIG, F32)
        l_ref[...] = jnp.zeros(l_ref.shape, F32)
        acc_ref[...] = jnp.zeros(acc_ref.shape, F32)

    def update(masked):
        if masked:
            key_chunk = lax.broadcasted_iota(jnp.int32, (tk, tq), 0) // CHUNK
            qry_chunk = lax.broadcasted_iota(jnp.int32, (tk, tq), 1) // CHUNK
            visible = key_chunk <= qry_chunk
        def scores(h):
            return lax.dot_general(k_ref[:, h * QK_PAD:(h + 1) * QK_PAD], q_ref[:, h * QK_PAD:(h + 1) * QK_PAD],
                                   _NT, preferred_element_type=F32)

        ahead = [scores(h) for h in range(MLA_LOOKAHEAD)]
        for h in range(MLA_HEADS):
            s = ahead.pop(0)
            if h + MLA_LOOKAHEAD < MLA_HEADS:
                ahead.append(scores(h + MLA_LOOKAHEAD))
            if masked:
                s = jnp.where(visible, s, NEG_BIG)
            m_old = m_ref[h]
            m_new = jnp.maximum(m_old, jnp.max(s, axis=0, keepdims=True))
            p = jnp.exp2(s - m_new)
            alpha = jnp.exp2(m_old - m_new)
            l_ref[h] = alpha * l_ref[h] + jnp.sum(p, axis=0, keepdims=True)
            acc_ref[h] = alpha * acc_ref[h] + jnp.dot(vt_ref[h * V_DIM:(h + 1) * V_DIM, :], p.astype(BF16),
                                                      preferred_element_type=F32)
            m_ref[h] = m_new

    @pl.when(ki < qi)
    def _():
        update(False)

    @pl.when(ki == qi)
    def _():
        update(True)
        for h in range(MLA_HEADS):
            o_ref[:, h * V_DIM:(h + 1) * V_DIM] = (acc_ref[h] / l_ref[h]).T.astype(BF16)


def _mla_prompt(q, k, vt, n_tokens, table_u, table_v):
    tq = MLA_TQ
    assert n_tokens % tq == 0
    nq = n_tokens // tq
    pairs = [(a, b) for a in range(nq) for b in range(a + 1)]
    qi = jnp.asarray([p[0] for p in pairs], jnp.int32)
    ki = jnp.asarray([p[1] for p in pairs], jnp.int32)
    rows, cols = table_u.shape
    assert table_v.shape == (rows, cols) and rows % MLA_SIDE_ROWS == 0
    side_steps = rows // MLA_SIDE_ROWS
    assert side_steps <= len(pairs)
    side = lambda: pl.BlockSpec((MLA_SIDE_ROWS, cols), lambda s, qi, ki: (jnp.minimum(s, side_steps - 1), 0))
    inv = lambda: pl.BlockSpec((1, SUBLANES, LANES), lambda s, qi, ki: (jnp.minimum(s, side_steps - 1), 0, 0))
    grid_spec = pltpu.PrefetchScalarGridSpec(
        num_scalar_prefetch=2,
        grid=(len(pairs),),
        in_specs=[pl.BlockSpec((tq, MLA_HEADS * QK_PAD), lambda s, qi, ki: (qi[s], 0)),
                  pl.BlockSpec((tq, MLA_HEADS * QK_PAD), lambda s, qi, ki: (ki[s], 0)),
                  pl.BlockSpec((MLA_WIDTH, tq), lambda s, qi, ki: (0, ki[s])), side(), side()],
        out_specs=[pl.BlockSpec((tq, MLA_WIDTH), lambda s, qi, ki: (qi[s], 0)), side(), inv(), side(), inv()],
        scratch_shapes=[pltpu.VMEM((MLA_HEADS, 1, tq), F32), pltpu.VMEM((MLA_HEADS, 1, tq), F32),
                        pltpu.VMEM((MLA_HEADS, V_DIM, tq), F32)],
    )
    table8 = jax.ShapeDtypeStruct((rows, cols), FP8)
    inv_shape = jax.ShapeDtypeStruct((side_steps, SUBLANES, LANES), F32)
    o, u8, u_inv, v8, v_inv = pl.pallas_call(
        functools.partial(_mla_prompt_kernel, side_steps=side_steps),
        grid_spec=grid_spec,
        out_shape=[jax.ShapeDtypeStruct((n_tokens, MLA_WIDTH), BF16), table8, inv_shape, table8, inv_shape],
        compiler_params=pltpu.CompilerParams(dimension_semantics=("arbitrary",),
                                             vmem_limit_bytes=VMEM_LIMIT_BYTES),
        name="mla_prompt",
    )(qi, ki, q, k, vt, table_u, table_v)
    return o, (u8, u_inv), (v8, v_inv)


def _mla_sample_kernel(q_ref, kn_ref, vn_ref, lat_ref, krc_ref, wukv_ref, o_ref):
    kvc = jnp.dot(lat_ref[0].astype(BF16), wukv_ref[...], preferred_element_type=F32)
    krc = krc_ref[0]
    for h in range(MLA_HEADS):
        q = q_ref[:, h * QK_PAD:(h + 1) * QK_PAD]
        knc = kvc[:, h * NOPE_DIM:(h + 1) * NOPE_DIM].astype(BF16)
        vc = kvc[:, MLA_HEADS * NOPE_DIM + h * V_DIM:MLA_HEADS * NOPE_DIM + (h + 1) * V_DIM].astype(BF16)
        s_c = (lax.dot_general(q[:, :NOPE_DIM], knc, _NT, preferred_element_type=F32)
               + lax.dot_general(q[:, NOPE_DIM:], krc, _NT, preferred_element_type=F32))
        s_n = lax.dot_general(q, kn_ref[:, h * QK_PAD:(h + 1) * QK_PAD], _NT, preferred_element_type=F32)
        m = jnp.maximum(jnp.max(s_c, axis=1, keepdims=True), jnp.max(s_n, axis=1, keepdims=True))
        p_c = jnp.exp2(s_c - m)
        p_n = jnp.exp2(s_n - m)
        l = jnp.sum(p_c, axis=1, keepdims=True) + jnp.sum(p_n, axis=1, keepdims=True)
        o = (jnp.dot(p_c.astype(BF16), vc, preferred_element_type=F32)
             + jnp.dot(p_n.astype(BF16), vn_ref[:, h * V_DIM:(h + 1) * V_DIM], preferred_element_type=F32))
        o_ref[:, h * V_DIM:(h + 1) * V_DIM] = (o / l).astype(BF16)


def _mla_sample(q, k, v, cache_lat, cache_kr, wukv_r, row0, n_new):
    bs, past, _ = cache_lat.shape
    assert row0 % n_new == 0
    blk0 = row0 // n_new
    krc = jnp.pad(cache_kr, ((0, 0), (0, 0), (0, LANES - ROPE_DIM))).astype(BF16)
    new = lambda w: pl.BlockSpec((n_new, w), lambda b: (blk0 + b, 0))
    return pl.pallas_call(
        _mla_sample_kernel,
        grid=(bs,),
        in_specs=[new(MLA_HEADS * QK_PAD), new(MLA_HEADS * QK_PAD), new(MLA_WIDTH),
                  pl.BlockSpec((1, past, KV_RANK), lambda b: (b, 0, 0)),
                  pl.BlockSpec((1, past, LANES), lambda b: (b, 0, 0)),
                  _const_spec(wukv_r.shape)],
        out_specs=pl.BlockSpec((n_new, MLA_WIDTH), lambda b: (b, 0)),
        out_shape=jax.ShapeDtypeStruct((bs * n_new, MLA_WIDTH), BF16),
        compiler_params=pltpu.CompilerParams(dimension_semantics=("arbitrary",),
                                             vmem_limit_bytes=VMEM_LIMIT_BYTES),
        name="mla_sample",
    )(q, k, v, cache_lat, krc, wukv_r)


GLA_SUB = 16
_TN = (((0,), (0,)), ((), ()))


def _cumsum_rows(x):
    n = x.shape[0]
    row = lax.broadcasted_iota(jnp.int32, x.shape, 0)
    s = 1
    while s < n:
        x = x + jnp.where(row >= s, pltpu.roll(x, s, axis=0), 0.0)
        s *= 2
    return x


def _gla_kernel(q_ref, k_ref, v_ref, lf_ref, gate_ref, s0_ref, g_ref, o_ref, send_ref, st_ref, *, n_chunks):
    c = pl.program_id(1)
    n_rows = q_ref.shape[0]
    n_sub = n_rows // GLA_SUB

    @pl.when(c == 0)
    def _():
        for h in range(GLA_HEADS):
            st_ref[h] = s0_ref[0, h].T

    sub_row = lax.broadcasted_iota(jnp.int32, (GLA_SUB, GLA_DK), 0)
    for h in range(GLA_HEADS):
        ks = slice(h * GLA_DK, (h + 1) * GLA_DK)
        vs = slice(h * GLA_DV, (h + 1) * GLA_DV)
        q = q_ref[:, ks] * (GLA_DK ** -0.5)
        k = k_ref[:, ks]
        v = v_ref[:, vs]
        vb = v.astype(BF16)
        b = _cumsum_rows(lf_ref[:, ks])
        b_last = b[n_rows - 1:n_rows]
        st = st_ref[h]
        o_inter = lax.dot_general((q * jnp.exp(b)).astype(BF16), st.astype(BF16), _NT,
                                  preferred_element_type=F32)
        outs = []
        for i in range(n_sub):
            r0 = i * GLA_SUB
            bi = b[r0:r0 + GLA_SUB]
            qi = q[r0:r0 + GLA_SUB]
            o_i = o_inter[r0:r0 + GLA_SUB]
            if i > 0:
                b_ref = b[r0 - 1:r0]
                qh = (qi * jnp.exp(bi - b_ref)).astype(BF16)
                kh = (k[:r0] * jnp.exp(b_ref - b[:r0])).astype(BF16)
                a_off = lax.dot_general(qh, kh, _NT, preferred_element_type=F32)
                o_i = o_i + jnp.dot(a_off.astype(BF16), vb[:r0], preferred_element_type=F32)
            for j in range(GLA_SUB):
                r = r0 + j
                decay = jnp.exp(jnp.where(sub_row >= j, bi - b[r:r + 1], -jnp.inf))
                a_col = jnp.sum(qi * k[r:r + 1] * decay, axis=1, keepdims=True)
                o_i = o_i + a_col * v[r:r + 1]
            outs.append(o_i)
        o = jnp.concatenate(outs, axis=0)

        kd = (k * jnp.exp(b_last - b)).astype(BF16)
        st_new = st * jnp.exp(b_last) + lax.dot_general(vb, kd, _TN, preferred_element_type=F32)
        st_ref[h] = st_new

        gate = gate_ref[:, vs]
        on = _rms(o, g_ref[...])
        o_ref[:, vs] = (on * (gate / (1.0 + jnp.exp(-gate)))).astype(BF16)

    @pl.when(c == n_chunks - 1)
    def _():
        for h in range(GLA_HEADS):
            send_ref[0, h] = st_ref[h].T


def _gla(gq, gk, gv, lf, gate, s0, gla_norm_g, row0, n_seq, n_chunks):
    assert row0 % CHUNK == 0
    blk0 = row0 // CHUNK
    row = lambda w: pl.BlockSpec((CHUNK, w), lambda b, c: (blk0 + b * n_chunks + c, 0))
    state = pl.BlockSpec((1, GLA_HEADS, GLA_DK, GLA_DV), lambda b, c: (b, 0, 0, 0))
    return pl.pallas_call(
        functools.partial(_gla_kernel, n_chunks=n_chunks),
        grid=(n_seq, n_chunks),
        in_specs=[row(GLA_KW), row(GLA_KW), row(GLA_WIDTH), row(GLA_KW), row(GLA_WIDTH), state,
                  pl.BlockSpec((1, GLA_DV), lambda b, c: (0, 0))],
        out_specs=[pl.BlockSpec((CHUNK, GLA_WIDTH), lambda b, c: (b * n_chunks + c, 0)), state],
        out_shape=[jax.ShapeDtypeStruct((n_seq * n_chunks * CHUNK, GLA_WIDTH), BF16),
                   jax.ShapeDtypeStruct((n_seq, GLA_HEADS, GLA_DK, GLA_DV), F32)],
        scratch_shapes=[pltpu.VMEM((GLA_HEADS, GLA_DV, GLA_DK), F32)],
        compiler_params=pltpu.CompilerParams(dimension_semantics=("arbitrary", "arbitrary"),
                                             vmem_limit_bytes=VMEM_LIMIT_BYTES),
        name="gla",
    )(gq, gk, gv, lf, gate, s0, gla_norm_g.reshape(1, -1))


OUT_TM = 256


def _layer_norm(y, g, b):
    mu = jnp.mean(y, axis=-1, keepdims=True)
    yc = y - mu
    var = jnp.mean(yc * yc, axis=-1, keepdims=True)
    return yc * lax.rsqrt(var + LN_EPS) * g + b


def _outproj_kernel(mlap_ref, mlas_ref, glap_ref, glas_ref, xp_ref, xs_ref, wo_ref, g_ref, b_ref, wq_ref,
                    h_ref, h8_ref, hinv_ref, qp_ref, *, n_prompt_tiles):
    in_prompt = pl.program_id(0) < n_prompt_tiles
    mla = jnp.where(in_prompt, mlap_ref[...], mlas_ref[...])
    gla = jnp.where(in_prompt, glap_ref[...], glas_ref[...])
    x = jnp.where(in_prompt, xp_ref[...], xs_ref[...])
    mix = (jnp.dot(mla, wo_ref[:MLA_WIDTH], preferred_element_type=F32)
           + jnp.dot(gla, wo_ref[MLA_WIDTH:], preferred_element_type=F32))
    h = _layer_norm(DN_ALPHA * x + mix, g_ref[...], b_ref[...])
    h_ref[...] = h
    amax = jnp.maximum(jnp.max(jnp.abs(h), axis=1, keepdims=True), FP8_TINY)
    h8_ref[...] = _fp8_scaled(h, amax)
    hinv_ref[...] = jnp.broadcast_to(amax * (1.0 / FP8_TARGET), hinv_ref.shape)
    qp = jnp.dot(h.astype(BF16), wq_ref[...], preferred_element_type=F32).astype(BF16)
    per_head = qp_ref.shape[2]
    for head in range(qp_ref.shape[0]):
        qp_ref[head] = qp[:, head * per_head:(head + 1) * per_head]


def _outproj(mla_p, mla_s, gla_p, gla_s, xp, xs, w_o_b, ln1_g, ln1_b, peer_wq_b):
    tp, ts = xp.shape[0], xs.shape[0]
    t = tp + ts
    tm = OUT_TM
    assert tp % tm == 0 and ts % tm == 0
    n_p = tp // tm
    row = lambda w: pl.BlockSpec((tm, w), lambda i: (i, 0))
    return pl.pallas_call(
        functools.partial(_outproj_kernel, n_prompt_tiles=n_p),
        grid=(t // tm,),
        in_specs=[*_two_stream_specs(tm, MLA_WIDTH, n_p), *_two_stream_specs(tm, GLA_WIDTH, n_p),
                  *_two_stream_specs(tm, D_MODEL, n_p), _const_spec(w_o_b.shape),
                  _const_spec((1, D_MODEL)), _const_spec((1, D_MODEL)), _const_spec(peer_wq_b.shape)],
        out_specs=[row(D_MODEL), row(D_MODEL), row(LANES),
                   pl.BlockSpec((PEER_HEADS, tm, 2 * PEER_HALF), lambda i: (0, i, 0))],
        out_shape=[jax.ShapeDtypeStruct((t, D_MODEL), F32), jax.ShapeDtypeStruct((t, D_MODEL), FP8),
                   jax.ShapeDtypeStruct((t, LANES), F32),
                   jax.ShapeDtypeStruct((PEER_HEADS, t, 2 * PEER_HALF), BF16)],
        compiler_params=pltpu.CompilerParams(dimension_semantics=("arbitrary",),
                                             vmem_limit_bytes=VMEM_LIMIT_BYTES),
        name="outproj",
    )(mla_p, mla_s, gla_p, gla_s, xp, xs, w_o_b, ln1_g.reshape(1, -1), ln1_b.reshape(1, -1), peer_wq_b)


PEER_HEADS = 8
N_KEYS = 128
PEER_HALF = 128
PEER_TOPK = 16
ROUTE_TM = 256
CAND_WIDE_RANKS = 8
CAND_ROWS = PEER_TOPK + (CAND_WIDE_RANKS - 1) * SUBLANES + (PEER_TOPK - CAND_WIDE_RANKS)


def _first_max(x, row):
    n = x.shape[0]
    blocks = [(x[r:r + SUBLANES], row[r:r + SUBLANES]) for r in range(0, n, SUBLANES)]
    while len(blocks) > 1:
        merged = []
        for j in range(0, len(blocks) - 1, 2):
            (va, ia), (vb, ib) = blocks[j], blocks[j + 1]
            take = va >= vb
            merged.append((jnp.where(take, va, vb), jnp.where(take, ia, ib)))
        if len(blocks) % 2:
            merged.append(blocks[-1])
        blocks = merged
    v, i = blocks[0]
    m = jnp.max(v, axis=0, keepdims=True)
    pick = jnp.min(jnp.where(v == m, i, float(n)), axis=0, keepdims=True)
    return m, pick


def _extract_top(xs, row, n_out, on_pick):
    xs = list(xs)
    for k in range(n_out):
        for p, x in enumerate(xs):
            m, pick = _first_max(x, row)
            hit = row == pick
            on_pick(p, k, m, hit, pick)
            xs[p] = jnp.where(hit, -jnp.inf, x)


def _route_kernel(qp_ref, keys_ref, row_ref, g_ref, i1_ref, i2_ref, sv_ref, si_ref, gt_ref, et_ref):
    lax.fori_loop(0, PEER_HEADS,
                  functools.partial(_route_head, qp_ref, keys_ref, row_ref, sv_ref, si_ref, gt_ref, et_ref), 0)
    g_ref[...] = gt_ref[...].T
    expert = et_ref[...]
    key0 = jnp.floor(expert * (1.0 / N_KEYS))
    i1_ref[...] = key0.T
    i2_ref[...] = (expert - key0 * N_KEYS).T


def _route_head(qp_ref, keys_ref, row_ref, sv_ref, si_ref, gt_ref, et_ref, h, carry):
    tm = qp_ref.shape[1]
    q = qp_ref[h]
    scores = [lax.dot_general(keys_ref[half], q[:, half * PEER_HALF:(half + 1) * PEER_HALF], _NT,
                              preferred_element_type=F32) for half in range(2)]

    def keep(half, k, m, hit, pick):
        sv_ref[half, k:k + 1, :] = m
        si_ref[half, k:k + 1, :] = pick

    _extract_top(scores, row_ref[...], PEER_TOPK, keep)

    s1, s2 = sv_ref[0], sv_ref[1]
    i1, i2 = si_ref[0], si_ref[1]
    bc = lambda r, n: jnp.broadcast_to(r, (n, tm))
    wide = range(1, CAND_WIDE_RANKS)
    cand = jnp.concatenate([bc(s1[0:1], PEER_TOPK) + s2]
                           + [bc(s1[a:a + 1], SUBLANES) + s2[:SUBLANES] for a in wide]
                           + [s1[CAND_WIDE_RANKS:] + bc(s2[0:1], PEER_TOPK - CAND_WIDE_RANKS)], axis=0)
    i1 = i1 * N_KEYS
    c_expert = jnp.concatenate([bc(i1[0:1], PEER_TOPK) + i2]
                               + [bc(i1[a:a + 1], SUBLANES) + i2[:SUBLANES] for a in wide]
                               + [i1[CAND_WIDE_RANKS:] + bc(i2[0:1], PEER_TOPK - CAND_WIDE_RANKS)], axis=0)
    base = pl.multiple_of(h * PEER_TOPK, PEER_TOPK)

    def keep_pair(_, k, m, hit, pick):
        gt_ref[pl.ds(base + k, 1), :] = m
        et_ref[pl.ds(base + k, 1), :] = jnp.sum(jnp.where(hit, c_expert, 0.0), axis=0, keepdims=True)

    _extract_top([cand], row_ref[:CAND_ROWS], PEER_TOPK, keep_pair)

    sc = gt_ref[pl.ds(base, PEER_TOPK), :]
    e = jnp.exp(sc - sc[0:1])
    gt_ref[pl.ds(base, PEER_TOPK), :] = e / jnp.sum(e, axis=0, keepdims=True)
    return carry


def _route(qp, keys_b):
    t = qp.shape[1]
    tm = ROUTE_TM
    assert t % tm == 0
    n_sel = PEER_HEADS * PEER_TOPK
    out = pl.BlockSpec((tm, n_sel), lambda i: (i, 0))
    row_index = jnp.broadcast_to(jnp.arange(N_KEYS, dtype=F32)[:, None], (N_KEYS, tm))
    return pl.pallas_call(
        _route_kernel,
        grid=(t // tm,),
        in_specs=[pl.BlockSpec((PEER_HEADS, tm, 2 * PEER_HALF), lambda i: (0, i, 0)),
                  pl.BlockSpec(keys_b.shape, lambda i: (0, 0, 0)),
                  pl.BlockSpec((N_KEYS, tm), lambda i: (0, 0))],
        out_specs=[out, out, out],
        out_shape=[jax.ShapeDtypeStruct((t, n_sel), F32)] * 3,
        scratch_shapes=[pltpu.VMEM((2, PEER_TOPK, tm), F32), pltpu.VMEM((2, PEER_TOPK, tm), F32),
                        pltpu.VMEM((n_sel, tm), F32), pltpu.VMEM((n_sel, tm), F32)],
        compiler_params=pltpu.CompilerParams(dimension_semantics=("arbitrary",),
                                             vmem_limit_bytes=VMEM_LIMIT_BYTES),
        name="peer_route",
    )(qp, keys_b, row_index)


SCATTER_TB = 128
SCATTER_UNROLL = SUBLANES
W_ROW_TILE = SUBLANES


def _w_table_tokens(t):
    return -(-t // EXP_TM) * EXP_TM


def _scatter_kernel(g_ref, i1_ref, i2_ref, w_ref, *, n_token_blocks):
    tb = g_ref.shape[0]
    key = lax.broadcasted_iota(jnp.int32, (N_KEYS, g_ref.shape[1]), 0).astype(F32)

    @pl.when(pl.program_id(0) >= n_token_blocks)
    def _():
        w_ref[...] = jnp.zeros(w_ref.shape, F32)

    def body(grp, carry):
        t0 = pl.multiple_of(grp * SCATTER_UNROLL, SCATTER_UNROLL)
        g8 = g_ref[pl.ds(t0, SCATTER_UNROLL), :]
        i18 = i1_ref[pl.ds(t0, SCATTER_UNROLL), :]
        i28 = i2_ref[pl.ds(t0, SCATTER_UNROLL), :]
        for u in range(SCATTER_UNROLL):
            g = jnp.broadcast_to(g8[u:u + 1], key.shape)
            i1 = jnp.broadcast_to(i18[u:u + 1], key.shape)
            i2 = jnp.broadcast_to(i28[u:u + 1], key.shape)
            a_t = jnp.where(key == i1, g, 0.0).astype(BF16)
            b_t = jnp.where(key == i2, 1.0, 0.0).astype(BF16)
            w = lax.dot_general(a_t, b_t, _NT, preferred_element_type=F32)
            w_ref[:, pl.ds(t0 + u, 1), :, :] = w.reshape(N_KEYS // W_ROW_TILE, 1, W_ROW_TILE, N_KEYS)
        return carry

    @pl.when(pl.program_id(0) < n_token_blocks)
    def _():
        lax.fori_loop(0, tb // SCATTER_UNROLL, body, 0)


def _scatter(g, i1, i2):
    t = g.shape[0]
    tb = SCATTER_TB
    t_pad = _w_table_tokens(t)
    assert t % tb == 0 and t_pad % tb == 0
    n_blocks = t // tb
    sel = pl.BlockSpec((tb, g.shape[1]), lambda i: (jnp.minimum(i, n_blocks - 1), 0))
    n_grp = N_KEYS // W_ROW_TILE
    return pl.pallas_call(
        functools.partial(_scatter_kernel, n_token_blocks=n_blocks),
        grid=(t_pad // tb,),
        in_specs=[sel, sel, sel],
        out_specs=pl.BlockSpec((n_grp, tb, W_ROW_TILE, N_KEYS), lambda i: (0, i, 0, 0)),
        out_shape=jax.ShapeDtypeStruct((n_grp, t_pad, W_ROW_TILE, N_KEYS), F32),
        compiler_params=pltpu.CompilerParams(dimension_semantics=("arbitrary",),
                                             vmem_limit_bytes=VMEM_LIMIT_BYTES),
        name="peer_scatter",
    )(g, i1, i2)


EXP_TM = 1024
EXP_ROWS = 4
EXP_TE = EXP_ROWS * N_KEYS
EXP_SUBTILES = 4


def _gelu(x):
    return x * 0.5 * (1.0 + lax.erf(x * (2.0 ** -0.5)))


def _experts_kernel(h8_ref, hinv_ref, w_ref, u_ref, uinv_ref, v_ref, vinv_ref, h_ref, g_ref, b_ref, y_ref):
    j = pl.program_id(1)
    tm = h8_ref.shape[0]

    @pl.when(j == 0)
    def _():
        y_ref[...] = DN_ALPHA * h_ref[...]

    row0 = (j % (W_ROW_TILE // EXP_ROWS)) * EXP_ROWS
    sub = tm // EXP_SUBTILES
    u_inv = jnp.concatenate([uinv_ref[r, 0:1, :] for r in range(EXP_ROWS)], axis=1)
    v_inv = jnp.concatenate([vinv_ref[r, 0:1, :] for r in range(EXP_ROWS)], axis=1)
    acts = []
    for s in range(EXP_SUBTILES):
        h_inv = hinv_ref[s * sub:(s + 1) * sub, :]
        a8 = lax.dot_general(h8_ref[s * sub:(s + 1) * sub, :], u_ref[...], _NT, preferred_element_type=F32)
        acts.append(a8 * jnp.concatenate([h_inv] * EXP_ROWS, axis=1) * u_inv)
    for s in range(EXP_SUBTILES):
        w = jnp.concatenate([w_ref[pl.ds(s * sub * W_ROW_TILE + row0 + r, sub, stride=W_ROW_TILE), :]
                             for r in range(EXP_ROWS)], axis=1)
        p = w * _gelu(acts[s]) * v_inv
        pmax = jnp.maximum(jnp.max(jnp.abs(p), axis=1, keepdims=True), FP8_TINY)
        out = jnp.dot(_fp8_scaled(p, pmax), v_ref[...], preferred_element_type=F32)
        y_ref[s * sub:(s + 1) * sub, :] += out * (pmax * (1.0 / FP8_TARGET))

    @pl.when(j == pl.num_programs(1) - 1)
    def _():
        y_ref[...] = _layer_norm(y_ref[...], g_ref[...], b_ref[...])


def _experts(h8, h_inv, h, w_table, u8, u_inv, v8, v_inv, ln2_g, ln2_b, row0, n_rows):
    tm = min(EXP_TM, n_rows)
    assert n_rows % tm == 0 and row0 % tm == 0 and tm % (EXP_SUBTILES * FP8_ROW_TILE) == 0
    nt = n_rows // tm
    blk0 = row0 // tm
    ne = u8.shape[0] // EXP_TE
    steps_per_group = W_ROW_TILE // EXP_ROWS
    tiles_per_group = w_table.shape[1] // tm
    w2d = w_table.reshape(-1, N_KEYS)
    once = lambda w: pl.BlockSpec((tm, w), lambda i, j: (blk0 + i, 0), pipeline_mode=pl.Buffered(1))
    return pl.pallas_call(
        _experts_kernel,
        grid=(nt, ne),
        in_specs=[once(D_MODEL), once(LANES),
                  pl.BlockSpec((tm * W_ROW_TILE, N_KEYS),
                               lambda i, j: ((j // steps_per_group) * tiles_per_group + blk0 + i, 0)),
                  pl.BlockSpec((EXP_TE, D_MODEL), lambda i, j: (j, 0)),
                  pl.BlockSpec((EXP_ROWS, SUBLANES, LANES), lambda i, j: (j, 0, 0)),
                  pl.BlockSpec((EXP_TE, D_MODEL), lambda i, j: (j, 0)),
                  pl.BlockSpec((EXP_ROWS, SUBLANES, LANES), lambda i, j: (j, 0, 0)),
                  once(D_MODEL), pl.BlockSpec((1, D_MODEL), lambda i, j: (0, 0)),
                  pl.BlockSpec((1, D_MODEL), lambda i, j: (0, 0))],
        out_specs=pl.BlockSpec((tm, D_MODEL), lambda i, j: (i, 0)),
        out_shape=jax.ShapeDtypeStruct((n_rows, D_MODEL), F32),
        compiler_params=pltpu.CompilerParams(dimension_semantics=("arbitrary", "arbitrary"),
                                             vmem_limit_bytes=VMEM_LIMIT_BYTES),
        name="peer_experts",
    )(h8, h_inv, w2d, u8, u_inv, v8, v_inv, h, ln2_g.reshape(1, -1), ln2_b.reshape(1, -1))


def kernel(x_prompt, x_sample, cache_kv_latent, cache_k_rope, state_gla, w_in, q_norm_g, w_uq, kv_norm_g, w_ukv,
           w_gk2, b_gk, gla_norm_g, w_o, ln1_g, ln1_b, peer_wq, peer_keys, peer_u, peer_v, ln2_g, ln2_b):
    bp, sp, _ = x_prompt.shape
    bs, ss, _ = x_sample.shape
    past = cache_kv_latent.shape[1]
    tp, ts = bp * sp, bs * ss
    xp = x_prompt.reshape(tp, D_MODEL)
    xs = x_sample.reshape(ts, D_MODEL)
    cos_p, sin_p = _rope_tables(jnp.arange(sp, dtype=jnp.int32))
    cos_s, sin_s = _rope_tables(past + jnp.arange(ss, dtype=jnp.int32))
    cos = jnp.concatenate([jnp.tile(cos_p, (bp, 1)), jnp.tile(cos_s, (bs, 1))])
    sin = jnp.concatenate([jnp.tile(sin_p, (bp, 1)), jnp.tile(sin_s, (bs, 1))])
    w_in_r, wq_r, wukv_r, wvt_r, wgk_r = _mixer_weights(w_in, w_uq, w_ukv, w_gk2)
    q, k, v, vt, ckv_p, ckv_s, kr_p, kr_s, gq, gk, gv, gate, lf = _mixer(
        xp, xs, cos, sin, w_in_r, wq_r, wukv_r, wvt_r, wgk_r, q_norm_g, kv_norm_g, b_gk)
    o_p, (u8, u_inv), (v8, v_inv) = _mla_prompt(q, k, vt, tp, peer_u, peer_v)
    o_s = _mla_sample(q, k, v, cache_kv_latent, cache_k_rope, wukv_r, tp, ss)
    g_p, st_p = _gla(gq, gk, gv, lf, gate, jnp.zeros((bp,) + state_gla.shape[1:], F32), gla_norm_g,
                     0, bp, sp // CHUNK)
    g_s, st_s = _gla(gq, gk, gv, lf, gate, state_gla, gla_norm_g, tp, bs, ss // CHUNK)

    h, h8, h_inv, qp = _outproj(o_p, o_s, g_p, g_s, xp, xs, w_o.astype(BF16), ln1_g, ln1_b, peer_wq.astype(BF16))
    gates, i1, i2 = _route(qp, peer_keys.astype(BF16))
    w_table = _scatter(gates, i1, i2)
    y_p = _experts(h8, h_inv, h, w_table, u8, u_inv, v8, v_inv, ln2_g, ln2_b, 0, tp)
    y_s = _experts(h8, h_inv, h, w_table, u8, u_inv, v8, v_inv, ln2_g, ln2_b, tp, ts)

    dt = x_prompt.dtype
    return (y_p.reshape(bp, sp, D_MODEL), y_s.reshape(bs, ss, D_MODEL),
            ckv_p.reshape(bp, sp, KV_RANK), kr_p.reshape(bp, sp, ROPE_DIM), st_p.astype(dt),
            ckv_s.reshape(bs, ss, KV_RANK), kr_s.reshape(bs, ss, ROPE_DIM), st_s.astype(dt))
```

```python
import functools
import math

import jax
import jax.numpy as jnp
from jax import lax
from jax.experimental import pallas as pl
from jax.experimental.pallas import tpu as pltpu

F32 = jnp.float32
BF16 = jnp.bfloat16
FP8 = jnp.float8_e4m3fn
FP8_ROW_TILE = 32
FP8_TARGET = 128.0
FP8_TINY = 1e-30


def _fp8_scaled(x, amax):
    return (x * (FP8_TARGET / amax)).astype(FP8)

LANES = 128
SUBLANES = 8
VMEM_LIMIT_BYTES = 60 * 1024 * 1024

D_MODEL = 2048
CHUNK = 64
MLA_HEADS = 8
Q_RANK = 512
KV_RANK = 256
NOPE_DIM = 128
ROPE_DIM = 64
V_DIM = 128
ROPE_THETA = 10000.0
MLA_SCALE = (NOPE_DIM + ROPE_DIM) ** -0.5
Q_SCALE = MLA_SCALE * math.log2(math.e)
QK_PAD = 256
GLA_HEADS = 4
GLA_DK = 128
GLA_DV = 256
GATE_RANK = 16
GATE_NORMALIZER = 16.0
GLA_KW = GLA_HEADS * GLA_DK
GLA_WIDTH = GLA_HEADS * GLA_DV
MLA_WIDTH = MLA_HEADS * V_DIM
DN_ALPHA = 2.0 ** 0.25
LN_EPS = 1e-5
RMS_EPS = 1e-6

Z_CQ = 0
Z_CKV = Z_CQ + Q_RANK
Z_KR = Z_CKV + KV_RANK
Z_KROT = Z_KR + LANES
Z_GQ = Z_KROT + LANES
Z_GK = Z_GQ + GLA_KW
Z_GV = Z_GK + GLA_KW
Z_GATE = Z_GV + GLA_WIDTH
Z_WIDTH = Z_GATE + GLA_WIDTH


def _const_spec(shape):
    nd = len(shape)
    return pl.BlockSpec(shape, lambda *_: (0,) * nd, pipeline_mode=pl.Buffered(1))


def _rms(x, g):
    return x * lax.rsqrt(jnp.mean(x * x, axis=-1, keepdims=True) + RMS_EPS) * g


MIX_TM = 256


def _two_stream_specs(tm, width, n_first):
    return (pl.BlockSpec((tm, width), lambda i: (jnp.minimum(i, n_first - 1), 0)),
            pl.BlockSpec((tm, width), lambda i: (jnp.maximum(i - n_first, 0), 0)))


def _mixer_kernel(xp_ref, xs_ref, w_in_ref, wq_ref, wukv_ref, wvt_ref, wgk_ref, qg_ref, kvg_ref, bgk_ref,
                  cos_ref, sin_ref, q_ref, k_ref, v_ref, vt_ref, ckvp_ref, ckvs_ref, krp_ref, krs_ref,
                  gq_ref, gk_ref, gv_ref, gate_ref, lf_ref, *, n_prompt_tiles):
    in_prompt = pl.program_id(0) < n_prompt_tiles
    xb = jnp.where(in_prompt, xp_ref[...], xs_ref[...]).astype(BF16)
    cos = cos_ref[...]
    sin = sin_ref[...]

    z_lat = jnp.dot(xb, w_in_ref[:, Z_CQ:Z_GQ], preferred_element_type=F32)
    cq = _rms(z_lat[:, Z_CQ:Z_CKV], qg_ref[...])
    c_kv = _rms(z_lat[:, Z_CKV:Z_KR], kvg_ref[...])
    krga = z_lat[:, Z_KR:Z_KROT]
    k_rope = krga * cos + z_lat[:, Z_KROT:Z_GQ] * sin

    @pl.when(in_prompt)
    def _():
        ckvp_ref[...] = c_kv
        krp_ref[...] = k_rope[:, :ROPE_DIM]

    @pl.when(jnp.logical_not(in_prompt))
    def _():
        ckvs_ref[...] = c_kv
        krs_ref[...] = k_rope[:, :ROPE_DIM]

    k_rope_b = k_rope.astype(BF16)

    q = jnp.dot(cq.astype(BF16), wq_ref[...], preferred_element_type=F32)
    kv = jnp.dot(c_kv.astype(BF16), wukv_ref[...], preferred_element_type=F32)
    for h in range(MLA_HEADS):
        lo = h * QK_PAD
        q_ref[:, lo:lo + NOPE_DIM] = (q[:, lo:lo + NOPE_DIM] * Q_SCALE).astype(BF16)
        rot = q[:, MLA_HEADS * QK_PAD + h * LANES:MLA_HEADS * QK_PAD + (h + 1) * LANES]
        q_ref[:, lo + NOPE_DIM:lo + QK_PAD] = (
            (q[:, lo + NOPE_DIM:lo + QK_PAD] * cos + rot * sin) * Q_SCALE).astype(BF16)
        k_ref[:, lo:lo + NOPE_DIM] = kv[:, h * NOPE_DIM:(h + 1) * NOPE_DIM].astype(BF16)
        k_ref[:, lo + NOPE_DIM:lo + QK_PAD] = k_rope_b
    v_ref[...] = kv[:, MLA_HEADS * NOPE_DIM:].astype(BF16)
    vt_ref[...] = jnp.dot(wvt_ref[...], c_kv.T.astype(BF16), preferred_element_type=F32).astype(BF16)

    pre = jnp.dot(krga.astype(BF16), wgk_ref[...], preferred_element_type=F32) + bgk_ref[...]
    lf_ref[...] = (jnp.minimum(pre, 0.0) - jnp.log1p(jnp.exp(-jnp.abs(pre)))) * (1.0 / GATE_NORMALIZER)

    gq_ref[...] = jnp.dot(xb, w_in_ref[:, Z_GQ:Z_GK], preferred_element_type=F32)
    gk_ref[...] = jnp.dot(xb, w_in_ref[:, Z_GK:Z_GV], preferred_element_type=F32)
    gv_ref[...] = jnp.dot(xb, w_in_ref[:, Z_GV:Z_GATE], preferred_element_type=F32)
    gate_ref[...] = jnp.dot(xb, w_in_ref[:, Z_GATE:Z_WIDTH], preferred_element_type=F32)


def _rotate_half_cols(w):
    half = ROPE_DIM // 2
    return jnp.concatenate([-w[..., half:], w[..., :half]], axis=-1)


def _mixer_weights(w_in, w_uq, w_ukv, w_gk2):
    pts = []
    acc = 0
    for s in (Q_RANK, KV_RANK, ROPE_DIM, GLA_KW, GLA_KW, GLA_WIDTH, GLA_WIDTH):
        acc += s
        pts.append(acc)
    cq, ckv, kr, gq, gk, gv, gate, ga = jnp.split(w_in, pts, axis=1)
    d = w_in.shape[0]
    w_in_r = jnp.concatenate(
        [cq, ckv, kr, ga, jnp.zeros((d, LANES - ROPE_DIM - GATE_RANK), F32),
         _rotate_half_cols(kr), jnp.zeros((d, LANES - ROPE_DIM), F32), gq, gk, gv, gate], axis=1).astype(BF16)

    wq = w_uq.reshape(Q_RANK, MLA_HEADS, NOPE_DIM + ROPE_DIM)
    wq_rope = wq[..., NOPE_DIM:]
    pad = jnp.zeros((Q_RANK, MLA_HEADS, QK_PAD - NOPE_DIM - ROPE_DIM), F32)
    wq_main = jnp.concatenate([wq, pad], axis=-1).reshape(Q_RANK, MLA_HEADS * QK_PAD)
    wq_rot = jnp.concatenate([_rotate_half_cols(wq_rope), pad], axis=-1).reshape(Q_RANK, MLA_HEADS * LANES)
    wq_r = jnp.concatenate([wq_main, wq_rot], axis=1).astype(BF16)

    wkv = w_ukv.reshape(KV_RANK, MLA_HEADS, NOPE_DIM + V_DIM)
    wukv_r = jnp.concatenate([wkv[..., :NOPE_DIM].reshape(KV_RANK, -1),
                              wkv[..., NOPE_DIM:].reshape(KV_RANK, -1)], axis=1).astype(BF16)

    wvt_r = wkv[..., NOPE_DIM:].reshape(KV_RANK, -1).T.astype(BF16)

    wgk_r = jnp.zeros((LANES, GLA_KW), F32).at[ROPE_DIM:ROPE_DIM + GATE_RANK].set(w_gk2).astype(BF16)
    return w_in_r, wq_r, wukv_r, wvt_r, wgk_r


def _rope_tables(pos):
    half = ROPE_DIM // 2
    freqs = ROPE_THETA ** (-jnp.arange(half, dtype=F32) / half)
    ang = pos.astype(F32)[:, None] * freqs[None, :]
    zeros = jnp.zeros((pos.shape[0], LANES - ROPE_DIM), F32)
    cos = jnp.concatenate([jnp.cos(ang), jnp.cos(ang), zeros], axis=1)
    sin = jnp.concatenate([jnp.sin(ang), jnp.sin(ang), zeros], axis=1)
    return cos, sin


def _mixer(xp, xs, cos, sin, w_in_r, wq_r, wukv_r, wvt_r, wgk_r, q_norm_g, kv_norm_g, b_gk):
    tp, ts = xp.shape[0], xs.shape[0]
    t = tp + ts
    tm = MIX_TM
    assert tp % tm == 0 and ts % tm == 0
    n_p = tp // tm
    row = lambda w: pl.BlockSpec((tm, w), lambda i: (i, 0))
    col = pl.BlockSpec((MLA_WIDTH, tm), lambda i: (0, i))
    full = lambda w, dt: (row(w), jax.ShapeDtypeStruct((t, w), dt))
    ckv_p, ckv_s = _two_stream_specs(tm, KV_RANK, n_p)
    kr_p, kr_s = _two_stream_specs(tm, ROPE_DIM, n_p)
    outs = [full(MLA_HEADS * QK_PAD, BF16), full(MLA_HEADS * QK_PAD, BF16), full(MLA_WIDTH, BF16),
            (col, jax.ShapeDtypeStruct((MLA_WIDTH, t), BF16)),
            (ckv_p, jax.ShapeDtypeStruct((tp, KV_RANK), F32)), (ckv_s, jax.ShapeDtypeStruct((ts, KV_RANK), F32)),
            (kr_p, jax.ShapeDtypeStruct((tp, ROPE_DIM), F32)), (kr_s, jax.ShapeDtypeStruct((ts, ROPE_DIM), F32)),
            full(GLA_KW, F32), full(GLA_KW, F32), full(GLA_WIDTH, F32), full(GLA_WIDTH, F32), full(GLA_KW, F32)]
    return pl.pallas_call(
        functools.partial(_mixer_kernel, n_prompt_tiles=n_p),
        grid=(t // tm,),
        in_specs=[*_two_stream_specs(tm, D_MODEL, n_p), _const_spec(w_in_r.shape), _const_spec(wq_r.shape),
                  _const_spec(wukv_r.shape), _const_spec(wvt_r.shape), _const_spec(wgk_r.shape),
                  _const_spec((1, Q_RANK)), _const_spec((1, KV_RANK)), _const_spec((1, GLA_KW)),
                  row(LANES), row(LANES)],
        out_specs=[o[0] for o in outs],
        out_shape=[o[1] for o in outs],
        compiler_params=pltpu.CompilerParams(dimension_semantics=("arbitrary",),
                                             vmem_limit_bytes=VMEM_LIMIT_BYTES),
        name="mixer",
    )(xp, xs, w_in_r, wq_r, wukv_r, wvt_r, wgk_r, q_norm_g.reshape(1, -1), kv_norm_g.reshape(1, -1),
      b_gk.reshape(1, -1), cos, sin)


MLA_TQ = 512
MLA_LOOKAHEAD = 2
MLA_SIDE_ROWS = 128
NEG_BIG = -1e30
_NT = (((1,), (1,)), ((), ()))


def _mla_prompt_kernel(qi_ref, ki_ref, q_ref, k_ref, vt_ref, u_ref, v_ref, o_ref, u8_ref, uinv_ref, v8_ref, vinv_ref,
                       m_ref, l_ref, acc_ref, *, side_steps):
    step = pl.program_id(0)
    qi = qi_ref[step]
    ki = ki_ref[step]
    tq = q_ref.shape[0]
    tk = k_ref.shape[0]

    @pl.when(step < side_steps)
    def _():
        for src, dst, inv in ((u_ref, u8_ref, uinv_ref), (v_ref, v8_ref, vinv_ref)):
            x = src[...]
            amax = jnp.max(jnp.max(jnp.abs(x), axis=0, keepdims=True), axis=1, keepdims=True)
            amax = jnp.maximum(amax, FP8_TINY)
            dst[...] = _fp8_scaled(x, amax)
            inv[0] = jnp.broadcast_to(amax * (1.0 / FP8_TARGET), inv.shape[1:])

    @pl.when(ki == 0)
    def _():
        m_ref[...] = jnp.full(m_ref.shape, NEG_BIG, F32)
        l_ref[...] = jnp.zeros(l_ref.shape, F32)
        acc_ref[...] = jnp.zeros(acc_ref.shape, F32)

    def update(masked):
        if masked:
            key_chunk = lax.broadcasted_iota(jnp.int32, (tk, tq), 0) // CHUNK
            qry_chunk = lax.broadcasted_iota(jnp.int32, (tk, tq), 1) // CHUNK
            visible = key_chunk <= qry_chunk
        def scores(h):
            return lax.dot_general(k_ref[:, h * QK_PAD:(h + 1) * QK_PAD], q_ref[:, h * QK_PAD:(h + 1) * QK_PAD],
                                   _NT, preferred_element_type=F32)

        ahead = [scores(h) for h in range(MLA_LOOKAHEAD)]
        for h in range(MLA_HEADS):
            s = ahead.pop(0)
            if h + MLA_LOOKAHEAD < MLA_HEADS:
                ahead.append(scores(h + MLA_LOOKAHEAD))
            if masked:
                s = jnp.where(visible, s, NEG_BIG)
            m_old = m_ref[h]
            m_new = jnp.maximum(m_old, jnp.max(s, axis=0, keepdims=True))
            p = jnp.exp2(s - m_new)
            alpha = jnp.exp2(m_old - m_new)
            l_ref[h] = alpha * l_ref[h] + jnp.sum(p, axis=0, keepdims=True)
            acc_ref[h] = alpha * acc_ref[h] + jnp.dot(vt_ref[h * V_DIM:(h + 1) * V_DIM, :], p.astype(BF16),
                                                      preferred_element_type=F32)
            m_ref[h] = m_new

    @pl.when(ki < qi)
    def _():
        update(False)

    @pl.when(ki == qi)
    def _():
        update(True)
        for h in range(MLA_HEADS):
            o_ref[:, h * V_DIM:(h + 1) * V_DIM] = (acc_ref[h] / l_ref[h]).T.astype(BF16)


def _mla_prompt(q, k, vt, n_tokens, table_u, table_v):
    tq = MLA_TQ
    assert n_tokens % tq == 0
    nq = n_tokens // tq
    pairs = [(a, b) for a in range(nq) for b in range(a + 1)]
    qi = jnp.asarray([p[0] for p in pairs], jnp.int32)
    ki = jnp.asarray([p[1] for p in pairs], jnp.int32)
    rows, cols = table_u.shape
    assert table_v.shape == (rows, cols) and rows % MLA_SIDE_ROWS == 0
    side_steps = rows // MLA_SIDE_ROWS
    assert side_steps <= len(pairs)
    side = lambda: pl.BlockSpec((MLA_SIDE_ROWS, cols), lambda s, qi, ki: (jnp.minimum(s, side_steps - 1), 0))
    inv = lambda: pl.BlockSpec((1, SUBLANES, LANES), lambda s, qi, ki: (jnp.minimum(s, side_steps - 1), 0, 0))
    grid_spec = pltpu.PrefetchScalarGridSpec(
        num_scalar_prefetch=2,
        grid=(len(pairs),),
        in_specs=[pl.BlockSpec((tq, MLA_HEADS * QK_PAD), lambda s, qi, ki: (qi[s], 0)),
                  pl.BlockSpec((tq, MLA_HEADS * QK_PAD), lambda s, qi, ki: (ki[s], 0)),
                  pl.BlockSpec((MLA_WIDTH, tq), lambda s, qi, ki: (0, ki[s])), side(), side()],
        out_specs=[pl.BlockSpec((tq, MLA_WIDTH), lambda s, qi, ki: (qi[s], 0)), side(), inv(), side(), inv()],
        scratch_shapes=[pltpu.VMEM((MLA_HEADS, 1, tq), F32), pltpu.VMEM((MLA_HEADS, 1, tq), F32),
                        pltpu.VMEM((MLA_HEADS, V_DIM, tq), F32)],
    )
    table8 = jax.ShapeDtypeStruct((rows, cols), FP8)
    inv_shape = jax.ShapeDtypeStruct((side_steps, SUBLANES, LANES), F32)
    o, u8, u_inv, v8, v_inv = pl.pallas_call(
        functools.partial(_mla_prompt_kernel, side_steps=side_steps),
        grid_spec=grid_spec,
        out_shape=[jax.ShapeDtypeStruct((n_tokens, MLA_WIDTH), BF16), table8, inv_shape, table8, inv_shape],
        compiler_params=pltpu.CompilerParams(dimension_semantics=("arbitrary",),
                                             vmem_limit_bytes=VMEM_LIMIT_BYTES),
        name="mla_prompt",
    )(qi, ki, q, k, vt, table_u, table_v)
    return o, (u8, u_inv), (v8, v_inv)


def _mla_sample_kernel(q_ref, kn_ref, vn_ref, lat_ref, krc_ref, wukv_ref, o_ref):
    kvc = jnp.dot(lat_ref[0].astype(BF16), wukv_ref[...], preferred_element_type=F32)
    krc = krc_ref[0]
    for h in range(MLA_HEADS):
        q = q_ref[:, h * QK_PAD:(h + 1) * QK_PAD]
        knc = kvc[:, h * NOPE_DIM:(h + 1) * NOPE_DIM].astype(BF16)
        vc = kvc[:, MLA_HEADS * NOPE_DIM + h * V_DIM:MLA_HEADS * NOPE_DIM + (h + 1) * V_DIM].astype(BF16)
        s_c = (lax.dot_general(q[:, :NOPE_DIM], knc, _NT, preferred_element_type=F32)
               + lax.dot_general(q[:, NOPE_DIM:], krc, _NT, preferred_element_type=F32))
        s_n = lax.dot_general(q, kn_ref[:, h * QK_PAD:(h + 1) * QK_PAD], _NT, preferred_element_type=F32)
        m = jnp.maximum(jnp.max(s_c, axis=1, keepdims=True), jnp.max(s_n, axis=1, keepdims=True))
        p_c = jnp.exp2(s_c - m)
        p_n = jnp.exp2(s_n - m)
        l = jnp.sum(p_c, axis=1, keepdims=True) + jnp.sum(p_n, axis=1, keepdims=True)
        o = (jnp.dot(p_c.astype(BF16), vc, preferred_element_type=F32)
             + jnp.dot(p_n.astype(BF16), vn_ref[:, h * V_DIM:(h + 1) * V_DIM], preferred_element_type=F32))
        o_ref[:, h * V_DIM:(h + 1) * V_DIM] = (o / l).astype(BF16)


def _mla_sample(q, k, v, cache_lat, cache_kr, wukv_r, row0, n_new):
    bs, past, _ = cache_lat.shape
    assert row0 % n_new == 0
    blk0 = row0 // n_new
    krc = jnp.pad(cache_kr, ((0, 0), (0, 0), (0, LANES - ROPE_DIM))).astype(BF16)
    new = lambda w: pl.BlockSpec((n_new, w), lambda b: (blk0 + b, 0))
    return pl.pallas_call(
        _mla_sample_kernel,
        grid=(bs,),
        in_specs=[new(MLA_HEADS * QK_PAD), new(MLA_HEADS * QK_PAD), new(MLA_WIDTH),
                  pl.BlockSpec((1, past, KV_RANK), lambda b: (b, 0, 0)),
                  pl.BlockSpec((1, past, LANES), lambda b: (b, 0, 0)),
                  _const_spec(wukv_r.shape)],
        out_specs=pl.BlockSpec((n_new, MLA_WIDTH), lambda b: (b, 0)),
        out_shape=jax.ShapeDtypeStruct((bs * n_new, MLA_WIDTH), BF16),
        compiler_params=pltpu.CompilerParams(dimension_semantics=("arbitrary",),
                                             vmem_limit_bytes=VMEM_LIMIT_BYTES),
        name="mla_sample",
    )(q, k, v, cache_lat, krc, wukv_r)


GLA_CHUNKS_PER_STEP = 2
GLA_SUB = 16
_TN = (((0,), (0,)), ((), ()))


def _cumsum_rows(x):
    n = x.shape[0]
    row = lax.broadcasted_iota(jnp.int32, x.shape, 0)
    s = 1
    while s < n:
        x = x + jnp.where(row >= s, pltpu.roll(x, s, axis=0), 0.0)
        s *= 2
    return x


def _gla_kernel(q_ref, k_ref, v_ref, lf_ref, gate_ref, s0_ref, g_ref, o_ref, send_ref, st_ref, *, n_chunks):
    c = pl.program_id(1)
    n_rows = CHUNK
    n_sub = n_rows // GLA_SUB

    @pl.when(c == 0)
    def _():
        for h in range(GLA_HEADS):
            st_ref[h] = s0_ref[0, h].T

    sub_row = lax.broadcasted_iota(jnp.int32, (GLA_SUB, GLA_DK), 0)
    for h, ci in [(h, ci) for h in range(GLA_HEADS) for ci in range(q_ref.shape[0] // CHUNK)]:
        rows = slice(ci * CHUNK, (ci + 1) * CHUNK)
        ks = slice(h * GLA_DK, (h + 1) * GLA_DK)
        vs = slice(h * GLA_DV, (h + 1) * GLA_DV)
        q = q_ref[rows, ks] * (GLA_DK ** -0.5)
        k = k_ref[rows, ks]
        v = v_ref[rows, vs]
        vb = v.astype(BF16)
        b = _cumsum_rows(lf_ref[rows, ks])
        b_last = b[n_rows - 1:n_rows]
        st = st_ref[h]
        o_inter = lax.dot_general((q * jnp.exp(b)).astype(BF16), st.astype(BF16), _NT,
                                  preferred_element_type=F32)
        outs = []
        for i in range(n_sub):
            r0 = i * GLA_SUB
            bi = b[r0:r0 + GLA_SUB]
            qi = q[r0:r0 + GLA_SUB]
            o_i = o_inter[r0:r0 + GLA_SUB]
            if i > 0:
                b_ref = b[r0 - 1:r0]
                qh = (qi * jnp.exp(bi - b_ref)).astype(BF16)
                kh = (k[:r0] * jnp.exp(b_ref - b[:r0])).astype(BF16)
                a_off = lax.dot_general(qh, kh, _NT, preferred_element_type=F32)
                o_i = o_i + jnp.dot(a_off.astype(BF16), vb[:r0], preferred_element_type=F32)
            for j in range(GLA_SUB):
                r = r0 + j
                decay = jnp.exp(jnp.where(sub_row >= j, bi - b[r:r + 1], -jnp.inf))
                a_col = jnp.sum(qi * k[r:r + 1] * decay, axis=1, keepdims=True)
                o_i = o_i + a_col * v[r:r + 1]
            outs.append(o_i)
        o = jnp.concatenate(outs, axis=0)

        kd = (k * jnp.exp(b_last - b)).astype(BF16)
        st_new = st * jnp.exp(b_last) + lax.dot_general(vb, kd, _TN, preferred_element_type=F32)
        st_ref[h] = st_new

        gate = gate_ref[rows, vs]
        on = _rms(o, g_ref[...])
        o_ref[rows, vs] = (on * (gate / (1.0 + jnp.exp(-gate)))).astype(BF16)

    @pl.when(c == n_chunks - 1)
    def _():
        for h in range(GLA_HEADS):
            send_ref[0, h] = st_ref[h].T


def _gla(gq, gk, gv, lf, gate, s0, gla_norm_g, row0, n_seq, n_chunks):
    per_step = GLA_CHUNKS_PER_STEP if n_chunks % GLA_CHUNKS_PER_STEP == 0 else 1
    tr = per_step * CHUNK
    assert row0 % tr == 0
    blk0 = row0 // tr
    n_steps = n_chunks // per_step
    row = lambda w: pl.BlockSpec((tr, w), lambda b, c: (blk0 + b * n_steps + c, 0))
    state = pl.BlockSpec((1, GLA_HEADS, GLA_DK, GLA_DV), lambda b, c: (b, 0, 0, 0))
    return pl.pallas_call(
        functools.partial(_gla_kernel, n_chunks=n_steps),
        grid=(n_seq, n_steps),
        in_specs=[row(GLA_KW), row(GLA_KW), row(GLA_WIDTH), row(GLA_KW), row(GLA_WIDTH), state,
                  pl.BlockSpec((1, GLA_DV), lambda b, c: (0, 0))],
        out_specs=[pl.BlockSpec((tr, GLA_WIDTH), lambda b, c: (b * n_steps + c, 0)), state],
        out_shape=[jax.ShapeDtypeStruct((n_seq * n_chunks * CHUNK, GLA_WIDTH), BF16),
                   jax.ShapeDtypeStruct((n_seq, GLA_HEADS, GLA_DK, GLA_DV), F32)],
        scratch_shapes=[pltpu.VMEM((GLA_HEADS, GLA_DV, GLA_DK), F32)],
        compiler_params=pltpu.CompilerParams(dimension_semantics=("arbitrary", "arbitrary"),
                                             vmem_limit_bytes=VMEM_LIMIT_BYTES),
        name="gla",
    )(gq, gk, gv, lf, gate, s0, gla_norm_g.reshape(1, -1))


OUT_TM = 256


def _layer_norm(y, g, b):
    mu = jnp.mean(y, axis=-1, keepdims=True)
    yc = y - mu
    var = jnp.mean(yc * yc, axis=-1, keepdims=True)
    return yc * lax.rsqrt(var + LN_EPS) * g + b


def _outproj_kernel(mlap_ref, mlas_ref, glap_ref, glas_ref, xp_ref, xs_ref, wo_ref, g_ref, b_ref, wq_ref,
                    h_ref, h8_ref, hinv_ref, qp_ref, *, n_prompt_tiles):
    in_prompt = pl.program_id(0) < n_prompt_tiles
    mla = jnp.where(in_prompt, mlap_ref[...], mlas_ref[...])
    gla = jnp.where(in_prompt, glap_ref[...], glas_ref[...])
    x = jnp.where(in_prompt, xp_ref[...], xs_ref[...])
    mix = (jnp.dot(mla, wo_ref[:MLA_WIDTH], preferred_element_type=F32)
           + jnp.dot(gla, wo_ref[MLA_WIDTH:], preferred_element_type=F32))
    h = _layer_norm(DN_ALPHA * x + mix, g_ref[...], b_ref[...])
    h_ref[...] = h
    amax = jnp.maximum(jnp.max(jnp.abs(h), axis=1, keepdims=True), FP8_TINY)
    h8_ref[...] = _fp8_scaled(h, amax)
    hinv_ref[...] = jnp.broadcast_to(amax * (1.0 / FP8_TARGET), hinv_ref.shape)
    qp = jnp.dot(h.astype(BF16), wq_ref[...], preferred_element_type=F32).astype(BF16)
    per_head = qp_ref.shape[2]
    for head in range(qp_ref.shape[0]):
        qp_ref[head] = qp[:, head * per_head:(head + 1) * per_head]


def _outproj(mla_p, mla_s, gla_p, gla_s, xp, xs, w_o_b, ln1_g, ln1_b, peer_wq_b):
    tp, ts = xp.shape[0], xs.shape[0]
    t = tp + ts
    tm = OUT_TM
    assert tp % tm == 0 and ts % tm == 0
    n_p = tp // tm
    row = lambda w: pl.BlockSpec((tm, w), lambda i: (i, 0))
    return pl.pallas_call(
        functools.partial(_outproj_kernel, n_prompt_tiles=n_p),
        grid=(t // tm,),
        in_specs=[*_two_stream_specs(tm, MLA_WIDTH, n_p), *_two_stream_specs(tm, GLA_WIDTH, n_p),
                  *_two_stream_specs(tm, D_MODEL, n_p), _const_spec(w_o_b.shape),
                  _const_spec((1, D_MODEL)), _const_spec((1, D_MODEL)), _const_spec(peer_wq_b.shape)],
        out_specs=[row(D_MODEL), row(D_MODEL), row(LANES),
                   pl.BlockSpec((PEER_HEADS, tm, 2 * PEER_HALF), lambda i: (0, i, 0))],
        out_shape=[jax.ShapeDtypeStruct((t, D_MODEL), F32), jax.ShapeDtypeStruct((t, D_MODEL), FP8),
                   jax.ShapeDtypeStruct((t, LANES), F32),
                   jax.ShapeDtypeStruct((PEER_HEADS, t, 2 * PEER_HALF), BF16)],
        compiler_params=pltpu.CompilerParams(dimension_semantics=("arbitrary",),
                                             vmem_limit_bytes=VMEM_LIMIT_BYTES),
        name="outproj",
    )(mla_p, mla_s, gla_p, gla_s, xp, xs, w_o_b, ln1_g.reshape(1, -1), ln1_b.reshape(1, -1), peer_wq_b)


PEER_HEADS = 8
N_KEYS = 128
PEER_HALF = 128
PEER_TOPK = 16
ROUTE_TM = 256
CAND_WIDE_RANKS = 8
CAND_ROWS = PEER_TOPK + (CAND_WIDE_RANKS - 1) * SUBLANES + (PEER_TOPK - CAND_WIDE_RANKS)


def _first_max(x, row):
    n = x.shape[0]
    blocks = [(x[r:r + SUBLANES], row[r:r + SUBLANES]) for r in range(0, n, SUBLANES)]
    while len(blocks) > 1:
        merged = []
        for j in range(0, len(blocks) - 1, 2):
            (va, ia), (vb, ib) = blocks[j], blocks[j + 1]
            take = va >= vb
            merged.append((jnp.where(take, va, vb), jnp.where(take, ia, ib)))
        if len(blocks) % 2:
            merged.append(blocks[-1])
        blocks = merged
    v, i = blocks[0]
    m = jnp.max(v, axis=0, keepdims=True)
    pick = jnp.min(jnp.where(v == m, i, float(n)), axis=0, keepdims=True)
    return m, pick


def _extract_top(xs, row, n_out, on_pick):
    xs = list(xs)
    for k in range(n_out):
        for p, x in enumerate(xs):
            m, pick = _first_max(x, row)
            hit = row == pick
            on_pick(p, k, m, hit, pick)
            xs[p] = jnp.where(hit, -jnp.inf, x)


def _route_kernel(qp_ref, keys_ref, row_ref, g_ref, i1_ref, i2_ref, sv_ref, si_ref, gt_ref, et_ref):
    lax.fori_loop(0, PEER_HEADS,
                  functools.partial(_route_head, qp_ref, keys_ref, row_ref, sv_ref, si_ref, gt_ref, et_ref), 0)
    g_ref[...] = gt_ref[...].T
    expert = et_ref[...]
    key0 = jnp.floor(expert * (1.0 / N_KEYS))
    i1_ref[...] = key0.T
    i2_ref[...] = (expert - key0 * N_KEYS).T


def _route_head(qp_ref, keys_ref, row_ref, sv_ref, si_ref, gt_ref, et_ref, h, carry):
    tm = qp_ref.shape[1]
    q = qp_ref[h]
    scores = [lax.dot_general(keys_ref[half], q[:, half * PEER_HALF:(half + 1) * PEER_HALF], _NT,
                              preferred_element_type=F32) for half in range(2)]

    def keep(half, k, m, hit, pick):
        sv_ref[half, k:k + 1, :] = m
        si_ref[half, k:k + 1, :] = pick

    _extract_top(scores, row_ref[...], PEER_TOPK, keep)

    s1, s2 = sv_ref[0], sv_ref[1]
    i1, i2 = si_ref[0], si_ref[1]
    bc = lambda r, n: jnp.broadcast_to(r, (n, tm))
    wide = range(1, CAND_WIDE_RANKS)
    cand = jnp.concatenate([bc(s1[0:1], PEER_TOPK) + s2]
                           + [bc(s1[a:a + 1], SUBLANES) + s2[:SUBLANES] for a in wide]
                           + [s1[CAND_WIDE_RANKS:] + bc(s2[0:1], PEER_TOPK - CAND_WIDE_RANKS)], axis=0)
    i1 = i1 * N_KEYS
    c_expert = jnp.concatenate([bc(i1[0:1], PEER_TOPK) + i2]
                               + [bc(i1[a:a + 1], SUBLANES) + i2[:SUBLANES] for a in wide]
                               + [i1[CAND_WIDE_RANKS:] + bc(i2[0:1], PEER_TOPK - CAND_WIDE_RANKS)], axis=0)
    base = pl.multiple_of(h * PEER_TOPK, PEER_TOPK)

    def keep_pair(_, k, m, hit, pick):
        gt_ref[pl.ds(base + k, 1), :] = m
        et_ref[pl.ds(base + k, 1), :] = jnp.sum(jnp.where(hit, c_expert, 0.0), axis=0, keepdims=True)

    _extract_top([cand], row_ref[:CAND_ROWS], PEER_TOPK, keep_pair)

    sc = gt_ref[pl.ds(base, PEER_TOPK), :]
    e = jnp.exp(sc - sc[0:1])
    gt_ref[pl.ds(base, PEER_TOPK), :] = e / jnp.sum(e, axis=0, keepdims=True)
    return carry


def _route(qp, keys_b):
    t = qp.shape[1]
    tm = ROUTE_TM
    assert t % tm == 0
    n_sel = PEER_HEADS * PEER_TOPK
    out = pl.BlockSpec((tm, n_sel), lambda i: (i, 0))
    row_index = jnp.broadcast_to(jnp.arange(N_KEYS, dtype=F32)[:, None], (N_KEYS, tm))
    return pl.pallas_call(
        _route_kernel,
        grid=(t // tm,),
        in_specs=[pl.BlockSpec((PEER_HEADS, tm, 2 * PEER_HALF), lambda i: (0, i, 0)),
                  pl.BlockSpec(keys_b.shape, lambda i: (0, 0, 0)),
                  pl.BlockSpec((N_KEYS, tm), lambda i: (0, 0))],
        out_specs=[out, out, out],
        out_shape=[jax.ShapeDtypeStruct((t, n_sel), F32)] * 3,
        scratch_shapes=[pltpu.VMEM((2, PEER_TOPK, tm), F32), pltpu.VMEM((2, PEER_TOPK, tm), F32),
                        pltpu.VMEM((n_sel, tm), F32), pltpu.VMEM((n_sel, tm), F32)],
        compiler_params=pltpu.CompilerParams(dimension_semantics=("arbitrary",),
                                             vmem_limit_bytes=VMEM_LIMIT_BYTES),
        name="peer_route",
    )(qp, keys_b, row_index)


SCATTER_TB = 128
SCATTER_UNROLL = 32
W_ROW_TILE = SUBLANES


def _w_table_tokens(t):
    return -(-t // EXP_TM) * EXP_TM


def _scatter_kernel(g_ref, i1_ref, i2_ref, w_ref, *, n_token_blocks):
    tb = g_ref.shape[0]
    key = lax.broadcasted_iota(jnp.int32, (N_KEYS, g_ref.shape[1]), 0).astype(F32)

    @pl.when(pl.program_id(0) >= n_token_blocks)
    def _():
        w_ref[...] = jnp.zeros(w_ref.shape, F32)

    def body(grp, carry):
        t0 = pl.multiple_of(grp * SCATTER_UNROLL, SCATTER_UNROLL)
        g8 = g_ref[pl.ds(t0, SCATTER_UNROLL), :]
        i18 = i1_ref[pl.ds(t0, SCATTER_UNROLL), :]
        i28 = i2_ref[pl.ds(t0, SCATTER_UNROLL), :]
        zero = jnp.zeros(key.shape, BF16)
        for u in range(0, SCATTER_UNROLL, 2):
            a_ts, b_ts = [], []
            for v in (u, u + 1):
                g = jnp.broadcast_to(g8[v:v + 1], key.shape)
                i1 = jnp.broadcast_to(i18[v:v + 1], key.shape)
                i2 = jnp.broadcast_to(i28[v:v + 1], key.shape)
                a_ts.append(jnp.where(key == i1, g, 0.0).astype(BF16))
                b_ts.append(jnp.where(key == i2, 1.0, 0.0).astype(BF16))
            a_pair = jnp.concatenate(a_ts, axis=1)
            b_pair = jnp.concatenate([jnp.concatenate([b_ts[0], zero], axis=1),
                                      jnp.concatenate([zero, b_ts[1]], axis=1)], axis=0)
            w = lax.dot_general(a_pair, b_pair, _NT, preferred_element_type=F32)
            for n, v in enumerate((u, u + 1)):
                w_ref[:, pl.ds(t0 + v, 1), :, :] = w[:, n * N_KEYS:(n + 1) * N_KEYS].reshape(
                    N_KEYS // W_ROW_TILE, 1, W_ROW_TILE, N_KEYS)
        return carry

    @pl.when(pl.program_id(0) < n_token_blocks)
    def _():
        lax.fori_loop(0, tb // SCATTER_UNROLL, body, 0)


def _scatter(g, i1, i2):
    t = g.shape[0]
    tb = SCATTER_TB
    t_pad = _w_table_tokens(t)
    assert t % tb == 0 and t_pad % tb == 0
    n_blocks = t // tb
    sel = pl.BlockSpec((tb, g.shape[1]), lambda i: (jnp.minimum(i, n_blocks - 1), 0))
    n_grp = N_KEYS // W_ROW_TILE
    return pl.pallas_call(
        functools.partial(_scatter_kernel, n_token_blocks=n_blocks),
        grid=(t_pad // tb,),
        in_specs=[sel, sel, sel],
        out_specs=pl.BlockSpec((n_grp, tb, W_ROW_TILE, N_KEYS), lambda i: (0, i, 0, 0)),
        out_shape=jax.ShapeDtypeStruct((n_grp, t_pad, W_ROW_TILE, N_KEYS), F32),
        compiler_params=pltpu.CompilerParams(dimension_semantics=("arbitrary",),
                                             vmem_limit_bytes=VMEM_LIMIT_BYTES),
        name="peer_scatter",
    )(g, i1, i2)


EXP_TM = 1024
EXP_ROWS = 4
EXP_TE = EXP_ROWS * N_KEYS
EXP_SUBTILES = 4


def _gelu(x):
    return x * 0.5 * (1.0 + lax.erf(x * (2.0 ** -0.5)))


def _experts_kernel(h8_ref, hinv_ref, w_ref, u_ref, uinv_ref, v_ref, vinv_ref, h_ref, g_ref, b_ref, y_ref):
    j = pl.program_id(1)
    tm = h8_ref.shape[0]

    @pl.when(j == 0)
    def _():
        y_ref[...] = DN_ALPHA * h_ref[...]

    row0 = (j % (W_ROW_TILE // EXP_ROWS)) * EXP_ROWS
    sub = tm // EXP_SUBTILES
    u_inv = jnp.concatenate([uinv_ref[r, 0:1, :] for r in range(EXP_ROWS)], axis=1)
    v_inv = jnp.concatenate([vinv_ref[r, 0:1, :] for r in range(EXP_ROWS)], axis=1)
    acts = []
    for s in range(EXP_SUBTILES):
        h_inv = hinv_ref[s * sub:(s + 1) * sub, :]
        a8 = lax.dot_general(h8_ref[s * sub:(s + 1) * sub, :], u_ref[...], _NT, preferred_element_type=F32)
        acts.append(a8 * jnp.concatenate([h_inv] * EXP_ROWS, axis=1) * u_inv)
    for s in range(EXP_SUBTILES):
        w = jnp.concatenate([w_ref[pl.ds(s * sub * W_ROW_TILE + row0 + r, sub, stride=W_ROW_TILE), :]
                             for r in range(EXP_ROWS)], axis=1)
        p = w * _gelu(acts[s]) * v_inv
        pmax = jnp.maximum(jnp.max(jnp.abs(p), axis=1, keepdims=True), FP8_TINY)
        out = jnp.dot(_fp8_scaled(p, pmax), v_ref[...], preferred_element_type=F32)
        y_ref[s * sub:(s + 1) * sub, :] += out * (pmax * (1.0 / FP8_TARGET))

    @pl.when(j == pl.num_programs(1) - 1)
    def _():
        y_ref[...] = _layer_norm(y_ref[...], g_ref[...], b_ref[...])


def _experts(h8, h_inv, h, w_table, u8, u_inv, v8, v_inv, ln2_g, ln2_b, row0, n_rows):
    tm = min(EXP_TM, n_rows)
    assert n_rows % tm == 0 and row0 % tm == 0 and tm % (EXP_SUBTILES * FP8_ROW_TILE) == 0
    nt = n_rows // tm
    blk0 = row0 // tm
    ne = u8.shape[0] // EXP_TE
    steps_per_group = W_ROW_TILE // EXP_ROWS
    tiles_per_group = w_table.shape[1] // tm
    w2d = w_table.reshape(-1, N_KEYS)
    once = lambda w: pl.BlockSpec((tm, w), lambda i, j: (blk0 + i, 0), pipeline_mode=pl.Buffered(1))
    return pl.pallas_call(
        _experts_kernel,
        grid=(nt, ne),
        in_specs=[once(D_MODEL), once(LANES),
                  pl.BlockSpec((tm * W_ROW_TILE, N_KEYS),
                               lambda i, j: ((j // steps_per_group) * tiles_per_group + blk0 + i, 0)),
                  pl.BlockSpec((EXP_TE, D_MODEL), lambda i, j: (j, 0)),
                  pl.BlockSpec((EXP_ROWS, SUBLANES, LANES), lambda i, j: (j, 0, 0)),
                  pl.BlockSpec((EXP_TE, D_MODEL), lambda i, j: (j, 0)),
                  pl.BlockSpec((EXP_ROWS, SUBLANES, LANES), lambda i, j: (j, 0, 0)),
                  once(D_MODEL), pl.BlockSpec((1, D_MODEL), lambda i, j: (0, 0)),
                  pl.BlockSpec((1, D_MODEL), lambda i, j: (0, 0))],
        out_specs=pl.BlockSpec((tm, D_MODEL), lambda i, j: (i, 0)),
        out_shape=jax.ShapeDtypeStruct((n_rows, D_MODEL), F32),
        compiler_params=pltpu.CompilerParams(dimension_semantics=("arbitrary", "arbitrary"),
                                             vmem_limit_bytes=VMEM_LIMIT_BYTES),
        name="peer_experts",
    )(h8, h_inv, w2d, u8, u_inv, v8, v_inv, h, ln2_g.reshape(1, -1), ln2_b.reshape(1, -1))


def kernel(x_prompt, x_sample, cache_kv_latent, cache_k_rope, state_gla, w_in, q_norm_g, w_uq, kv_norm_g, w_ukv,
           w_gk2, b_gk, gla_norm_g, w_o, ln1_g, ln1_b, peer_wq, peer_keys, peer_u, peer_v, ln2_g, ln2_b):
    bp, sp, _ = x_prompt.shape
    bs, ss, _ = x_sample.shape
    past = cache_kv_latent.shape[1]
    tp, ts = bp * sp, bs * ss
    xp = x_prompt.reshape(tp, D_MODEL)
    xs = x_sample.reshape(ts, D_MODEL)
    cos_p, sin_p = _rope_tables(jnp.arange(sp, dtype=jnp.int32))
    cos_s, sin_s = _rope_tables(past + jnp.arange(ss, dtype=jnp.int32))
    cos = jnp.concatenate([jnp.tile(cos_p, (bp, 1)), jnp.tile(cos_s, (bs, 1))])
    sin = jnp.concatenate([jnp.tile(sin_p, (bp, 1)), jnp.tile(sin_s, (bs, 1))])
    w_in_r, wq_r, wukv_r, wvt_r, wgk_r = _mixer_weights(w_in, w_uq, w_ukv, w_gk2)
    q, k, v, vt, ckv_p, ckv_s, kr_p, kr_s, gq, gk, gv, gate, lf = _mixer(
        xp, xs, cos, sin, w_in_r, wq_r, wukv_r, wvt_r, wgk_r, q_norm_g, kv_norm_g, b_gk)
    o_p, (u8, u_inv), (v8, v_inv) = _mla_prompt(q, k, vt, tp, peer_u, peer_v)
    o_s = _mla_sample(q, k, v, cache_kv_latent, cache_k_rope, wukv_r, tp, ss)
    g_p, st_p = _gla(gq, gk, gv, lf, gate, jnp.zeros((bp,) + state_gla.shape[1:], F32), gla_norm_g,
                     0, bp, sp // CHUNK)
    g_s, st_s = _gla(gq, gk, gv, lf, gate, state_gla, gla_norm_g, tp, bs, ss // CHUNK)

    h, h8, h_inv, qp = _outproj(o_p, o_s, g_p, g_s, xp, xs, w_o.astype(BF16), ln1_g, ln1_b, peer_wq.astype(BF16))
    gates, i1, i2 = _route(qp, peer_keys.astype(BF16))
    w_table = _scatter(gates, i1, i2)
    y_p = _experts(h8, h_inv, h, w_table, u8, u_inv, v8, v_inv, ln2_g, ln2_b, 0, tp)
    y_s = _experts(h8, h_inv, h, w_table, u8, u_inv, v8, v_inv, ln2_g, ln2_b, tp, ts)

    dt = x_prompt.dtype
    return (y_p.reshape(bp, sp, D_MODEL), y_s.reshape(bs, ss, D_MODEL),
            ckv_p.reshape(bp, sp, KV_RANK), kr_p.reshape(bp, sp, ROPE_DIM), st_p.astype(dt),
            ckv_s.reshape(bs, ss, KV_RANK), kr_s.reshape(bs, ss, ROPE_DIM), st_s.astype(dt))
```

```python
import functools
import math

import jax
import jax.numpy as jnp
from jax import lax
from jax.experimental import pallas as pl
from jax.experimental.pallas import tpu as pltpu

F32 = jnp.float32
BF16 = jnp.bfloat16
FP8 = jnp.float8_e4m3fn
FP8_ROW_TILE = 32
FP8_TARGET = 128.0
FP8_TINY = 1e-30


def _fp8_scaled(x, amax):
    return (x * (FP8_TARGET / amax)).astype(FP8)

LANES = 128
SUBLANES = 8
VMEM_LIMIT_BYTES = 60 * 1024 * 1024

D_MODEL = 2048
CHUNK = 64
MLA_HEADS = 8
Q_RANK = 512
KV_RANK = 256
NOPE_DIM = 128
ROPE_DIM = 64
V_DIM = 128
ROPE_THETA = 10000.0
MLA_SCALE = (NOPE_DIM + ROPE_DIM) ** -0.5
Q_SCALE = MLA_SCALE * math.log2(math.e)
QK_PAD = 256
GLA_HEADS = 4
GLA_DK = 128
GLA_DV = 256
GATE_RANK = 16
GATE_NORMALIZER = 16.0
GLA_KW = GLA_HEADS * GLA_DK
GLA_WIDTH = GLA_HEADS * GLA_DV
MLA_WIDTH = MLA_HEADS * V_DIM
DN_ALPHA = 2.0 ** 0.25
LN_EPS = 1e-5
RMS_EPS = 1e-6

Z_CQ = 0
Z_CKV = Z_CQ + Q_RANK
Z_KR = Z_CKV + KV_RANK
Z_KROT = Z_KR + LANES
Z_GQ = Z_KROT + LANES
Z_GK = Z_GQ + GLA_KW
Z_GV = Z_GK + GLA_KW
Z_GATE = Z_GV + GLA_WIDTH
Z_WIDTH = Z_GATE + GLA_WIDTH


def _const_spec(shape):
    nd = len(shape)
    return pl.BlockSpec(shape, lambda *_: (0,) * nd, pipeline_mode=pl.Buffered(1))


def _rms(x, g):
    return x * lax.rsqrt(jnp.mean(x * x, axis=-1, keepdims=True) + RMS_EPS) * g


MIX_TM = 256


def _two_stream_specs(tm, width, n_first):
    return (pl.BlockSpec((tm, width), lambda i: (jnp.minimum(i, n_first - 1), 0)),
            pl.BlockSpec((tm, width), lambda i: (jnp.maximum(i - n_first, 0), 0)))


def _mixer_kernel(xp_ref, xs_ref, w_in_ref, wq_ref, wukv_ref, wvt_ref, wgk_ref, qg_ref, kvg_ref, bgk_ref,
                  cos_ref, sin_ref, q_ref, k_ref, v_ref, vt_ref, ckvp_ref, ckvs_ref, krp_ref, krs_ref,
                  gq_ref, gk_ref, gv_ref, gate_ref, lf_ref, *, n_prompt_tiles):
    in_prompt = pl.program_id(0) < n_prompt_tiles
    xb = jnp.where(in_prompt, xp_ref[...], xs_ref[...]).astype(BF16)
    cos = cos_ref[...]
    sin = sin_ref[...]

    z_lat = jnp.dot(xb, w_in_ref[:, Z_CQ:Z_GQ], preferred_element_type=F32)
    cq = _rms(z_lat[:, Z_CQ:Z_CKV], qg_ref[...])
    c_kv = _rms(z_lat[:, Z_CKV:Z_KR], kvg_ref[...])
    krga = z_lat[:, Z_KR:Z_KROT]
    k_rope = krga * cos + z_lat[:, Z_KROT:Z_GQ] * sin

    @pl.when(in_prompt)
    def _():
        ckvp_ref[...] = c_kv
        krp_ref[...] = k_rope[:, :ROPE_DIM]

    @pl.when(jnp.logical_not(in_prompt))
    def _():
        ckvs_ref[...] = c_kv
        krs_ref[...] = k_rope[:, :ROPE_DIM]

    k_rope_b = k_rope.astype(BF16)

    q = jnp.dot(cq.astype(BF16), wq_ref[...], preferred_element_type=F32)
    kv = jnp.dot(c_kv.astype(BF16), wukv_ref[...], preferred_element_type=F32)
    for h in range(MLA_HEADS):
        lo = h * QK_PAD
        q_ref[:, lo:lo + NOPE_DIM] = (q[:, lo:lo + NOPE_DIM] * Q_SCALE).astype(BF16)
        rot = q[:, MLA_HEADS * QK_PAD + h * LANES:MLA_HEADS * QK_PAD + (h + 1) * LANES]
        q_ref[:, lo + NOPE_DIM:lo + QK_PAD] = (
            (q[:, lo + NOPE_DIM:lo + QK_PAD] * cos + rot * sin) * Q_SCALE).astype(BF16)
        k_ref[:, lo:lo + NOPE_DIM] = kv[:, h * NOPE_DIM:(h + 1) * NOPE_DIM].astype(BF16)
        k_ref[:, lo + NOPE_DIM:lo + QK_PAD] = k_rope_b
    v_ref[...] = kv[:, MLA_HEADS * NOPE_DIM:].astype(BF16)
    vt_ref[...] = jnp.dot(wvt_ref[...], c_kv.T.astype(BF16), preferred_element_type=F32).astype(BF16)

    pre = jnp.dot(krga.astype(BF16), wgk_ref[...], preferred_element_type=F32) + bgk_ref[...]
    lf_ref[...] = (jnp.minimum(pre, 0.0) - jnp.log1p(jnp.exp(-jnp.abs(pre)))) * (1.0 / GATE_NORMALIZER)

    gq_ref[...] = jnp.dot(xb, w_in_ref[:, Z_GQ:Z_GK], preferred_element_type=F32)
    gk_ref[...] = jnp.dot(xb, w_in_ref[:, Z_GK:Z_GV], preferred_element_type=F32)
    gv_ref[...] = jnp.dot(xb, w_in_ref[:, Z_GV:Z_GATE], preferred_element_type=F32)
    gate_ref[...] = jnp.dot(xb, w_in_ref[:, Z_GATE:Z_WIDTH], preferred_element_type=F32)


def _rotate_half_cols(w):
    half = ROPE_DIM // 2
    return jnp.concatenate([-w[..., half:], w[..., :half]], axis=-1)


def _mixer_weights(w_in, w_uq, w_ukv, w_gk2):
    pts = []
    acc = 0
    for s in (Q_RANK, KV_RANK, ROPE_DIM, GLA_KW, GLA_KW, GLA_WIDTH, GLA_WIDTH):
        acc += s
        pts.append(acc)
    cq, ckv, kr, gq, gk, gv, gate, ga = jnp.split(w_in, pts, axis=1)
    d = w_in.shape[0]
    w_in_r = jnp.concatenate(
        [cq, ckv, kr, ga, jnp.zeros((d, LANES - ROPE_DIM - GATE_RANK), F32),
         _rotate_half_cols(kr), jnp.zeros((d, LANES - ROPE_DIM), F32), gq, gk, gv, gate], axis=1).astype(BF16)

    wq = w_uq.reshape(Q_RANK, MLA_HEADS, NOPE_DIM + ROPE_DIM)
    wq_rope = wq[..., NOPE_DIM:]
    pad = jnp.zeros((Q_RANK, MLA_HEADS, QK_PAD - NOPE_DIM - ROPE_DIM), F32)
    wq_main = jnp.concatenate([wq, pad], axis=-1).reshape(Q_RANK, MLA_HEADS * QK_PAD)
    wq_rot = jnp.concatenate([_rotate_half_cols(wq_rope), pad], axis=-1).reshape(Q_RANK, MLA_HEADS * LANES)
    wq_r = jnp.concatenate([wq_main, wq_rot], axis=1).astype(BF16)

    wkv = w_ukv.reshape(KV_RANK, MLA_HEADS, NOPE_DIM + V_DIM)
    wukv_r = jnp.concatenate([wkv[..., :NOPE_DIM].reshape(KV_RANK, -1),
                              wkv[..., NOPE_DIM:].reshape(KV_RANK, -1)], axis=1).astype(BF16)

    wvt_r = wkv[..., NOPE_DIM:].reshape(KV_RANK, -1).T.astype(BF16)

    wgk_r = jnp.zeros((LANES, GLA_KW), F32).at[ROPE_DIM:ROPE_DIM + GATE_RANK].set(w_gk2).astype(BF16)
    return w_in_r, wq_r, wukv_r, wvt_r, wgk_r


def _rope_tables(pos):
    half = ROPE_DIM // 2
    freqs = ROPE_THETA ** (-jnp.arange(half, dtype=F32) / half)
    ang = pos.astype(F32)[:, None] * freqs[None, :]
    zeros = jnp.zeros((pos.shape[0], LANES - ROPE_DIM), F32)
    cos = jnp.concatenate([jnp.cos(ang), jnp.cos(ang), zeros], axis=1)
    sin = jnp.concatenate([jnp.sin(ang), jnp.sin(ang), zeros], axis=1)
    return cos, sin


def _mixer(xp, xs, cos, sin, w_in_r, wq_r, wukv_r, wvt_r, wgk_r, q_norm_g, kv_norm_g, b_gk):
    tp, ts = xp.shape[0], xs.shape[0]
    t = tp + ts
    tm = MIX_TM
    assert tp % tm == 0 and ts % tm == 0
    n_p = tp // tm
    row = lambda w: pl.BlockSpec((tm, w), lambda i: (i, 0))
    col = pl.BlockSpec((MLA_WIDTH, tm), lambda i: (0, i))
    full = lambda w, dt: (row(w), jax.ShapeDtypeStruct((t, w), dt))
    ckv_p, ckv_s = _two_stream_specs(tm, KV_RANK, n_p)
    kr_p, kr_s = _two_stream_specs(tm, ROPE_DIM, n_p)
    outs = [full(MLA_HEADS * QK_PAD, BF16), full(MLA_HEADS * QK_PAD, BF16), full(MLA_WIDTH, BF16),
            (col, jax.ShapeDtypeStruct((MLA_WIDTH, t), BF16)),
            (ckv_p, jax.ShapeDtypeStruct((tp, KV_RANK), F32)), (ckv_s, jax.ShapeDtypeStruct((ts, KV_RANK), F32)),
            (kr_p, jax.ShapeDtypeStruct((tp, ROPE_DIM), F32)), (kr_s, jax.ShapeDtypeStruct((ts, ROPE_DIM), F32)),
            full(GLA_KW, F32), full(GLA_KW, F32), full(GLA_WIDTH, F32), full(GLA_WIDTH, F32), full(GLA_KW, F32)]
    return pl.pallas_call(
        functools.partial(_mixer_kernel, n_prompt_tiles=n_p),
        grid=(t // tm,),
        in_specs=[*_two_stream_specs(tm, D_MODEL, n_p), _const_spec(w_in_r.shape), _const_spec(wq_r.shape),
                  _const_spec(wukv_r.shape), _const_spec(wvt_r.shape), _const_spec(wgk_r.shape),
                  _const_spec((1, Q_RANK)), _const_spec((1, KV_RANK)), _const_spec((1, GLA_KW)),
                  row(LANES), row(LANES)],
        out_specs=[o[0] for o in outs],
        out_shape=[o[1] for o in outs],
        compiler_params=pltpu.CompilerParams(dimension_semantics=("arbitrary",),
                                             vmem_limit_bytes=VMEM_LIMIT_BYTES),
        name="mixer",
    )(xp, xs, w_in_r, wq_r, wukv_r, wvt_r, wgk_r, q_norm_g.reshape(1, -1), kv_norm_g.reshape(1, -1),
      b_gk.reshape(1, -1), cos, sin)


MLA_TQ = 512
MLA_LOOKAHEAD = 2
MLA_SIDE_ROWS = 128
NEG_BIG = -1e30
_NT = (((1,), (1,)), ((), ()))


def _mla_prompt_kernel(qi_ref, ki_ref, q_ref, k_ref, vt_ref, u_ref, v_ref, o_ref, u8_ref, uinv_ref, v8_ref, vinv_ref,
                       m_ref, l_ref, acc_ref, *, side_steps):
    step = pl.program_id(0)
    qi = qi_ref[step]
    ki = ki_ref[step]
    tq = q_ref.shape[0]
    tk = k_ref.shape[0]

    @pl.when(step < side_steps)
    def _():
        for src, dst, inv in ((u_ref, u8_ref, uinv_ref), (v_ref, v8_ref, vinv_ref)):
            x = src[...]
            amax = jnp.max(jnp.max(jnp.abs(x), axis=0, keepdims=True), axis=1, keepdims=True)
            amax = jnp.maximum(amax, FP8_TINY)
            dst[...] = _fp8_scaled(x, amax)
            inv[0] = jnp.broadcast_to(amax * (1.0 / FP8_TARGET), inv.shape[1:])

    @pl.when(ki == 0)
    def _():
        m_ref[...] = jnp.full(m_ref.shape, NEG_BIG, F32)
        l_ref[...] = jnp.zeros(l_ref.shape, F32)
        acc_ref[...] = jnp.zeros(acc_ref.shape, F32)

    def update(masked):
        if masked:
            key_chunk = lax.broadcasted_iota(jnp.int32, (tk, tq), 0) // CHUNK
            qry_chunk = lax.broadcasted_iota(jnp.int32, (tk, tq), 1) // CHUNK
            visible = key_chunk <= qry_chunk
        def scores(h):
            return lax.dot_general(k_ref[:, h * QK_PAD:(h + 1) * QK_PAD], q_ref[:, h * QK_PAD:(h + 1) * QK_PAD],
                                   _NT, preferred_element_type=F32)

        ahead = [scores(h) for h in range(MLA_LOOKAHEAD)]
        for h in range(MLA_HEADS):
            s = ahead.pop(0)
            if h + MLA_LOOKAHEAD < MLA_HEADS:
                ahead.append(scores(h + MLA_LOOKAHEAD))
            if masked:
                s = jnp.where(visible, s, NEG_BIG)
            m_old = m_ref[h]
            m_new = jnp.maximum(m_old, jnp.max(s, axis=0, keepdims=True))
            p = jnp.exp2(s - m_new)
            alpha = jnp.exp2(m_old - m_new)
            l_ref[h] = alpha * l_ref[h] + jnp.sum(p, axis=0, keepdims=True)
            acc_ref[h] = alpha * acc_ref[h] + jnp.dot(vt_ref[h * V_DIM:(h + 1) * V_DIM, :], p.astype(BF16),
                                                      preferred_element_type=F32)
            m_ref[h] = m_new

    @pl.when(ki < qi)
    def _():
        update(False)

    @pl.when(ki == qi)
    def _():
        update(True)
        for h in range(MLA_HEADS):
            o_ref[:, h * V_DIM:(h + 1) * V_DIM] = (acc_ref[h] / l_ref[h]).T.astype(BF16)


def _mla_prompt(q, k, vt, n_tokens, table_u, table_v):
    tq = MLA_TQ
    assert n_tokens % tq == 0
    nq = n_tokens // tq
    pairs = [(a, b) for a in range(nq) for b in range(a + 1)]
    qi = jnp.asarray([p[0] for p in pairs], jnp.int32)
    ki = jnp.asarray([p[1] for p in pairs], jnp.int32)
    rows, cols = table_u.shape
    assert table_v.shape == (rows, cols) and rows % MLA_SIDE_ROWS == 0
    side_steps = rows // MLA_SIDE_ROWS
    assert side_steps <= len(pairs)
    side = lambda: pl.BlockSpec((MLA_SIDE_ROWS, cols), lambda s, qi, ki: (jnp.minimum(s, side_steps - 1), 0))
    inv = lambda: pl.BlockSpec((1, SUBLANES, LANES), lambda s, qi, ki: (jnp.minimum(s, side_steps - 1), 0, 0))
    grid_spec = pltpu.PrefetchScalarGridSpec(
        num_scalar_prefetch=2,
        grid=(len(pairs),),
        in_specs=[pl.BlockSpec((tq, MLA_HEADS * QK_PAD), lambda s, qi, ki: (qi[s], 0)),
                  pl.BlockSpec((tq, MLA_HEADS * QK_PAD), lambda s, qi, ki: (ki[s], 0)),
                  pl.BlockSpec((MLA_WIDTH, tq), lambda s, qi, ki: (0, ki[s])), side(), side()],
        out_specs=[pl.BlockSpec((tq, MLA_WIDTH), lambda s, qi, ki: (qi[s], 0)), side(), inv(), side(), inv()],
        scratch_shapes=[pltpu.VMEM((MLA_HEADS, 1, tq), F32), pltpu.VMEM((MLA_HEADS, 1, tq), F32),
                        pltpu.VMEM((MLA_HEADS, V_DIM, tq), F32)],
    )
    table8 = jax.ShapeDtypeStruct((rows, cols), FP8)
    inv_shape = jax.ShapeDtypeStruct((side_steps, SUBLANES, LANES), F32)
    o, u8, u_inv, v8, v_inv = pl.pallas_call(
        functools.partial(_mla_prompt_kernel, side_steps=side_steps),
        grid_spec=grid_spec,
        out_shape=[jax.ShapeDtypeStruct((n_tokens, MLA_WIDTH), BF16), table8, inv_shape, table8, inv_shape],
        compiler_params=pltpu.CompilerParams(dimension_semantics=("arbitrary",),
                                             vmem_limit_bytes=VMEM_LIMIT_BYTES),
        name="mla_prompt",
    )(qi, ki, q, k, vt, table_u, table_v)
    return o, (u8, u_inv), (v8, v_inv)


def _mla_sample_kernel(q_ref, kn_ref, vn_ref, lat_ref, krc_ref, wukv_ref, o_ref):
    kvc = jnp.dot(lat_ref[0].astype(BF16), wukv_ref[...], preferred_element_type=F32)
    krc = krc_ref[0]
    for h in range(MLA_HEADS):
        q = q_ref[:, h * QK_PAD:(h + 1) * QK_PAD]
        knc = kvc[:, h * NOPE_DIM:(h + 1) * NOPE_DIM].astype(BF16)
        vc = kvc[:, MLA_HEADS * NOPE_DIM + h * V_DIM:MLA_HEADS * NOPE_DIM + (h + 1) * V_DIM].astype(BF16)
        s_c = (lax.dot_general(q[:, :NOPE_DIM], knc, _NT, preferred_element_type=F32)
               + lax.dot_general(q[:, NOPE_DIM:], krc, _NT, preferred_element_type=F32))
        s_n = lax.dot_general(q, kn_ref[:, h * QK_PAD:(h + 1) * QK_PAD], _NT, preferred_element_type=F32)
        m = jnp.maximum(jnp.max(s_c, axis=1, keepdims=True), jnp.max(s_n, axis=1, keepdims=True))
        p_c = jnp.exp2(s_c - m)
        p_n = jnp.exp2(s_n - m)
        l = jnp.sum(p_c, axis=1, keepdims=True) + jnp.sum(p_n, axis=1, keepdims=True)
        o = (jnp.dot(p_c.astype(BF16), vc, preferred_element_type=F32)
             + jnp.dot(p_n.astype(BF16), vn_ref[:, h * V_DIM:(h + 1) * V_DIM], preferred_element_type=F32))
        o_ref[:, h * V_DIM:(h + 1) * V_DIM] = (o / l).astype(BF16)


def _mla_sample(q, k, v, cache_lat, cache_kr, wukv_r, row0, n_new):
    bs, past, _ = cache_lat.shape
    assert row0 % n_new == 0
    blk0 = row0 // n_new
    krc = jnp.pad(cache_kr, ((0, 0), (0, 0), (0, LANES - ROPE_DIM))).astype(BF16)
    new = lambda w: pl.BlockSpec((n_new, w), lambda b: (blk0 + b, 0))
    return pl.pallas_call(
        _mla_sample_kernel,
        grid=(bs,),
        in_specs=[new(MLA_HEADS * QK_PAD), new(MLA_HEADS * QK_PAD), new(MLA_WIDTH),
                  pl.BlockSpec((1, past, KV_RANK), lambda b: (b, 0, 0)),
                  pl.BlockSpec((1, past, LANES), lambda b: (b, 0, 0)),
                  _const_spec(wukv_r.shape)],
        out_specs=pl.BlockSpec((n_new, MLA_WIDTH), lambda b: (b, 0)),
        out_shape=jax.ShapeDtypeStruct((bs * n_new, MLA_WIDTH), BF16),
        compiler_params=pltpu.CompilerParams(dimension_semantics=("arbitrary",),
                                             vmem_limit_bytes=VMEM_LIMIT_BYTES),
        name="mla_sample",
    )(q, k, v, cache_lat, krc, wukv_r)


GLA_CHUNKS_PER_STEP = 2
GLA_SUB = 16
_TN = (((0,), (0,)), ((), ()))


def _cumsum_rows(x):
    n = x.shape[0]
    row = lax.broadcasted_iota(jnp.int32, x.shape, 0)
    s = 1
    while s < n:
        x = x + jnp.where(row >= s, pltpu.roll(x, s, axis=0), 0.0)
        s *= 2
    return x


def _gla_kernel(q_ref, k_ref, v_ref, lf_ref, gate_ref, s0_ref, g_ref, o_ref, send_ref, st_ref, *, n_chunks):
    c = pl.program_id(1)
    n_rows = CHUNK
    n_sub = n_rows // GLA_SUB

    @pl.when(c == 0)
    def _():
        for h in range(GLA_HEADS):
            st_ref[h] = s0_ref[0, h].T

    sub_row = lax.broadcasted_iota(jnp.int32, (GLA_SUB, GLA_DK), 0)
    for h, ci in [(h, ci) for h in range(GLA_HEADS) for ci in range(q_ref.shape[0] // CHUNK)]:
        rows = slice(ci * CHUNK, (ci + 1) * CHUNK)
        ks = slice(h * GLA_DK, (h + 1) * GLA_DK)
        vs = slice(h * GLA_DV, (h + 1) * GLA_DV)
        q = q_ref[rows, ks] * (GLA_DK ** -0.5)
        k = k_ref[rows, ks]
        v = v_ref[rows, vs]
        vb = v.astype(BF16)
        b = _cumsum_rows(lf_ref[rows, ks])
        b_last = b[n_rows - 1:n_rows]
        st = st_ref[h]
        o_inter = lax.dot_general((q * jnp.exp(b)).astype(BF16), st.astype(BF16), _NT,
                                  preferred_element_type=F32)
        outs = []
        for i in range(n_sub):
            r0 = i * GLA_SUB
            bi = b[r0:r0 + GLA_SUB]
            qi = q[r0:r0 + GLA_SUB]
            o_i = o_inter[r0:r0 + GLA_SUB]
            if i > 0:
                b_ref = b[r0 - 1:r0]
                qh = (qi * jnp.exp(bi - b_ref)).astype(BF16)
                kh = (k[:r0] * jnp.exp(b_ref - b[:r0])).astype(BF16)
                a_off = lax.dot_general(qh, kh, _NT, preferred_element_type=F32)
                o_i = o_i + jnp.dot(a_off.astype(BF16), vb[:r0], preferred_element_type=F32)
            for j in range(GLA_SUB):
                r = r0 + j
                decay = jnp.exp(jnp.where(sub_row >= j, bi - b[r:r + 1], -jnp.inf))
                a_col = jnp.sum(qi * k[r:r + 1] * decay, axis=1, keepdims=True)
                o_i = o_i + a_col * v[r:r + 1]
            outs.append(o_i)
        o = jnp.concatenate(outs, axis=0)

        kd = (k * jnp.exp(b_last - b)).astype(BF16)
        st_new = st * jnp.exp(b_last) + lax.dot_general(vb, kd, _TN, preferred_element_type=F32)
        st_ref[h] = st_new

        gate = gate_ref[rows, vs]
        on = _rms(o, g_ref[...])
        o_ref[rows, vs] = (on * (gate / (1.0 + jnp.exp(-gate)))).astype(BF16)

    @pl.when(c == n_chunks - 1)
    def _():
        for h in range(GLA_HEADS):
            send_ref[0, h] = st_ref[h].T


def _gla(gq, gk, gv, lf, gate, s0, gla_norm_g, row0, n_seq, n_chunks):
    per_step = GLA_CHUNKS_PER_STEP if n_chunks % GLA_CHUNKS_PER_STEP == 0 else 1
    tr = per_step * CHUNK
    assert row0 % tr == 0
    blk0 = row0 // tr
    n_steps = n_chunks // per_step
    row = lambda w: pl.BlockSpec((tr, w), lambda b, c: (blk0 + b * n_steps + c, 0))
    state = pl.BlockSpec((1, GLA_HEADS, GLA_DK, GLA_DV), lambda b, c: (b, 0, 0, 0))
    return pl.pallas_call(
        functools.partial(_gla_kernel, n_chunks=n_steps),
        grid=(n_seq, n_steps),
        in_specs=[row(GLA_KW), row(GLA_KW), row(GLA_WIDTH), row(GLA_KW), row(GLA_WIDTH), state,
                  pl.BlockSpec((1, GLA_DV), lambda b, c: (0, 0))],
        out_specs=[pl.BlockSpec((tr, GLA_WIDTH), lambda b, c: (b * n_steps + c, 0)), state],
        out_shape=[jax.ShapeDtypeStruct((n_seq * n_chunks * CHUNK, GLA_WIDTH), BF16),
                   jax.ShapeDtypeStruct((n_seq, GLA_HEADS, GLA_DK, GLA_DV), F32)],
        scratch_shapes=[pltpu.VMEM((GLA_HEADS, GLA_DV, GLA_DK), F32)],
        compiler_params=pltpu.CompilerParams(dimension_semantics=("arbitrary", "arbitrary"),
                                             vmem_limit_bytes=VMEM_LIMIT_BYTES),
        name="gla",
    )(gq, gk, gv, lf, gate, s0, gla_norm_g.reshape(1, -1))


OUT_TM = 256


def _layer_norm(y, g, b):
    mu = jnp.mean(y, axis=-1, keepdims=True)
    yc = y - mu
    var = jnp.mean(yc * yc, axis=-1, keepdims=True)
    return yc * lax.rsqrt(var + LN_EPS) * g + b


def _outproj_kernel(mlap_ref, mlas_ref, glap_ref, glas_ref, xp_ref, xs_ref, wo_ref, g_ref, b_ref, wq_ref,
                    h_ref, h8_ref, hinv_ref, qp_ref, *, n_prompt_tiles):
    in_prompt = pl.program_id(0) < n_prompt_tiles
    mla = jnp.where(in_prompt, mlap_ref[...], mlas_ref[...])
    gla = jnp.where(in_prompt, glap_ref[...], glas_ref[...])
    x = jnp.where(in_prompt, xp_ref[...], xs_ref[...])
    mix = (jnp.dot(mla, wo_ref[:MLA_WIDTH], preferred_element_type=F32)
           + jnp.dot(gla, wo_ref[MLA_WIDTH:], preferred_element_type=F32))
    h = _layer_norm(DN_ALPHA * x + mix, g_ref[...], b_ref[...])
    h_ref[...] = h
    amax = jnp.maximum(jnp.max(jnp.abs(h), axis=1, keepdims=True), FP8_TINY)
    h8_ref[...] = _fp8_scaled(h, amax)
    hinv_ref[...] = jnp.broadcast_to(amax * (1.0 / FP8_TARGET), hinv_ref.shape)
    qp = jnp.dot(h.astype(BF16), wq_ref[...], preferred_element_type=F32).astype(BF16)
    per_head = qp_ref.shape[2]
    for head in range(qp_ref.shape[0]):
        qp_ref[head] = qp[:, head * per_head:(head + 1) * per_head]


def _outproj(mla_p, mla_s, gla_p, gla_s, xp, xs, w_o_b, ln1_g, ln1_b, peer_wq_b):
    tp, ts = xp.shape[0], xs.shape[0]
    t = tp + ts
    tm = OUT_TM
    assert tp % tm == 0 and ts % tm == 0
    n_p = tp // tm
    row = lambda w: pl.BlockSpec((tm, w), lambda i: (i, 0))
    return pl.pallas_call(
        functools.partial(_outproj_kernel, n_prompt_tiles=n_p),
        grid=(t // tm,),
        in_specs=[*_two_stream_specs(tm, MLA_WIDTH, n_p), *_two_stream_specs(tm, GLA_WIDTH, n_p),
                  *_two_stream_specs(tm, D_MODEL, n_p), _const_spec(w_o_b.shape),
                  _const_spec((1, D_MODEL)), _const_spec((1, D_MODEL)), _const_spec(peer_wq_b.shape)],
        out_specs=[row(D_MODEL), row(D_MODEL), row(LANES),
                   pl.BlockSpec((PEER_HEADS, tm, 2 * PEER_HALF), lambda i: (0, i, 0))],
        out_shape=[jax.ShapeDtypeStruct((t, D_MODEL), F32), jax.ShapeDtypeStruct((t, D_MODEL), FP8),
                   jax.ShapeDtypeStruct((t, LANES), F32),
                   jax.ShapeDtypeStruct((PEER_HEADS, t, 2 * PEER_HALF), BF16)],
        compiler_params=pltpu.CompilerParams(dimension_semantics=("arbitrary",),
                                             vmem_limit_bytes=VMEM_LIMIT_BYTES),
        name="outproj",
    )(mla_p, mla_s, gla_p, gla_s, xp, xs, w_o_b, ln1_g.reshape(1, -1), ln1_b.reshape(1, -1), peer_wq_b)


PEER_HEADS = 8
N_KEYS = 128
PEER_HALF = 128
PEER_TOPK = 16
ROUTE_TM = 256
CAND_WIDE_RANKS = 8
CAND_ROWS = PEER_TOPK + (CAND_WIDE_RANKS - 1) * SUBLANES + (PEER_TOPK - CAND_WIDE_RANKS)


def _first_max(x, row):
    n = x.shape[0]
    blocks = [(x[r:r + SUBLANES], row[r:r + SUBLANES]) for r in range(0, n, SUBLANES)]
    while len(blocks) > 1:
        merged = []
        for j in range(0, len(blocks) - 1, 2):
            (va, ia), (vb, ib) = blocks[j], blocks[j + 1]
            take = va >= vb
            merged.append((jnp.where(take, va, vb), jnp.where(take, ia, ib)))
        if len(blocks) % 2:
            merged.append(blocks[-1])
        blocks = merged
    v, i = blocks[0]
    m = jnp.max(v, axis=0, keepdims=True)
    pick = jnp.min(jnp.where(v == m, i, float(n)), axis=0, keepdims=True)
    return m, pick


def _extract_top(xs, row, n_out, on_pick):
    xs = list(xs)
    for k in range(n_out):
        for p, x in enumerate(xs):
            m, pick = _first_max(x, row)
            hit = row == pick
            on_pick(p, k, m, hit, pick)
            xs[p] = jnp.where(hit, -jnp.inf, x)


def _route_kernel(qp_ref, keys_ref, row_ref, g_ref, i1_ref, i2_ref, sv_ref, si_ref, gt_ref, et_ref):
    lax.fori_loop(0, PEER_HEADS,
                  functools.partial(_route_head, qp_ref, keys_ref, row_ref, sv_ref, si_ref, gt_ref, et_ref), 0)
    g_ref[...] = gt_ref[...].T
    expert = et_ref[...]
    key0 = jnp.floor(expert * (1.0 / N_KEYS))
    i1_ref[...] = key0.T
    i2_ref[...] = (expert - key0 * N_KEYS).T


def _route_head(qp_ref, keys_ref, row_ref, sv_ref, si_ref, gt_ref, et_ref, h, carry):
    tm = qp_ref.shape[1]
    q = qp_ref[h]
    scores = [lax.dot_general(keys_ref[half], q[:, half * PEER_HALF:(half + 1) * PEER_HALF], _NT,
                              preferred_element_type=F32) for half in range(2)]

    def keep(half, k, m, hit, pick):
        sv_ref[half, k:k + 1, :] = m
        si_ref[half, k:k + 1, :] = pick

    _extract_top(scores, row_ref[...], PEER_TOPK, keep)

    s1, s2 = sv_ref[0], sv_ref[1]
    i1, i2 = si_ref[0], si_ref[1]
    bc = lambda r, n: jnp.broadcast_to(r, (n, tm))
    wide = range(1, CAND_WIDE_RANKS)
    cand = jnp.concatenate([bc(s1[0:1], PEER_TOPK) + s2]
                           + [bc(s1[a:a + 1], SUBLANES) + s2[:SUBLANES] for a in wide]
                           + [s1[CAND_WIDE_RANKS:] + bc(s2[0:1], PEER_TOPK - CAND_WIDE_RANKS)], axis=0)
    i1 = i1 * N_KEYS
    c_expert = jnp.concatenate([bc(i1[0:1], PEER_TOPK) + i2]
                               + [bc(i1[a:a + 1], SUBLANES) + i2[:SUBLANES] for a in wide]
                               + [i1[CAND_WIDE_RANKS:] + bc(i2[0:1], PEER_TOPK - CAND_WIDE_RANKS)], axis=0)
    base = pl.multiple_of(h * PEER_TOPK, PEER_TOPK)

    def keep_pair(_, k, m, hit, pick):
        gt_ref[pl.ds(base + k, 1), :] = m
        et_ref[pl.ds(base + k, 1), :] = jnp.sum(jnp.where(hit, c_expert, 0.0), axis=0, keepdims=True)

    _extract_top([cand], row_ref[:CAND_ROWS], PEER_TOPK, keep_pair)

    sc = gt_ref[pl.ds(base, PEER_TOPK), :]
    e = jnp.exp(sc - sc[0:1])
    gt_ref[pl.ds(base, PEER_TOPK), :] = e / jnp.sum(e, axis=0, keepdims=True)
    return carry


def _route(qp, keys_b):
    t = qp.shape[1]
    tm = ROUTE_TM
    assert t % tm == 0
    n_sel = PEER_HEADS * PEER_TOPK
    out = pl.BlockSpec((tm, n_sel), lambda i: (i, 0))
    row_index = jnp.broadcast_to(jnp.arange(N_KEYS, dtype=F32)[:, None], (N_KEYS, tm))
    return pl.pallas_call(
        _route_kernel,
        grid=(t // tm,),
        in_specs=[pl.BlockSpec((PEER_HEADS, tm, 2 * PEER_HALF), lambda i: (0, i, 0)),
                  pl.BlockSpec(keys_b.shape, lambda i: (0, 0, 0)),
                  pl.BlockSpec((N_KEYS, tm), lambda i: (0, 0))],
        out_specs=[out, out, out],
        out_shape=[jax.ShapeDtypeStruct((t, n_sel), F32)] * 3,
        scratch_shapes=[pltpu.VMEM((2, PEER_TOPK, tm), F32), pltpu.VMEM((2, PEER_TOPK, tm), F32),
                        pltpu.VMEM((n_sel, tm), F32), pltpu.VMEM((n_sel, tm), F32)],
        compiler_params=pltpu.CompilerParams(dimension_semantics=("arbitrary",),
                                             vmem_limit_bytes=VMEM_LIMIT_BYTES),
        name="peer_route",
    )(qp, keys_b, row_index)


SCATTER_TB = 128
SCATTER_UNROLL = 32
W_ROW_TILE = SUBLANES


def _w_table_tokens(t):
    return -(-t // EXP_TM) * EXP_TM


def _scatter_kernel(g_ref, i1_ref, i2_ref, w_ref, *, n_token_blocks):
    tb = g_ref.shape[0]
    key = lax.broadcasted_iota(jnp.int32, (N_KEYS, g_ref.shape[1]), 0).astype(F32)

    @pl.when(pl.program_id(0) >= n_token_blocks)
    def _():
        w_ref[...] = jnp.zeros(w_ref.shape, F32)

    def body(grp, carry):
        t0 = pl.multiple_of(grp * SCATTER_UNROLL, SCATTER_UNROLL)
        g8 = g_ref[pl.ds(t0, SCATTER_UNROLL), :]
        i18 = i1_ref[pl.ds(t0, SCATTER_UNROLL), :]
        i28 = i2_ref[pl.ds(t0, SCATTER_UNROLL), :]
        zero = jnp.zeros(key.shape, BF16)
        for u in range(0, SCATTER_UNROLL, 2):
            a_ts, b_ts = [], []
            for v in (u, u + 1):
                g = jnp.broadcast_to(g8[v:v + 1], key.shape)
                i1 = jnp.broadcast_to(i18[v:v + 1], key.shape)
                i2 = jnp.broadcast_to(i28[v:v + 1], key.shape)
                a_ts.append(jnp.where(key == i1, g, 0.0).astype(BF16))
                b_ts.append(jnp.where(key == i2, 1.0, 0.0).astype(BF16))
            a_pair = jnp.concatenate(a_ts, axis=1)
            b_pair = jnp.concatenate([jnp.concatenate([b_ts[0], zero], axis=1),
                                      jnp.concatenate([zero, b_ts[1]], axis=1)], axis=0)
            w = lax.dot_general(a_pair, b_pair, _NT, preferred_element_type=F32)
            for n, v in enumerate((u, u + 1)):
                w_ref[:, pl.ds(t0 + v, 1), :, :] = w[:, n * N_KEYS:(n + 1) * N_KEYS].reshape(
                    N_KEYS // W_ROW_TILE, 1, W_ROW_TILE, N_KEYS)
        return carry

    @pl.when(pl.program_id(0) < n_token_blocks)
    def _():
        lax.fori_loop(0, tb // SCATTER_UNROLL, body, 0)


def _scatter(g, i1, i2):
    t = g.shape[0]
    tb = SCATTER_TB
    t_pad = _w_table_tokens(t)
    assert t % tb == 0 and t_pad % tb == 0
    n_blocks = t // tb
    sel = pl.BlockSpec((tb, g.shape[1]), lambda i: (jnp.minimum(i, n_blocks - 1), 0))
    n_grp = N_KEYS // W_ROW_TILE
    return pl.pallas_call(
        functools.partial(_scatter_kernel, n_token_blocks=n_blocks),
        grid=(t_pad // tb,),
        in_specs=[sel, sel, sel],
        out_specs=pl.BlockSpec((n_grp, tb, W_ROW_TILE, N_KEYS), lambda i: (0, i, 0, 0)),
        out_shape=jax.ShapeDtypeStruct((n_grp, t_pad, W_ROW_TILE, N_KEYS), F32),
        compiler_params=pltpu.CompilerParams(dimension_semantics=("arbitrary",),
                                             vmem_limit_bytes=VMEM_LIMIT_BYTES),
        name="peer_scatter",
    )(g, i1, i2)


EXP_TM = 1024
EXP_ROWS = W_ROW_TILE
EXP_TE = EXP_ROWS * N_KEYS
EXP_SUBTILES = 4
GELU_C = 2.0 ** -0.5


def _experts_kernel(h8_ref, hinv_ref, w_ref, u_ref, uinv_ref, v_ref, vinv_ref, h_ref, g_ref, b_ref, y_ref):
    j = pl.program_id(1)
    tm = h8_ref.shape[0]

    @pl.when(j == 0)
    def _():
        y_ref[...] = DN_ALPHA * h_ref[...]

    sub = tm // EXP_SUBTILES
    u_inv = jnp.concatenate([uinv_ref[r, 0:1, :] for r in range(EXP_ROWS)], axis=1)
    half_v_inv = 0.5 * jnp.concatenate([vinv_ref[r, 0:1, :] for r in range(EXP_ROWS)], axis=1)
    acts = []
    for s in range(EXP_SUBTILES):
        h_inv = hinv_ref[s * sub:(s + 1) * sub, :]
        a8 = lax.dot_general(h8_ref[s * sub:(s + 1) * sub, :], u_ref[...], _NT, preferred_element_type=F32)
        acts.append(a8 * jnp.concatenate([h_inv] * EXP_ROWS, axis=1) * u_inv)
    for s in range(EXP_SUBTILES):
        w = jnp.concatenate([w_ref[pl.ds(s * sub * W_ROW_TILE + r, sub, stride=W_ROW_TILE), :]
                             for r in range(EXP_ROWS)], axis=1)
        a = acts[s]
        p = w * (a * (1.0 + lax.erf(a * GELU_C))) * half_v_inv
        pmax = jnp.maximum(jnp.max(jnp.abs(p), axis=1, keepdims=True), FP8_TINY)
        out = jnp.dot(_fp8_scaled(p, pmax), v_ref[...], preferred_element_type=F32)
        y_ref[s * sub:(s + 1) * sub, :] += out * (pmax * (1.0 / FP8_TARGET))

    @pl.when(j == pl.num_programs(1) - 1)
    def _():
        y_ref[...] = _layer_norm(y_ref[...], g_ref[...], b_ref[...])


def _experts(h8, h_inv, h, w_table, u8, u_inv, v8, v_inv, ln2_g, ln2_b, row0, n_rows):
    tm = min(EXP_TM, n_rows)
    assert n_rows % tm == 0 and row0 % tm == 0 and tm % (EXP_SUBTILES * FP8_ROW_TILE) == 0
    nt = n_rows // tm
    blk0 = row0 // tm
    ne = u8.shape[0] // EXP_TE
    tiles_per_group = w_table.shape[1] // tm
    w2d = w_table.reshape(-1, N_KEYS)
    once = lambda w: pl.BlockSpec((tm, w), lambda i, j: (blk0 + i, 0), pipeline_mode=pl.Buffered(1))
    return pl.pallas_call(
        _experts_kernel,
        grid=(nt, ne),
        in_specs=[once(D_MODEL), once(LANES),
                  pl.BlockSpec((tm * W_ROW_TILE, N_KEYS), lambda i, j: (j * tiles_per_group + blk0 + i, 0)),
                  pl.BlockSpec((EXP_TE, D_MODEL), lambda i, j: (j, 0)),
                  pl.BlockSpec((EXP_ROWS, SUBLANES, LANES), lambda i, j: (j, 0, 0)),
                  pl.BlockSpec((EXP_TE, D_MODEL), lambda i, j: (j, 0)),
                  pl.BlockSpec((EXP_ROWS, SUBLANES, LANES), lambda i, j: (j, 0, 0)),
                  once(D_MODEL), pl.BlockSpec((1, D_MODEL), lambda i, j: (0, 0)),
                  pl.BlockSpec((1, D_MODEL), lambda i, j: (0, 0))],
        out_specs=pl.BlockSpec((tm, D_MODEL), lambda i, j: (i, 0)),
        out_shape=jax.ShapeDtypeStruct((n_rows, D_MODEL), F32),
        compiler_params=pltpu.CompilerParams(dimension_semantics=("arbitrary", "arbitrary"),
                                             vmem_limit_bytes=VMEM_LIMIT_BYTES),
        name="peer_experts",
    )(h8, h_inv, w2d, u8, u_inv, v8, v_inv, h, ln2_g.reshape(1, -1), ln2_b.reshape(1, -1))


def kernel(x_prompt, x_sample, cache_kv_latent, cache_k_rope, state_gla, w_in, q_norm_g, w_uq, kv_norm_g, w_ukv,
           w_gk2, b_gk, gla_norm_g, w_o, ln1_g, ln1_b, peer_wq, peer_keys, peer_u, peer_v, ln2_g, ln2_b):
    bp, sp, _ = x_prompt.shape
    bs, ss, _ = x_sample.shape
    past = cache_kv_latent.shape[1]
    tp, ts = bp * sp, bs * ss
    xp = x_prompt.reshape(tp, D_MODEL)
    xs = x_sample.reshape(ts, D_MODEL)
    cos_p, sin_p = _rope_tables(jnp.arange(sp, dtype=jnp.int32))
    cos_s, sin_s = _rope_tables(past + jnp.arange(ss, dtype=jnp.int32))
    cos = jnp.concatenate([jnp.tile(cos_p, (bp, 1)), jnp.tile(cos_s, (bs, 1))])
    sin = jnp.concatenate([jnp.tile(sin_p, (bp, 1)), jnp.tile(sin_s, (bs, 1))])
    w_in_r, wq_r, wukv_r, wvt_r, wgk_r = _mixer_weights(w_in, w_uq, w_ukv, w_gk2)
    q, k, v, vt, ckv_p, ckv_s, kr_p, kr_s, gq, gk, gv, gate, lf = _mixer(
        xp, xs, cos, sin, w_in_r, wq_r, wukv_r, wvt_r, wgk_r, q_norm_g, kv_norm_g, b_gk)
    o_p, (u8, u_inv), (v8, v_inv) = _mla_prompt(q, k, vt, tp, peer_u, peer_v)
    o_s = _mla_sample(q, k, v, cache_kv_latent, cache_k_rope, wukv_r, tp, ss)
    g_p, st_p = _gla(gq, gk, gv, lf, gate, jnp.zeros((bp,) + state_gla.shape[1:], F32), gla_norm_g,
                     0, bp, sp // CHUNK)
    g_s, st_s = _gla(gq, gk, gv, lf, gate, state_gla, gla_norm_g, tp, bs, ss // CHUNK)

    h, h8, h_inv, qp = _outproj(o_p, o_s, g_p, g_s, xp, xs, w_o.astype(BF16), ln1_g, ln1_b, peer_wq.astype(BF16))
    gates, i1, i2 = _route(qp, peer_keys.astype(BF16))
    w_table = _scatter(gates, i1, i2)
    y_p = _experts(h8, h_inv, h, w_table, u8, u_inv, v8, v_inv, ln2_g, ln2_b, 0, tp)
    y_s = _experts(h8, h_inv, h, w_table, u8, u_inv, v8, v_inv, ln2_g, ln2_b, tp, ts)

    dt = x_prompt.dtype
    return (y_p.reshape(bp, sp, D_MODEL), y_s.reshape(bs, ss, D_MODEL),
            ckv_p.reshape(bp, sp, KV_RANK), kr_p.reshape(bp, sp, ROPE_DIM), st_p.astype(dt),
            ckv_s.reshape(bs, ss, KV_RANK), kr_s.reshape(bs, ss, ROPE_DIM), st_s.astype(dt))
```

```python
import functools
import math

import jax
import jax.numpy as jnp
from jax import lax
from jax.experimental import pallas as pl
from jax.experimental.pallas import tpu as pltpu

F32 = jnp.float32
BF16 = jnp.bfloat16
FP8 = jnp.float8_e4m3fn
FP8_ROW_TILE = 32
FP8_TARGET = 128.0
FP8_TINY = 1e-30


def _fp8_scaled(x, amax):
    return (x * (FP8_TARGET / amax)).astype(FP8)

LANES = 128
SUBLANES = 8
VMEM_LIMIT_BYTES = 60 * 1024 * 1024

D_MODEL = 2048
CHUNK = 64
MLA_HEADS = 8
Q_RANK = 512
KV_RANK = 256
NOPE_DIM = 128
ROPE_DIM = 64
V_DIM = 128
ROPE_THETA = 10000.0
MLA_SCALE = (NOPE_DIM + ROPE_DIM) ** -0.5
Q_SCALE = MLA_SCALE * math.log2(math.e)
QK_PAD = 256
GLA_HEADS = 4
GLA_DK = 128
GLA_DV = 256
GATE_RANK = 16
GATE_NORMALIZER = 16.0
GLA_KW = GLA_HEADS * GLA_DK
GLA_WIDTH = GLA_HEADS * GLA_DV
MLA_WIDTH = MLA_HEADS * V_DIM
DN_ALPHA = 2.0 ** 0.25
LN_EPS = 1e-5
RMS_EPS = 1e-6

Z_CQ = 0
Z_CKV = Z_CQ + Q_RANK
Z_KR = Z_CKV + KV_RANK
Z_KROT = Z_KR + LANES
Z_GQ = Z_KROT + LANES
Z_GK = Z_GQ + GLA_KW
Z_GV = Z_GK + GLA_KW
Z_GATE = Z_GV + GLA_WIDTH
Z_WIDTH = Z_GATE + GLA_WIDTH


def _const_spec(shape):
    nd = len(shape)
    return pl.BlockSpec(shape, lambda *_: (0,) * nd, pipeline_mode=pl.Buffered(1))


def _rms(x, g):
    return x * lax.rsqrt(jnp.mean(x * x, axis=-1, keepdims=True) + RMS_EPS) * g


MIX_TM = 256


def _two_stream_specs(tm, width, n_first):
    return (pl.BlockSpec((tm, width), lambda i: (jnp.minimum(i, n_first - 1), 0)),
            pl.BlockSpec((tm, width), lambda i: (jnp.maximum(i - n_first, 0), 0)))


def _mixer_kernel(xp_ref, xs_ref, w_in_ref, wq_ref, wukv_ref, wvt_ref, wgk_ref, qg_ref, kvg_ref, bgk_ref,
                  cos_ref, sin_ref, q_ref, k_ref, v_ref, vt_ref, ckvp_ref, ckvs_ref, krp_ref, krs_ref,
                  gq_ref, gk_ref, gv_ref, gate_ref, lf_ref, *, n_prompt_tiles):
    in_prompt = pl.program_id(0) < n_prompt_tiles
    xb = jnp.where(in_prompt, xp_ref[...], xs_ref[...]).astype(BF16)
    cos = cos_ref[...]
    sin = sin_ref[...]

    z_lat = jnp.dot(xb, w_in_ref[:, Z_CQ:Z_GQ], preferred_element_type=F32)
    cq = _rms(z_lat[:, Z_CQ:Z_CKV], qg_ref[...])
    c_kv = _rms(z_lat[:, Z_CKV:Z_KR], kvg_ref[...])
    krga = z_lat[:, Z_KR:Z_KROT]
    k_rope = krga * cos + z_lat[:, Z_KROT:Z_GQ] * sin

    @pl.when(in_prompt)
    def _():
        ckvp_ref[...] = c_kv
        krp_ref[...] = k_rope[:, :ROPE_DIM]

    @pl.when(jnp.logical_not(in_prompt))
    def _():
        ckvs_ref[...] = c_kv
        krs_ref[...] = k_rope[:, :ROPE_DIM]

    k_rope_b = k_rope.astype(BF16)

    q = jnp.dot(cq.astype(BF16), wq_ref[...], preferred_element_type=F32)
    kv = jnp.dot(c_kv.astype(BF16), wukv_ref[...], preferred_element_type=F32)
    for h in range(MLA_HEADS):
        lo = h * QK_PAD
        q_ref[:, lo:lo + NOPE_DIM] = (q[:, lo:lo + NOPE_DIM] * Q_SCALE).astype(BF16)
        rot = q[:, MLA_HEADS * QK_PAD + h * LANES:MLA_HEADS * QK_PAD + (h + 1) * LANES]
        q_ref[:, lo + NOPE_DIM:lo + QK_PAD] = (
            (q[:, lo + NOPE_DIM:lo + QK_PAD] * cos + rot * sin) * Q_SCALE).astype(BF16)
        k_ref[:, lo:lo + NOPE_DIM] = kv[:, h * NOPE_DIM:(h + 1) * NOPE_DIM].astype(BF16)
        k_ref[:, lo + NOPE_DIM:lo + QK_PAD] = k_rope_b
    v_ref[...] = kv[:, MLA_HEADS * NOPE_DIM:].astype(BF16)
    vt_ref[...] = jnp.dot(wvt_ref[...], c_kv.T.astype(BF16), preferred_element_type=F32).astype(BF16)

    pre = jnp.dot(krga.astype(BF16), wgk_ref[...], preferred_element_type=F32) + bgk_ref[...]
    lf_ref[...] = (jnp.minimum(pre, 0.0) - jnp.log1p(jnp.exp(-jnp.abs(pre)))) * (1.0 / GATE_NORMALIZER)

    gq_ref[...] = jnp.dot(xb, w_in_ref[:, Z_GQ:Z_GK], preferred_element_type=F32)
    gk_ref[...] = jnp.dot(xb, w_in_ref[:, Z_GK:Z_GV], preferred_element_type=F32)
    gv_ref[...] = jnp.dot(xb, w_in_ref[:, Z_GV:Z_GATE], preferred_element_type=F32)
    gate_ref[...] = jnp.dot(xb, w_in_ref[:, Z_GATE:Z_WIDTH], preferred_element_type=F32)


def _rotate_half_cols(w):
    half = ROPE_DIM // 2
    return jnp.concatenate([-w[..., half:], w[..., :half]], axis=-1)


def _mixer_weights(w_in, w_uq, w_ukv, w_gk2):
    pts = []
    acc = 0
    for s in (Q_RANK, KV_RANK, ROPE_DIM, GLA_KW, GLA_KW, GLA_WIDTH, GLA_WIDTH):
        acc += s
        pts.append(acc)
    cq, ckv, kr, gq, gk, gv, gate, ga = jnp.split(w_in, pts, axis=1)
    d = w_in.shape[0]
    w_in_r = jnp.concatenate(
        [cq, ckv, kr, ga, jnp.zeros((d, LANES - ROPE_DIM - GATE_RANK), F32),
         _rotate_half_cols(kr), jnp.zeros((d, LANES - ROPE_DIM), F32), gq, gk, gv, gate], axis=1).astype(BF16)

    wq = w_uq.reshape(Q_RANK, MLA_HEADS, NOPE_DIM + ROPE_DIM)
    wq_rope = wq[..., NOPE_DIM:]
    pad = jnp.zeros((Q_RANK, MLA_HEADS, QK_PAD - NOPE_DIM - ROPE_DIM), F32)
    wq_main = jnp.concatenate([wq, pad], axis=-1).reshape(Q_RANK, MLA_HEADS * QK_PAD)
    wq_rot = jnp.concatenate([_rotate_half_cols(wq_rope), pad], axis=-1).reshape(Q_RANK, MLA_HEADS * LANES)
    wq_r = jnp.concatenate([wq_main, wq_rot], axis=1).astype(BF16)

    wkv = w_ukv.reshape(KV_RANK, MLA_HEADS, NOPE_DIM + V_DIM)
    wukv_r = jnp.concatenate([wkv[..., :NOPE_DIM].reshape(KV_RANK, -1),
                              wkv[..., NOPE_DIM:].reshape(KV_RANK, -1)], axis=1).astype(BF16)

    wvt_r = wkv[..., NOPE_DIM:].reshape(KV_RANK, -1).T.astype(BF16)

    wgk_r = jnp.zeros((LANES, GLA_KW), F32).at[ROPE_DIM:ROPE_DIM + GATE_RANK].set(w_gk2).astype(BF16)
    return w_in_r, wq_r, wukv_r, wvt_r, wgk_r


def _rope_tables(pos):
    half = ROPE_DIM // 2
    freqs = ROPE_THETA ** (-jnp.arange(half, dtype=F32) / half)
    ang = pos.astype(F32)[:, None] * freqs[None, :]
    zeros = jnp.zeros((pos.shape[0], LANES - ROPE_DIM), F32)
    cos = jnp.concatenate([jnp.cos(ang), jnp.cos(ang), zeros], axis=1)
    sin = jnp.concatenate([jnp.sin(ang), jnp.sin(ang), zeros], axis=1)
    return cos, sin


def _mixer(xp, xs, cos, sin, w_in_r, wq_r, wukv_r, wvt_r, wgk_r, q_norm_g, kv_norm_g, b_gk):
    tp, ts = xp.shape[0], xs.shape[0]
    t = tp + ts
    tm = MIX_TM
    assert tp % tm == 0 and ts % tm == 0
    n_p = tp // tm
    row = lambda w: pl.BlockSpec((tm, w), lambda i: (i, 0))
    col = pl.BlockSpec((MLA_WIDTH, tm), lambda i: (0, i))
    full = lambda w, dt: (row(w), jax.ShapeDtypeStruct((t, w), dt))
    ckv_p, ckv_s = _two_stream_specs(tm, KV_RANK, n_p)
    kr_p, kr_s = _two_stream_specs(tm, ROPE_DIM, n_p)
    outs = [full(MLA_HEADS * QK_PAD, BF16), full(MLA_HEADS * QK_PAD, BF16), full(MLA_WIDTH, BF16),
            (col, jax.ShapeDtypeStruct((MLA_WIDTH, t), BF16)),
            (ckv_p, jax.ShapeDtypeStruct((tp, KV_RANK), F32)), (ckv_s, jax.ShapeDtypeStruct((ts, KV_RANK), F32)),
            (kr_p, jax.ShapeDtypeStruct((tp, ROPE_DIM), F32)), (kr_s, jax.ShapeDtypeStruct((ts, ROPE_DIM), F32)),
            full(GLA_KW, F32), full(GLA_KW, F32), full(GLA_WIDTH, F32), full(GLA_WIDTH, F32), full(GLA_KW, F32)]
    return pl.pallas_call(
        functools.partial(_mixer_kernel, n_prompt_tiles=n_p),
        grid=(t // tm,),
        in_specs=[*_two_stream_specs(tm, D_MODEL, n_p), _const_spec(w_in_r.shape), _const_spec(wq_r.shape),
                  _const_spec(wukv_r.shape), _const_spec(wvt_r.shape), _const_spec(wgk_r.shape),
                  _const_spec((1, Q_RANK)), _const_spec((1, KV_RANK)), _const_spec((1, GLA_KW)),
                  row(LANES), row(LANES)],
        out_specs=[o[0] for o in outs],
        out_shape=[o[1] for o in outs],
        compiler_params=pltpu.CompilerParams(dimension_semantics=("arbitrary",),
                                             vmem_limit_bytes=VMEM_LIMIT_BYTES),
        name="mixer",
    )(xp, xs, w_in_r, wq_r, wukv_r, wvt_r, wgk_r, q_norm_g.reshape(1, -1), kv_norm_g.reshape(1, -1),
      b_gk.reshape(1, -1), cos, sin)


MLA_TQ = 512
MLA_LOOKAHEAD = 2
MLA_SIDE_ROWS = 128
NEG_BIG = -1e30
_NT = (((1,), (1,)), ((), ()))


def _mla_prompt_kernel(qi_ref, ki_ref, q_ref, k_ref, vt_ref, u_ref, v_ref, o_ref, u8_ref, uinv_ref, v8_ref, vinv_ref,
                       m_ref, l_ref, acc_ref, *, side_steps):
    step = pl.program_id(0)
    qi = qi_ref[step]
    ki = ki_ref[step]
    tq = q_ref.shape[0]
    tk = k_ref.shape[0]

    @pl.when(step < side_steps)
    def _():
        for src, dst, inv in ((u_ref, u8_ref, uinv_ref), (v_ref, v8_ref, vinv_ref)):
            x = src[...]
            amax = jnp.max(jnp.max(jnp.abs(x), axis=0, keepdims=True), axis=1, keepdims=True)
            amax = jnp.maximum(amax, FP8_TINY)
            dst[...] = _fp8_scaled(x, amax)
            inv[0] = jnp.broadcast_to(amax * (1.0 / FP8_TARGET), inv.shape[1:])

    @pl.when(ki == 0)
    def _():
        m_ref[...] = jnp.full(m_ref.shape, NEG_BIG, F32)
        l_ref[...] = jnp.zeros(l_ref.shape, F32)
        acc_ref[...] = jnp.zeros(acc_ref.shape, F32)

    def update(masked):
        if masked:
            key_chunk = lax.broadcasted_iota(jnp.int32, (tk, tq), 0) // CHUNK
            qry_chunk = lax.broadcasted_iota(jnp.int32, (tk, tq), 1) // CHUNK
            visible = key_chunk <= qry_chunk
        def scores(h):
            return lax.dot_general(k_ref[:, h * QK_PAD:(h + 1) * QK_PAD], q_ref[:, h * QK_PAD:(h + 1) * QK_PAD],
                                   _NT, preferred_element_type=F32)

        ahead = [scores(h) for h in range(MLA_LOOKAHEAD)]
        for h in range(MLA_HEADS):
            s = ahead.pop(0)
            if h + MLA_LOOKAHEAD < MLA_HEADS:
                ahead.append(scores(h + MLA_LOOKAHEAD))
            if masked:
                s = jnp.where(visible, s, NEG_BIG)
            m_old = m_ref[h]
            m_new = jnp.maximum(m_old, jnp.max(s, axis=0, keepdims=True))
            p = jnp.exp2(s - m_new)
            alpha = jnp.exp2(m_old - m_new)
            l_ref[h] = alpha * l_ref[h] + jnp.sum(p, axis=0, keepdims=True)
            acc_ref[h] = alpha * acc_ref[h] + jnp.dot(vt_ref[h * V_DIM:(h + 1) * V_DIM, :], p.astype(BF16),
                                                      preferred_element_type=F32)
            m_ref[h] = m_new

    @pl.when(ki < qi)
    def _():
        update(False)

    @pl.when(ki == qi)
    def _():
        update(True)
        for h in range(MLA_HEADS):
            o_ref[:, h * V_DIM:(h + 1) * V_DIM] = (acc_ref[h] / l_ref[h]).T.astype(BF16)


def _mla_prompt(q, k, vt, n_tokens, table_u, table_v):
    tq = MLA_TQ
    assert n_tokens % tq == 0
    nq = n_tokens // tq
    pairs = [(a, b) for a in range(nq) for b in range(a + 1)]
    qi = jnp.asarray([p[0] for p in pairs], jnp.int32)
    ki = jnp.asarray([p[1] for p in pairs], jnp.int32)
    rows, cols = table_u.shape
    assert table_v.shape == (rows, cols) and rows % MLA_SIDE_ROWS == 0
    side_steps = rows // MLA_SIDE_ROWS
    assert side_steps <= len(pairs)
    side = lambda: pl.BlockSpec((MLA_SIDE_ROWS, cols), lambda s, qi, ki: (jnp.minimum(s, side_steps - 1), 0))
    inv = lambda: pl.BlockSpec((1, SUBLANES, LANES), lambda s, qi, ki: (jnp.minimum(s, side_steps - 1), 0, 0))
    grid_spec = pltpu.PrefetchScalarGridSpec(
        num_scalar_prefetch=2,
        grid=(len(pairs),),
        in_specs=[pl.BlockSpec((tq, MLA_HEADS * QK_PAD), lambda s, qi, ki: (qi[s], 0)),
                  pl.BlockSpec((tq, MLA_HEADS * QK_PAD), lambda s, qi, ki: (ki[s], 0)),
                  pl.BlockSpec((MLA_WIDTH, tq), lambda s, qi, ki: (0, ki[s])), side(), side()],
        out_specs=[pl.BlockSpec((tq, MLA_WIDTH), lambda s, qi, ki: (qi[s], 0)), side(), inv(), side(), inv()],
        scratch_shapes=[pltpu.VMEM((MLA_HEADS, 1, tq), F32), pltpu.VMEM((MLA_HEADS, 1, tq), F32),
                        pltpu.VMEM((MLA_HEADS, V_DIM, tq), F32)],
    )
    table8 = jax.ShapeDtypeStruct((rows, cols), FP8)
    inv_shape = jax.ShapeDtypeStruct((side_steps, SUBLANES, LANES), F32)
    o, u8, u_inv, v8, v_inv = pl.pallas_call(
        functools.partial(_mla_prompt_kernel, side_steps=side_steps),
        grid_spec=grid_spec,
        out_shape=[jax.ShapeDtypeStruct((n_tokens, MLA_WIDTH), BF16), table8, inv_shape, table8, inv_shape],
        compiler_params=pltpu.CompilerParams(dimension_semantics=("arbitrary",),
                                             vmem_limit_bytes=VMEM_LIMIT_BYTES),
        name="mla_prompt",
    )(qi, ki, q, k, vt, table_u, table_v)
    return o, (u8, u_inv), (v8, v_inv)


def _mla_sample_kernel(q_ref, kn_ref, vn_ref, lat_ref, krc_ref, wukv_ref, o_ref):
    kvc = jnp.dot(lat_ref[0].astype(BF16), wukv_ref[...], preferred_element_type=F32)
    krc = krc_ref[0]
    for h in range(MLA_HEADS):
        q = q_ref[:, h * QK_PAD:(h + 1) * QK_PAD]
        knc = kvc[:, h * NOPE_DIM:(h + 1) * NOPE_DIM].astype(BF16)
        vc = kvc[:, MLA_HEADS * NOPE_DIM + h * V_DIM:MLA_HEADS * NOPE_DIM + (h + 1) * V_DIM].astype(BF16)
        s_c = (lax.dot_general(q[:, :NOPE_DIM], knc, _NT, preferred_element_type=F32)
               + lax.dot_general(q[:, NOPE_DIM:], krc, _NT, preferred_element_type=F32))
        s_n = lax.dot_general(q, kn_ref[:, h * QK_PAD:(h + 1) * QK_PAD], _NT, preferred_element_type=F32)
        m = jnp.maximum(jnp.max(s_c, axis=1, keepdims=True), jnp.max(s_n, axis=1, keepdims=True))
        p_c = jnp.exp2(s_c - m)
        p_n = jnp.exp2(s_n - m)
        l = jnp.sum(p_c, axis=1, keepdims=True) + jnp.sum(p_n, axis=1, keepdims=True)
        o = (jnp.dot(p_c.astype(BF16), vc, preferred_element_type=F32)
             + jnp.dot(p_n.astype(BF16), vn_ref[:, h * V_DIM:(h + 1) * V_DIM], preferred_element_type=F32))
        o_ref[:, h * V_DIM:(h + 1) * V_DIM] = (o / l).astype(BF16)


def _mla_sample(q, k, v, cache_lat, cache_kr, wukv_r, row0, n_new):
    bs, past, _ = cache_lat.shape
    assert row0 % n_new == 0
    blk0 = row0 // n_new
    krc = jnp.pad(cache_kr, ((0, 0), (0, 0), (0, LANES - ROPE_DIM))).astype(BF16)
    new = lambda w: pl.BlockSpec((n_new, w), lambda b: (blk0 + b, 0))
    return pl.pallas_call(
        _mla_sample_kernel,
        grid=(bs,),
        in_specs=[new(MLA_HEADS * QK_PAD), new(MLA_HEADS * QK_PAD), new(MLA_WIDTH),
                  pl.BlockSpec((1, past, KV_RANK), lambda b: (b, 0, 0)),
                  pl.BlockSpec((1, past, LANES), lambda b: (b, 0, 0)),
                  _const_spec(wukv_r.shape)],
        out_specs=pl.BlockSpec((n_new, MLA_WIDTH), lambda b: (b, 0)),
        out_shape=jax.ShapeDtypeStruct((bs * n_new, MLA_WIDTH), BF16),
        compiler_params=pltpu.CompilerParams(dimension_semantics=("arbitrary",),
                                             vmem_limit_bytes=VMEM_LIMIT_BYTES),
        name="mla_sample",
    )(q, k, v, cache_lat, krc, wukv_r)


GLA_CHUNKS_PER_STEP = 2
GLA_SUB = 16
_TN = (((0,), (0,)), ((), ()))


def _cumsum_rows(x):
    n = x.shape[0]
    row = lax.broadcasted_iota(jnp.int32, x.shape, 0)
    s = 1
    while s < n:
        x = x + jnp.where(row >= s, pltpu.roll(x, s, axis=0), 0.0)
        s *= 2
    return x


def _gla_kernel(q_ref, k_ref, v_ref, lf_ref, gate_ref, s0_ref, g_ref, o_ref, send_ref, st_ref, *, n_chunks):
    c = pl.program_id(1)
    n_rows = CHUNK
    n_sub = n_rows // GLA_SUB

    @pl.when(c == 0)
    def _():
        for h in range(GLA_HEADS):
            st_ref[h] = s0_ref[0, h].T

    sub_row = lax.broadcasted_iota(jnp.int32, (GLA_SUB, GLA_DK), 0)
    for h, ci in [(h, ci) for h in range(GLA_HEADS) for ci in range(q_ref.shape[0] // CHUNK)]:
        rows = slice(ci * CHUNK, (ci + 1) * CHUNK)
        ks = slice(h * GLA_DK, (h + 1) * GLA_DK)
        vs = slice(h * GLA_DV, (h + 1) * GLA_DV)
        q = q_ref[rows, ks] * (GLA_DK ** -0.5)
        k = k_ref[rows, ks]
        v = v_ref[rows, vs]
        vb = v.astype(BF16)
        b = _cumsum_rows(lf_ref[rows, ks])
        b_last = b[n_rows - 1:n_rows]
        st = st_ref[h]
        o_inter = lax.dot_general((q * jnp.exp(b)).astype(BF16), st.astype(BF16), _NT,
                                  preferred_element_type=F32)
        outs = []
        for i in range(n_sub):
            r0 = i * GLA_SUB
            bi = b[r0:r0 + GLA_SUB]
            qi = q[r0:r0 + GLA_SUB]
            o_i = o_inter[r0:r0 + GLA_SUB]
            if i > 0:
                b_ref = b[r0 - 1:r0]
                qh = (qi * jnp.exp(bi - b_ref)).astype(BF16)
                kh = (k[:r0] * jnp.exp(b_ref - b[:r0])).astype(BF16)
                a_off = lax.dot_general(qh, kh, _NT, preferred_element_type=F32)
                o_i = o_i + jnp.dot(a_off.astype(BF16), vb[:r0], preferred_element_type=F32)
            for j in range(GLA_SUB):
                r = r0 + j
                decay = jnp.exp(jnp.where(sub_row >= j, bi - b[r:r + 1], -jnp.inf))
                a_col = jnp.sum(qi * k[r:r + 1] * decay, axis=1, keepdims=True)
                o_i = o_i + a_col * v[r:r + 1]
            outs.append(o_i)
        o = jnp.concatenate(outs, axis=0)

        kd = (k * jnp.exp(b_last - b)).astype(BF16)
        st_new = st * jnp.exp(b_last) + lax.dot_general(vb, kd, _TN, preferred_element_type=F32)
        st_ref[h] = st_new

        gate = gate_ref[rows, vs]
        on = _rms(o, g_ref[...])
        o_ref[rows, vs] = (on * (gate / (1.0 + jnp.exp(-gate)))).astype(BF16)

    @pl.when(c == n_chunks - 1)
    def _():
        for h in range(GLA_HEADS):
            send_ref[0, h] = st_ref[h].T


def _gla(gq, gk, gv, lf, gate, s0, gla_norm_g, row0, n_seq, n_chunks):
    per_step = GLA_CHUNKS_PER_STEP if n_chunks % GLA_CHUNKS_PER_STEP == 0 else 1
    tr = per_step * CHUNK
    assert row0 % tr == 0
    blk0 = row0 // tr
    n_steps = n_chunks // per_step
    row = lambda w: pl.BlockSpec((tr, w), lambda b, c: (blk0 + b * n_steps + c, 0))
    state = pl.BlockSpec((1, GLA_HEADS, GLA_DK, GLA_DV), lambda b, c: (b, 0, 0, 0))
    return pl.pallas_call(
        functools.partial(_gla_kernel, n_chunks=n_steps),
        grid=(n_seq, n_steps),
        in_specs=[row(GLA_KW), row(GLA_KW), row(GLA_WIDTH), row(GLA_KW), row(GLA_WIDTH), state,
                  pl.BlockSpec((1, GLA_DV), lambda b, c: (0, 0))],
        out_specs=[pl.BlockSpec((tr, GLA_WIDTH), lambda b, c: (b * n_steps + c, 0)), state],
        out_shape=[jax.ShapeDtypeStruct((n_seq * n_chunks * CHUNK, GLA_WIDTH), BF16),
                   jax.ShapeDtypeStruct((n_seq, GLA_HEADS, GLA_DK, GLA_DV), F32)],
        scratch_shapes=[pltpu.VMEM((GLA_HEADS, GLA_DV, GLA_DK), F32)],
        compiler_params=pltpu.CompilerParams(dimension_semantics=("arbitrary", "arbitrary"),
                                             vmem_limit_bytes=VMEM_LIMIT_BYTES),
        name="gla",
    )(gq, gk, gv, lf, gate, s0, gla_norm_g.reshape(1, -1))


OUT_TM = 256


def _layer_norm(y, g, b):
    mu = jnp.mean(y, axis=-1, keepdims=True)
    yc = y - mu
    var = jnp.mean(yc * yc, axis=-1, keepdims=True)
    return yc * lax.rsqrt(var + LN_EPS) * g + b


def _outproj_kernel(mlap_ref, mlas_ref, glap_ref, glas_ref, xp_ref, xs_ref, wo_ref, g_ref, b_ref, wq_ref,
                    h_ref, h8_ref, hinv_ref, qp_ref, *, n_prompt_tiles):
    in_prompt = pl.program_id(0) < n_prompt_tiles
    mla = jnp.where(in_prompt, mlap_ref[...], mlas_ref[...])
    gla = jnp.where(in_prompt, glap_ref[...], glas_ref[...])
    x = jnp.where(in_prompt, xp_ref[...], xs_ref[...])
    mix = (jnp.dot(mla, wo_ref[:MLA_WIDTH], preferred_element_type=F32)
           + jnp.dot(gla, wo_ref[MLA_WIDTH:], preferred_element_type=F32))
    h = _layer_norm(DN_ALPHA * x + mix, g_ref[...], b_ref[...])
    h_ref[...] = h
    amax = jnp.maximum(jnp.max(jnp.abs(h), axis=1, keepdims=True), FP8_TINY)
    h8_ref[...] = _fp8_scaled(h, amax)
    hinv_ref[...] = jnp.broadcast_to(amax * (1.0 / FP8_TARGET), hinv_ref.shape)
    qp = jnp.dot(h.astype(BF16), wq_ref[...], preferred_element_type=F32).astype(BF16)
    per_head = qp_ref.shape[2]
    for head in range(qp_ref.shape[0]):
        qp_ref[head] = qp[:, head * per_head:(head + 1) * per_head]


def _outproj(mla_p, mla_s, gla_p, gla_s, xp, xs, w_o_b, ln1_g, ln1_b, peer_wq_b):
    tp, ts = xp.shape[0], xs.shape[0]
    t = tp + ts
    tm = OUT_TM
    assert tp % tm == 0 and ts % tm == 0
    n_p = tp // tm
    row = lambda w: pl.BlockSpec((tm, w), lambda i: (i, 0))
    return pl.pallas_call(
        functools.partial(_outproj_kernel, n_prompt_tiles=n_p),
        grid=(t // tm,),
        in_specs=[*_two_stream_specs(tm, MLA_WIDTH, n_p), *_two_stream_specs(tm, GLA_WIDTH, n_p),
                  *_two_stream_specs(tm, D_MODEL, n_p), _const_spec(w_o_b.shape),
                  _const_spec((1, D_MODEL)), _const_spec((1, D_MODEL)), _const_spec(peer_wq_b.shape)],
        out_specs=[row(D_MODEL), row(D_MODEL), row(LANES),
                   pl.BlockSpec((PEER_HEADS, tm, 2 * PEER_HALF), lambda i: (0, i, 0))],
        out_shape=[jax.ShapeDtypeStruct((t, D_MODEL), F32), jax.ShapeDtypeStruct((t, D_MODEL), FP8),
                   jax.ShapeDtypeStruct((t, LANES), F32),
                   jax.ShapeDtypeStruct((PEER_HEADS, t, 2 * PEER_HALF), BF16)],
        compiler_params=pltpu.CompilerParams(dimension_semantics=("arbitrary",),
                                             vmem_limit_bytes=VMEM_LIMIT_BYTES),
        name="outproj",
    )(mla_p, mla_s, gla_p, gla_s, xp, xs, w_o_b, ln1_g.reshape(1, -1), ln1_b.reshape(1, -1), peer_wq_b)


PEER_HEADS = 8
N_KEYS = 128
PEER_HALF = 128
PEER_TOPK = 16
ROUTE_TM = 256
CAND_WIDE_RANKS = 8
CAND_ROWS = PEER_TOPK + (CAND_WIDE_RANKS - 1) * SUBLANES + (PEER_TOPK - CAND_WIDE_RANKS)


def _first_max(x, row):
    n = x.shape[0]
    blocks = [(x[r:r + SUBLANES], row[r:r + SUBLANES]) for r in range(0, n, SUBLANES)]
    while len(blocks) > 1:
        merged = []
        for j in range(0, len(blocks) - 1, 2):
            (va, ia), (vb, ib) = blocks[j], blocks[j + 1]
            take = va >= vb
            merged.append((jnp.where(take, va, vb), jnp.where(take, ia, ib)))
        if len(blocks) % 2:
            merged.append(blocks[-1])
        blocks = merged
    v, i = blocks[0]
    m = jnp.max(v, axis=0, keepdims=True)
    pick = jnp.min(jnp.where(v == m, i, float(n)), axis=0, keepdims=True)
    return m, pick


def _extract_top(xs, row, n_out, on_pick):
    xs = list(xs)
    for k in range(n_out):
        for p, x in enumerate(xs):
            m, pick = _first_max(x, row)
            hit = row == pick
            on_pick(p, k, m, hit, pick)
            xs[p] = jnp.where(hit, -jnp.inf, x)


def _route_kernel(qp_ref, keys_ref, row_ref, g_ref, i1_ref, i2_ref, sv_ref, si_ref, gt_ref, et_ref):
    lax.fori_loop(0, PEER_HEADS,
                  functools.partial(_route_head, qp_ref, keys_ref, row_ref, sv_ref, si_ref, gt_ref, et_ref), 0)
    g_ref[...] = gt_ref[...].T
    expert = et_ref[...]
    key0 = jnp.floor(expert * (1.0 / N_KEYS))
    i1_ref[...] = key0.T
    i2_ref[...] = (expert - key0 * N_KEYS).T


def _route_head(qp_ref, keys_ref, row_ref, sv_ref, si_ref, gt_ref, et_ref, h, carry):
    tm = qp_ref.shape[1]
    q = qp_ref[h]
    scores = [lax.dot_general(keys_ref[half], q[:, half * PEER_HALF:(half + 1) * PEER_HALF], _NT,
                              preferred_element_type=F32) for half in range(2)]

    def keep(half, k, m, hit, pick):
        sv_ref[half, k:k + 1, :] = m
        si_ref[half, k:k + 1, :] = pick

    _extract_top(scores, row_ref[...], PEER_TOPK, keep)

    s1, s2 = sv_ref[0], sv_ref[1]
    i1, i2 = si_ref[0], si_ref[1]
    bc = lambda r, n: jnp.broadcast_to(r, (n, tm))
    wide = range(1, CAND_WIDE_RANKS)
    cand = jnp.concatenate([bc(s1[0:1], PEER_TOPK) + s2]
                           + [bc(s1[a:a + 1], SUBLANES) + s2[:SUBLANES] for a in wide]
                           + [s1[CAND_WIDE_RANKS:] + bc(s2[0:1], PEER_TOPK - CAND_WIDE_RANKS)], axis=0)
    i1 = i1 * N_KEYS
    c_expert = jnp.concatenate([bc(i1[0:1], PEER_TOPK) + i2]
                               + [bc(i1[a:a + 1], SUBLANES) + i2[:SUBLANES] for a in wide]
                               + [i1[CAND_WIDE_RANKS:] + bc(i2[0:1], PEER_TOPK - CAND_WIDE_RANKS)], axis=0)
    base = h * PEER_TOPK if isinstance(h, int) else pl.multiple_of(h * PEER_TOPK, PEER_TOPK)

    def keep_pair(_, k, m, hit, pick):
        gt_ref[pl.ds(base + k, 1), :] = m
        et_ref[pl.ds(base + k, 1), :] = jnp.sum(jnp.where(hit, c_expert, 0.0), axis=0, keepdims=True)

    _extract_top([cand], row_ref[:CAND_ROWS], PEER_TOPK, keep_pair)

    sc = gt_ref[pl.ds(base, PEER_TOPK), :]
    e = jnp.exp(sc - sc[0:1])
    gt_ref[pl.ds(base, PEER_TOPK), :] = e / jnp.sum(e, axis=0, keepdims=True)
    return carry


def _route(qp, keys_b):
    t = qp.shape[1]
    tm = ROUTE_TM
    assert t % tm == 0
    n_sel = PEER_HEADS * PEER_TOPK
    out = pl.BlockSpec((tm, n_sel), lambda i: (i, 0))
    row_index = jnp.broadcast_to(jnp.arange(N_KEYS, dtype=F32)[:, None], (N_KEYS, tm))
    return pl.pallas_call(
        _route_kernel,
        grid=(t // tm,),
        in_specs=[pl.BlockSpec((PEER_HEADS, tm, 2 * PEER_HALF), lambda i: (0, i, 0)),
                  pl.BlockSpec(keys_b.shape, lambda i: (0, 0, 0)),
                  pl.BlockSpec((N_KEYS, tm), lambda i: (0, 0))],
        out_specs=[out, out, out],
        out_shape=[jax.ShapeDtypeStruct((t, n_sel), F32)] * 3,
        scratch_shapes=[pltpu.VMEM((2, PEER_TOPK, tm), F32), pltpu.VMEM((2, PEER_TOPK, tm), F32),
                        pltpu.VMEM((n_sel, tm), F32), pltpu.VMEM((n_sel, tm), F32)],
        compiler_params=pltpu.CompilerParams(dimension_semantics=("arbitrary",),
                                             vmem_limit_bytes=VMEM_LIMIT_BYTES),
        name="peer_route",
    )(qp, keys_b, row_index)


SCATTER_TB = 128
SCATTER_UNROLL = 32
W_ROW_TILE = SUBLANES


def _w_table_tokens(t):
    return -(-t // EXP_TM) * EXP_TM


def _scatter_kernel(g_ref, i1_ref, i2_ref, w_ref, *, n_token_blocks):
    tb = g_ref.shape[0]
    key = lax.broadcasted_iota(jnp.int32, (N_KEYS, g_ref.shape[1]), 0).astype(F32)

    @pl.when(pl.program_id(0) >= n_token_blocks)
    def _():
        w_ref[...] = jnp.zeros(w_ref.shape, F32)

    def body(grp, carry):
        t0 = pl.multiple_of(grp * SCATTER_UNROLL, SCATTER_UNROLL)
        g8 = g_ref[pl.ds(t0, SCATTER_UNROLL), :]
        i18 = i1_ref[pl.ds(t0, SCATTER_UNROLL), :]
        i28 = i2_ref[pl.ds(t0, SCATTER_UNROLL), :]
        zero = jnp.zeros(key.shape, BF16)
        for u in range(0, SCATTER_UNROLL, 2):
            a_ts, b_ts = [], []
            for v in (u, u + 1):
                g = jnp.broadcast_to(g8[v:v + 1], key.shape)
                i1 = jnp.broadcast_to(i18[v:v + 1], key.shape)
                i2 = jnp.broadcast_to(i28[v:v + 1], key.shape)
                a_ts.append(jnp.where(key == i1, g, 0.0).astype(BF16))
                b_ts.append(jnp.where(key == i2, 1.0, 0.0).astype(BF16))
            a_pair = jnp.concatenate(a_ts, axis=1)
            b_pair = jnp.concatenate([jnp.concatenate([b_ts[0], zero], axis=1),
                                      jnp.concatenate([zero, b_ts[1]], axis=1)], axis=0)
            w = lax.dot_general(a_pair, b_pair, _NT, preferred_element_type=F32)
            for n, v in enumerate((u, u + 1)):
                w_ref[:, pl.ds(t0 + v, 1), :, :] = w[:, n * N_KEYS:(n + 1) * N_KEYS].reshape(
                    N_KEYS // W_ROW_TILE, 1, W_ROW_TILE, N_KEYS)
        return carry

    @pl.when(pl.program_id(0) < n_token_blocks)
    def _():
        lax.fori_loop(0, tb // SCATTER_UNROLL, body, 0)


def _scatter(g, i1, i2):
    t = g.shape[0]
    tb = SCATTER_TB
    t_pad = _w_table_tokens(t)
    assert t % tb == 0 and t_pad % tb == 0
    n_blocks = t // tb
    sel = pl.BlockSpec((tb, g.shape[1]), lambda i: (jnp.minimum(i, n_blocks - 1), 0))
    n_grp = N_KEYS // W_ROW_TILE
    return pl.pallas_call(
        functools.partial(_scatter_kernel, n_token_blocks=n_blocks),
        grid=(t_pad // tb,),
        in_specs=[sel, sel, sel],
        out_specs=pl.BlockSpec((n_grp, tb, W_ROW_TILE, N_KEYS), lambda i: (0, i, 0, 0)),
        out_shape=jax.ShapeDtypeStruct((n_grp, t_pad, W_ROW_TILE, N_KEYS), F32),
        compiler_params=pltpu.CompilerParams(dimension_semantics=("arbitrary",),
                                             vmem_limit_bytes=VMEM_LIMIT_BYTES),
        name="peer_scatter",
    )(g, i1, i2)


RS_TM = 128


def _route_scatter_kernel(qp_ref, keys_ref, row_ref, w_ref, sv_ref, si_ref, gt_ref, et_ref, sel_ref, *, n_tiles):
    i = pl.program_id(0)
    tm = qp_ref.shape[1]

    @pl.when(i == 0)
    def _():
        sel_ref[...] = jnp.zeros(sel_ref.shape, F32)

    @pl.when(i > n_tiles)
    def _():
        w_ref[...] = jnp.zeros(w_ref.shape, F32)

    @pl.when(i <= n_tiles)
    def _():
        cur = i % 2
        prev = 1 - cur
        key = lax.broadcasted_iota(jnp.int32, (N_KEYS, N_KEYS), 0).astype(F32)
        zero = jnp.zeros(key.shape, BF16)
        per_head = tm // PEER_HEADS
        for h in range(PEER_HEADS):
            _route_head(qp_ref, keys_ref, row_ref, sv_ref, si_ref, gt_ref, et_ref, h, 0)
            t0 = h * per_head
            g8 = sel_ref[prev, 0, t0:t0 + per_head, :]
            i18 = sel_ref[prev, 1, t0:t0 + per_head, :]
            i28 = sel_ref[prev, 2, t0:t0 + per_head, :]
            for u in range(0, per_head, 2):
                a_ts, b_ts = [], []
                for v in (u, u + 1):
                    g = jnp.broadcast_to(g8[v:v + 1], key.shape)
                    i1 = jnp.broadcast_to(i18[v:v + 1], key.shape)
                    i2 = jnp.broadcast_to(i28[v:v + 1], key.shape)
                    a_ts.append(jnp.where(key == i1, g, 0.0).astype(BF16))
                    b_ts.append(jnp.where(key == i2, 1.0, 0.0).astype(BF16))
                a_pair = jnp.concatenate(a_ts, axis=1)
                b_pair = jnp.concatenate([jnp.concatenate([b_ts[0], zero], axis=1),
                                          jnp.concatenate([zero, b_ts[1]], axis=1)], axis=0)
                w = lax.dot_general(a_pair, b_pair, _NT, preferred_element_type=F32)
                for n, v in enumerate((u, u + 1)):
                    w_ref[:, t0 + v, :, :] = w[:, n * N_KEYS:(n + 1) * N_KEYS].reshape(
                        N_KEYS // W_ROW_TILE, W_ROW_TILE, N_KEYS)
        sel_ref[cur, 0] = gt_ref[...].T
        expert = et_ref[...]
        key0 = jnp.floor(expert * (1.0 / N_KEYS))
        sel_ref[cur, 1] = key0.T
        sel_ref[cur, 2] = (expert - key0 * N_KEYS).T


def _route_scatter(qp, keys_b):
    t = qp.shape[1]
    tm = RS_TM
    t_pad = _w_table_tokens(t)
    assert t % tm == 0 and t_pad % tm == 0
    n_tiles = t // tm
    n_sel = PEER_HEADS * PEER_TOPK
    n_grp = N_KEYS // W_ROW_TILE
    row_index = jnp.broadcast_to(jnp.arange(N_KEYS, dtype=F32)[:, None], (N_KEYS, tm))
    return pl.pallas_call(
        functools.partial(_route_scatter_kernel, n_tiles=n_tiles),
        grid=(t_pad // tm + 1,),
        in_specs=[pl.BlockSpec((PEER_HEADS, tm, 2 * PEER_HALF), lambda i: (0, jnp.minimum(i, n_tiles - 1), 0)),
                  pl.BlockSpec(keys_b.shape, lambda i: (0, 0, 0)),
                  pl.BlockSpec((N_KEYS, tm), lambda i: (0, 0))],
        out_specs=pl.BlockSpec((n_grp, tm, W_ROW_TILE, N_KEYS), lambda i: (0, jnp.maximum(i - 1, 0), 0, 0)),
        out_shape=jax.ShapeDtypeStruct((n_grp, t_pad, W_ROW_TILE, N_KEYS), F32),
        scratch_shapes=[pltpu.VMEM((2, PEER_TOPK, tm), F32), pltpu.VMEM((2, PEER_TOPK, tm), F32),
                        pltpu.VMEM((n_sel, tm), F32), pltpu.VMEM((n_sel, tm), F32),
                        pltpu.VMEM((2, 3, tm, n_sel), F32)],
        compiler_params=pltpu.CompilerParams(dimension_semantics=("arbitrary",),
                                             vmem_limit_bytes=VMEM_LIMIT_BYTES),
        name="peer_route_scatter",
    )(qp, keys_b, row_index)


EXP_TM = 1024
EXP_ROWS = W_ROW_TILE
EXP_TE = EXP_ROWS * N_KEYS
EXP_SUBTILES = 4
GELU_C = 2.0 ** -0.5


def _experts_kernel(h8_ref, hinv_ref, w_ref, u_ref, uinv_ref, v_ref, vinv_ref, h_ref, g_ref, b_ref, y_ref):
    j = pl.program_id(1)
    tm = h8_ref.shape[0]

    @pl.when(j == 0)
    def _():
        y_ref[...] = DN_ALPHA * h_ref[...]

    sub = tm // EXP_SUBTILES
    u_inv = jnp.concatenate([uinv_ref[r, 0:1, :] for r in range(EXP_ROWS)], axis=1)
    half_v_inv = 0.5 * jnp.concatenate([vinv_ref[r, 0:1, :] for r in range(EXP_ROWS)], axis=1)
    acts = []
    for s in range(EXP_SUBTILES):
        h_inv = hinv_ref[s * sub:(s + 1) * sub, :]
        a8 = lax.dot_general(h8_ref[s * sub:(s + 1) * sub, :], u_ref[...], _NT, preferred_element_type=F32)
        acts.append(a8 * jnp.concatenate([h_inv] * EXP_ROWS, axis=1) * u_inv)
    for s in range(EXP_SUBTILES):
        w = jnp.concatenate([w_ref[pl.ds(s * sub * W_ROW_TILE + r, sub, stride=W_ROW_TILE), :]
                             for r in range(EXP_ROWS)], axis=1)
        a = acts[s]
        p = w * (a * (1.0 + lax.erf(a * GELU_C))) * half_v_inv
        pmax = jnp.maximum(jnp.max(jnp.abs(p), axis=1, keepdims=True), FP8_TINY)
        out = jnp.dot(_fp8_scaled(p, pmax), v_ref[...], preferred_element_type=F32)
        y_ref[s * sub:(s + 1) * sub, :] += out * (pmax * (1.0 / FP8_TARGET))

    @pl.when(j == pl.num_programs(1) - 1)
    def _():
        y_ref[...] = _layer_norm(y_ref[...], g_ref[...], b_ref[...])


def _experts(h8, h_inv, h, w_table, u8, u_inv, v8, v_inv, ln2_g, ln2_b, row0, n_rows):
    tm = min(EXP_TM, n_rows)
    assert n_rows % tm == 0 and row0 % tm == 0 and tm % (EXP_SUBTILES * FP8_ROW_TILE) == 0
    nt = n_rows // tm
    blk0 = row0 // tm
    ne = u8.shape[0] // EXP_TE
    tiles_per_group = w_table.shape[1] // tm
    w2d = w_table.reshape(-1, N_KEYS)
    once = lambda w: pl.BlockSpec((tm, w), lambda i, j: (blk0 + i, 0), pipeline_mode=pl.Buffered(1))
    return pl.pallas_call(
        _experts_kernel,
        grid=(nt, ne),
        in_specs=[once(D_MODEL), once(LANES),
                  pl.BlockSpec((tm * W_ROW_TILE, N_KEYS), lambda i, j: (j * tiles_per_group + blk0 + i, 0)),
                  pl.BlockSpec((EXP_TE, D_MODEL), lambda i, j: (j, 0)),
                  pl.BlockSpec((EXP_ROWS, SUBLANES, LANES), lambda i, j: (j, 0, 0)),
                  pl.BlockSpec((EXP_TE, D_MODEL), lambda i, j: (j, 0)),
                  pl.BlockSpec((EXP_ROWS, SUBLANES, LANES), lambda i, j: (j, 0, 0)),
                  once(D_MODEL), pl.BlockSpec((1, D_MODEL), lambda i, j: (0, 0)),
                  pl.BlockSpec((1, D_MODEL), lambda i, j: (0, 0))],
        out_specs=pl.BlockSpec((tm, D_MODEL), lambda i, j: (i, 0)),
        out_shape=jax.ShapeDtypeStruct((n_rows, D_MODEL), F32),
        compiler_params=pltpu.CompilerParams(dimension_semantics=("arbitrary", "arbitrary"),
                                             vmem_limit_bytes=VMEM_LIMIT_BYTES),
        name="peer_experts",
    )(h8, h_inv, w2d, u8, u_inv, v8, v_inv, h, ln2_g.reshape(1, -1), ln2_b.reshape(1, -1))


def kernel(x_prompt, x_sample, cache_kv_latent, cache_k_rope, state_gla, w_in, q_norm_g, w_uq, kv_norm_g, w_ukv,
           w_gk2, b_gk, gla_norm_g, w_o, ln1_g, ln1_b, peer_wq, peer_keys, peer_u, peer_v, ln2_g, ln2_b):
    bp, sp, _ = x_prompt.shape
    bs, ss, _ = x_sample.shape
    past = cache_kv_latent.shape[1]
    tp, ts = bp * sp, bs * ss
    xp = x_prompt.reshape(tp, D_MODEL)
    xs = x_sample.reshape(ts, D_MODEL)
    cos_p, sin_p = _rope_tables(jnp.arange(sp, dtype=jnp.int32))
    cos_s, sin_s = _rope_tables(past + jnp.arange(ss, dtype=jnp.int32))
    cos = jnp.concatenate([jnp.tile(cos_p, (bp, 1)), jnp.tile(cos_s, (bs, 1))])
    sin = jnp.concatenate([jnp.tile(sin_p, (bp, 1)), jnp.tile(sin_s, (bs, 1))])
    w_in_r, wq_r, wukv_r, wvt_r, wgk_r = _mixer_weights(w_in, w_uq, w_ukv, w_gk2)
    q, k, v, vt, ckv_p, ckv_s, kr_p, kr_s, gq, gk, gv, gate, lf = _mixer(
        xp, xs, cos, sin, w_in_r, wq_r, wukv_r, wvt_r, wgk_r, q_norm_g, kv_norm_g, b_gk)
    o_p, (u8, u_inv), (v8, v_inv) = _mla_prompt(q, k, vt, tp, peer_u, peer_v)
    o_s = _mla_sample(q, k, v, cache_kv_latent, cache_k_rope, wukv_r, tp, ss)
    g_p, st_p = _gla(gq, gk, gv, lf, gate, jnp.zeros((bp,) + state_gla.shape[1:], F32), gla_norm_g,
                     0, bp, sp // CHUNK)
    g_s, st_s = _gla(gq, gk, gv, lf, gate, state_gla, gla_norm_g, tp, bs, ss // CHUNK)

    h, h8, h_inv, qp = _outproj(o_p, o_s, g_p, g_s, xp, xs, w_o.astype(BF16), ln1_g, ln1_b, peer_wq.astype(BF16))
    w_table = _route_scatter(qp, peer_keys.astype(BF16))
    y_p = _experts(h8, h_inv, h, w_table, u8, u_inv, v8, v_inv, ln2_g, ln2_b, 0, tp)
    y_s = _experts(h8, h_inv, h, w_table, u8, u_inv, v8, v_inv, ln2_g, ln2_b, tp, ts)

    dt = x_prompt.dtype
    return (y_p.reshape(bp, sp, D_MODEL), y_s.reshape(bs, ss, D_MODEL),
            ckv_p.reshape(bp, sp, KV_RANK), kr_p.reshape(bp, sp, ROPE_DIM), st_p.astype(dt),
            ckv_s.reshape(bs, ss, KV_RANK), kr_s.reshape(bs, ss, ROPE_DIM), st_s.astype(dt))
```

```python
import functools
import math

import jax
import jax.numpy as jnp
from jax import lax
from jax.experimental import pallas as pl
from jax.experimental.pallas import tpu as pltpu

F32 = jnp.float32
BF16 = jnp.bfloat16
FP8 = jnp.float8_e4m3fn
FP8_ROW_TILE = 32
FP8_TARGET = 128.0
FP8_TINY = 1e-30


def _fp8_scaled(x, amax):
    return (x * (FP8_TARGET / amax)).astype(FP8)

LANES = 128
SUBLANES = 8
VMEM_LIMIT_BYTES = 60 * 1024 * 1024

D_MODEL = 2048
CHUNK = 64
MLA_HEADS = 8
Q_RANK = 512
KV_RANK = 256
NOPE_DIM = 128
ROPE_DIM = 64
V_DIM = 128
V_AUG = V_DIM + 16
ROPE_THETA = 10000.0
MLA_SCALE = (NOPE_DIM + ROPE_DIM) ** -0.5
Q_SCALE = MLA_SCALE * math.log2(math.e)
QK_PAD = 256
GLA_HEADS = 4
GLA_DK = 128
GLA_DV = 256
GATE_RANK = 16
GATE_NORMALIZER = 16.0
GLA_KW = GLA_HEADS * GLA_DK
GLA_WIDTH = GLA_HEADS * GLA_DV
MLA_WIDTH = MLA_HEADS * V_DIM
DN_ALPHA = 2.0 ** 0.25
LN_EPS = 1e-5
RMS_EPS = 1e-6

Z_CQ = 0
Z_CKV = Z_CQ + Q_RANK
Z_KR = Z_CKV + KV_RANK
Z_KROT = Z_KR + LANES
Z_GQ = Z_KROT + LANES
Z_GK = Z_GQ + GLA_KW
Z_GV = Z_GK + GLA_KW
Z_GATE = Z_GV + GLA_WIDTH
Z_WIDTH = Z_GATE + GLA_WIDTH


def _const_spec(shape):
    nd = len(shape)
    return pl.BlockSpec(shape, lambda *_: (0,) * nd, pipeline_mode=pl.Buffered(1))


def _rms(x, g):
    return x * lax.rsqrt(jnp.mean(x * x, axis=-1, keepdims=True) + RMS_EPS) * g


MIX_TM = 256


def _two_stream_specs(tm, width, n_first):
    return (pl.BlockSpec((tm, width), lambda i: (jnp.minimum(i, n_first - 1), 0)),
            pl.BlockSpec((tm, width), lambda i: (jnp.maximum(i - n_first, 0), 0)))


def _mixer_kernel(xp_ref, xs_ref, w_in_ref, wq_ref, wukv_ref, wvt_ref, wgk_ref, qg_ref, kvg_ref, bgk_ref,
                  cos_ref, sin_ref, q_ref, k_ref, v_ref, vt_ref, ckvp_ref, ckvs_ref, krp_ref, krs_ref,
                  gq_ref, gk_ref, gv_ref, gate_ref, lf_ref, *, n_prompt_tiles):
    in_prompt = pl.program_id(0) < n_prompt_tiles
    xb = jnp.where(in_prompt, xp_ref[...], xs_ref[...]).astype(BF16)
    cos = cos_ref[...]
    sin = sin_ref[...]

    z_lat = jnp.dot(xb, w_in_ref[:, Z_CQ:Z_GQ], preferred_element_type=F32)
    cq = _rms(z_lat[:, Z_CQ:Z_CKV], qg_ref[...])
    c_kv = _rms(z_lat[:, Z_CKV:Z_KR], kvg_ref[...])
    krga = z_lat[:, Z_KR:Z_KROT]
    k_rope = krga * cos + z_lat[:, Z_KROT:Z_GQ] * sin

    @pl.when(in_prompt)
    def _():
        ckvp_ref[...] = c_kv
        krp_ref[...] = k_rope[:, :ROPE_DIM]

    @pl.when(jnp.logical_not(in_prompt))
    def _():
        ckvs_ref[...] = c_kv
        krs_ref[...] = k_rope[:, :ROPE_DIM]

    k_rope_b = k_rope.astype(BF16)

    q = jnp.dot(cq.astype(BF16), wq_ref[...], preferred_element_type=F32)
    kv = jnp.dot(c_kv.astype(BF16), wukv_ref[...], preferred_element_type=F32)
    for h in range(MLA_HEADS):
        lo = h * QK_PAD
        q_ref[:, lo:lo + NOPE_DIM] = (q[:, lo:lo + NOPE_DIM] * Q_SCALE).astype(BF16)
        rot = q[:, MLA_HEADS * QK_PAD + h * LANES:MLA_HEADS * QK_PAD + (h + 1) * LANES]
        q_ref[:, lo + NOPE_DIM:lo + QK_PAD] = (
            (q[:, lo + NOPE_DIM:lo + QK_PAD] * cos + rot * sin) * Q_SCALE).astype(BF16)
        k_ref[:, lo:lo + NOPE_DIM] = kv[:, h * NOPE_DIM:(h + 1) * NOPE_DIM].astype(BF16)
        k_ref[:, lo + NOPE_DIM:lo + QK_PAD] = k_rope_b
    v_ref[...] = kv[:, MLA_HEADS * NOPE_DIM:].astype(BF16)
    vt = jnp.dot(wvt_ref[...], c_kv.T.astype(BF16), preferred_element_type=F32).astype(BF16)
    ones = jnp.ones((V_AUG - V_DIM, vt.shape[1]), BF16)
    for h in range(MLA_HEADS):
        vt_ref[h * V_AUG:h * V_AUG + V_DIM, :] = vt[h * V_DIM:(h + 1) * V_DIM]
        vt_ref[h * V_AUG + V_DIM:(h + 1) * V_AUG, :] = ones

    pre = jnp.dot(krga.astype(BF16), wgk_ref[...], preferred_element_type=F32) + bgk_ref[...]
    lf_ref[...] = (jnp.minimum(pre, 0.0) - jnp.log1p(jnp.exp(-jnp.abs(pre)))) * (1.0 / GATE_NORMALIZER)

    gq_ref[...] = jnp.dot(xb, w_in_ref[:, Z_GQ:Z_GK], preferred_element_type=F32)
    gk_ref[...] = jnp.dot(xb, w_in_ref[:, Z_GK:Z_GV], preferred_element_type=F32)
    gv_ref[...] = jnp.dot(xb, w_in_ref[:, Z_GV:Z_GATE], preferred_element_type=F32)
    gate_ref[...] = jnp.dot(xb, w_in_ref[:, Z_GATE:Z_WIDTH], preferred_element_type=F32)


def _rotate_half_cols(w):
    half = ROPE_DIM // 2
    return jnp.concatenate([-w[..., half:], w[..., :half]], axis=-1)


def _mixer_weights(w_in, w_uq, w_ukv, w_gk2):
    pts = []
    acc = 0
    for s in (Q_RANK, KV_RANK, ROPE_DIM, GLA_KW, GLA_KW, GLA_WIDTH, GLA_WIDTH):
        acc += s
        pts.append(acc)
    cq, ckv, kr, gq, gk, gv, gate, ga = jnp.split(w_in, pts, axis=1)
    d = w_in.shape[0]
    w_in_r = jnp.concatenate(
        [cq, ckv, kr, ga, jnp.zeros((d, LANES - ROPE_DIM - GATE_RANK), F32),
         _rotate_half_cols(kr), jnp.zeros((d, LANES - ROPE_DIM), F32), gq, gk, gv, gate], axis=1).astype(BF16)

    wq = w_uq.reshape(Q_RANK, MLA_HEADS, NOPE_DIM + ROPE_DIM)
    wq_rope = wq[..., NOPE_DIM:]
    pad = jnp.zeros((Q_RANK, MLA_HEADS, QK_PAD - NOPE_DIM - ROPE_DIM), F32)
    wq_main = jnp.concatenate([wq, pad], axis=-1).reshape(Q_RANK, MLA_HEADS * QK_PAD)
    wq_rot = jnp.concatenate([_rotate_half_cols(wq_rope), pad], axis=-1).reshape(Q_RANK, MLA_HEADS * LANES)
    wq_r = jnp.concatenate([wq_main, wq_rot], axis=1).astype(BF16)

    wkv = w_ukv.reshape(KV_RANK, MLA_HEADS, NOPE_DIM + V_DIM)
    wukv_r = jnp.concatenate([wkv[..., :NOPE_DIM].reshape(KV_RANK, -1),
                              wkv[..., NOPE_DIM:].reshape(KV_RANK, -1)], axis=1).astype(BF16)

    wvt_r = wkv[..., NOPE_DIM:].reshape(KV_RANK, -1).T.astype(BF16)

    wgk_r = jnp.zeros((LANES, GLA_KW), F32).at[ROPE_DIM:ROPE_DIM + GATE_RANK].set(w_gk2).astype(BF16)
    return w_in_r, wq_r, wukv_r, wvt_r, wgk_r


def _rope_tables(pos):
    half = ROPE_DIM // 2
    freqs = ROPE_THETA ** (-jnp.arange(half, dtype=F32) / half)
    ang = pos.astype(F32)[:, None] * freqs[None, :]
    zeros = jnp.zeros((pos.shape[0], LANES - ROPE_DIM), F32)
    cos = jnp.concatenate([jnp.cos(ang), jnp.cos(ang), zeros], axis=1)
    sin = jnp.concatenate([jnp.sin(ang), jnp.sin(ang), zeros], axis=1)
    return cos, sin


def _mixer(xp, xs, cos, sin, w_in_r, wq_r, wukv_r, wvt_r, wgk_r, q_norm_g, kv_norm_g, b_gk):
    tp, ts = xp.shape[0], xs.shape[0]
    t = tp + ts
    tm = MIX_TM
    assert tp % tm == 0 and ts % tm == 0
    n_p = tp // tm
    row = lambda w: pl.BlockSpec((tm, w), lambda i: (i, 0))
    col = pl.BlockSpec((MLA_HEADS * V_AUG, tm), lambda i: (0, i))
    full = lambda w, dt: (row(w), jax.ShapeDtypeStruct((t, w), dt))
    ckv_p, ckv_s = _two_stream_specs(tm, KV_RANK, n_p)
    kr_p, kr_s = _two_stream_specs(tm, ROPE_DIM, n_p)
    outs = [full(MLA_HEADS * QK_PAD, BF16), full(MLA_HEADS * QK_PAD, BF16), full(MLA_WIDTH, BF16),
            (col, jax.ShapeDtypeStruct((MLA_HEADS * V_AUG, t), BF16)),
            (ckv_p, jax.ShapeDtypeStruct((tp, KV_RANK), F32)), (ckv_s, jax.ShapeDtypeStruct((ts, KV_RANK), F32)),
            (kr_p, jax.ShapeDtypeStruct((tp, ROPE_DIM), F32)), (kr_s, jax.ShapeDtypeStruct((ts, ROPE_DIM), F32)),
            full(GLA_KW, F32), full(GLA_KW, F32), full(GLA_WIDTH, F32), full(GLA_WIDTH, F32), full(GLA_KW, F32)]
    return pl.pallas_call(
        functools.partial(_mixer_kernel, n_prompt_tiles=n_p),
        grid=(t // tm,),
        in_specs=[*_two_stream_specs(tm, D_MODEL, n_p), _const_spec(w_in_r.shape), _const_spec(wq_r.shape),
                  _const_spec(wukv_r.shape), _const_spec(wvt_r.shape), _const_spec(wgk_r.shape),
                  _const_spec((1, Q_RANK)), _const_spec((1, KV_RANK)), _const_spec((1, GLA_KW)),
                  row(LANES), row(LANES)],
        out_specs=[o[0] for o in outs],
        out_shape=[o[1] for o in outs],
        compiler_params=pltpu.CompilerParams(dimension_semantics=("arbitrary",),
                                             vmem_limit_bytes=VMEM_LIMIT_BYTES),
        name="mixer",
    )(xp, xs, w_in_r, wq_r, wukv_r, wvt_r, wgk_r, q_norm_g.reshape(1, -1), kv_norm_g.reshape(1, -1),
      b_gk.reshape(1, -1), cos, sin)


MLA_TQ = 512
MLA_LOOKAHEAD = 2
MLA_SIDE_ROWS = 128
NEG_BIG = -1e30
_NT = (((1,), (1,)), ((), ()))


def _mla_prompt_kernel(qi_ref, ki_ref, q_ref, k_ref, vt_ref, u_ref, v_ref, o_ref, u8_ref, uinv_ref, v8_ref, vinv_ref,
                       m_ref, acc_ref, *, side_steps):
    step = pl.program_id(0)
    qi = qi_ref[step]
    ki = ki_ref[step]
    tq = q_ref.shape[0]
    tk = k_ref.shape[0]

    @pl.when(step < side_steps)
    def _():
        for src, dst, inv in ((u_ref, u8_ref, uinv_ref), (v_ref, v8_ref, vinv_ref)):
            x = src[...]
            amax = jnp.max(jnp.max(jnp.abs(x), axis=0, keepdims=True), axis=1, keepdims=True)
            amax = jnp.maximum(amax, FP8_TINY)
            dst[...] = _fp8_scaled(x, amax)
            inv[0] = jnp.broadcast_to(amax * (1.0 / FP8_TARGET), inv.shape[1:])

    @pl.when(ki == 0)
    def _():
        m_ref[...] = jnp.full(m_ref.shape, NEG_BIG, F32)
        acc_ref[...] = jnp.zeros(acc_ref.shape, F32)

    def update(masked):
        if masked:
            key_chunk = lax.broadcasted_iota(jnp.int32, (tk, tq), 0) // CHUNK
            qry_chunk = lax.broadcasted_iota(jnp.int32, (tk, tq), 1) // CHUNK
            visible = key_chunk <= qry_chunk
        def scores(h):
            return lax.dot_general(k_ref[:, h * QK_PAD:(h + 1) * QK_PAD], q_ref[:, h * QK_PAD:(h + 1) * QK_PAD],
                                   _NT, preferred_element_type=F32)

        ahead = [scores(h) for h in range(MLA_LOOKAHEAD)]
        for h in range(MLA_HEADS):
            s = ahead.pop(0)
            if h + MLA_LOOKAHEAD < MLA_HEADS:
                ahead.append(scores(h + MLA_LOOKAHEAD))
            if masked:
                s = jnp.where(visible, s, NEG_BIG)
            m_old = m_ref[h]
            m_new = jnp.maximum(m_old, jnp.max(s, axis=0, keepdims=True))
            p = jnp.exp2((s - m_new).astype(BF16))
            alpha = jnp.exp2(m_old - m_new)
            acc_ref[h] = alpha * acc_ref[h] + jnp.dot(vt_ref[h * V_AUG:(h + 1) * V_AUG, :], p,
                                                      preferred_element_type=F32)
            m_ref[h] = m_new

    @pl.when(ki < qi)
    def _():
        update(False)

    @pl.when(ki == qi)
    def _():
        update(True)
        for h in range(MLA_HEADS):
            acc = acc_ref[h]
            o_ref[:, h * V_DIM:(h + 1) * V_DIM] = (acc[:V_DIM] / acc[V_DIM:V_DIM + 1]).T.astype(BF16)


def _mla_prompt(q, k, vt, n_tokens, table_u, table_v):
    tq = MLA_TQ
    assert n_tokens % tq == 0
    nq = n_tokens // tq
    pairs = [(a, b) for a in range(nq) for b in range(a + 1)]
    qi = jnp.asarray([p[0] for p in pairs], jnp.int32)
    ki = jnp.asarray([p[1] for p in pairs], jnp.int32)
    rows, cols = table_u.shape
    assert table_v.shape == (rows, cols) and rows % MLA_SIDE_ROWS == 0
    side_steps = rows // MLA_SIDE_ROWS
    assert side_steps <= len(pairs)
    side = lambda: pl.BlockSpec((MLA_SIDE_ROWS, cols), lambda s, qi, ki: (jnp.minimum(s, side_steps - 1), 0))
    inv = lambda: pl.BlockSpec((1, SUBLANES, LANES), lambda s, qi, ki: (jnp.minimum(s, side_steps - 1), 0, 0))
    grid_spec = pltpu.PrefetchScalarGridSpec(
        num_scalar_prefetch=2,
        grid=(len(pairs),),
        in_specs=[pl.BlockSpec((tq, MLA_HEADS * QK_PAD), lambda s, qi, ki: (qi[s], 0)),
                  pl.BlockSpec((tq, MLA_HEADS * QK_PAD), lambda s, qi, ki: (ki[s], 0)),
                  pl.BlockSpec((MLA_HEADS * V_AUG, tq), lambda s, qi, ki: (0, ki[s])), side(), side()],
        out_specs=[pl.BlockSpec((tq, MLA_WIDTH), lambda s, qi, ki: (qi[s], 0)), side(), inv(), side(), inv()],
        scratch_shapes=[pltpu.VMEM((MLA_HEADS, 1, tq), F32), pltpu.VMEM((MLA_HEADS, V_AUG, tq), F32)],
    )
    table8 = jax.ShapeDtypeStruct((rows, cols), FP8)
    inv_shape = jax.ShapeDtypeStruct((side_steps, SUBLANES, LANES), F32)
    o, u8, u_inv, v8, v_inv = pl.pallas_call(
        functools.partial(_mla_prompt_kernel, side_steps=side_steps),
        grid_spec=grid_spec,
        out_shape=[jax.ShapeDtypeStruct((n_tokens, MLA_WIDTH), BF16), table8, inv_shape, table8, inv_shape],
        compiler_params=pltpu.CompilerParams(dimension_semantics=("arbitrary",),
                                             vmem_limit_bytes=VMEM_LIMIT_BYTES),
        name="mla_prompt",
    )(qi, ki, q, k, vt, table_u, table_v)
    return o, (u8, u_inv), (v8, v_inv)


def _mla_sample_kernel(q_ref, kn_ref, vn_ref, lat_ref, krc_ref, wukv_ref, o_ref):
    kvc = jnp.dot(lat_ref[0].astype(BF16), wukv_ref[...], preferred_element_type=F32)
    krc = krc_ref[0]
    for h in range(MLA_HEADS):
        q = q_ref[:, h * QK_PAD:(h + 1) * QK_PAD]
        knc = kvc[:, h * NOPE_DIM:(h + 1) * NOPE_DIM].astype(BF16)
        vc = kvc[:, MLA_HEADS * NOPE_DIM + h * V_DIM:MLA_HEADS * NOPE_DIM + (h + 1) * V_DIM].astype(BF16)
        s_c = (lax.dot_general(q[:, :NOPE_DIM], knc, _NT, preferred_element_type=F32)
               + lax.dot_general(q[:, NOPE_DIM:], krc, _NT, preferred_element_type=F32))
        s_n = lax.dot_general(q, kn_ref[:, h * QK_PAD:(h + 1) * QK_PAD], _NT, preferred_element_type=F32)
        m = jnp.maximum(jnp.max(s_c, axis=1, keepdims=True), jnp.max(s_n, axis=1, keepdims=True))
        p_c = jnp.exp2(s_c - m)
        p_n = jnp.exp2(s_n - m)
        l = jnp.sum(p_c, axis=1, keepdims=True) + jnp.sum(p_n, axis=1, keepdims=True)
        o = (jnp.dot(p_c.astype(BF16), vc, preferred_element_type=F32)
             + jnp.dot(p_n.astype(BF16), vn_ref[:, h * V_DIM:(h + 1) * V_DIM], preferred_element_type=F32))
        o_ref[:, h * V_DIM:(h + 1) * V_DIM] = (o / l).astype(BF16)


def _mla_sample(q, k, v, cache_lat, cache_kr, wukv_r, row0, n_new):
    bs, past, _ = cache_lat.shape
    assert row0 % n_new == 0
    blk0 = row0 // n_new
    krc = jnp.pad(cache_kr, ((0, 0), (0, 0), (0, LANES - ROPE_DIM))).astype(BF16)
    new = lambda w: pl.BlockSpec((n_new, w), lambda b: (blk0 + b, 0))
    return pl.pallas_call(
        _mla_sample_kernel,
        grid=(bs,),
        in_specs=[new(MLA_HEADS * QK_PAD), new(MLA_HEADS * QK_PAD), new(MLA_WIDTH),
                  pl.BlockSpec((1, past, KV_RANK), lambda b: (b, 0, 0)),
                  pl.BlockSpec((1, past, LANES), lambda b: (b, 0, 0)),
                  _const_spec(wukv_r.shape)],
        out_specs=pl.BlockSpec((n_new, MLA_WIDTH), lambda b: (b, 0)),
        out_shape=jax.ShapeDtypeStruct((bs * n_new, MLA_WIDTH), BF16),
        compiler_params=pltpu.CompilerParams(dimension_semantics=("arbitrary",),
                                             vmem_limit_bytes=VMEM_LIMIT_BYTES),
        name="mla_sample",
    )(q, k, v, cache_lat, krc, wukv_r)


GLA_CHUNKS_PER_STEP = 2
GLA_SUB = 16
_TN = (((0,), (0,)), ((), ()))


def _cumsum_rows(x):
    n = x.shape[0]
    row = lax.broadcasted_iota(jnp.int32, x.shape, 0)
    s = 1
    while s < n:
        x = x + jnp.where(row >= s, pltpu.roll(x, s, axis=0), 0.0)
        s *= 2
    return x


def _gla_kernel(q_ref, k_ref, v_ref, lf_ref, gate_ref, s0_ref, g_ref, o_ref, send_ref, st_ref, *, n_chunks):
    c = pl.program_id(1)
    n_rows = CHUNK
    n_sub = n_rows // GLA_SUB

    @pl.when(c == 0)
    def _():
        for h in range(GLA_HEADS):
            st_ref[h] = s0_ref[0, h].T

    sub_row = lax.broadcasted_iota(jnp.int32, (GLA_SUB, GLA_DK), 0)
    for h, ci in [(h, ci) for h in range(GLA_HEADS) for ci in range(q_ref.shape[0] // CHUNK)]:
        rows = slice(ci * CHUNK, (ci + 1) * CHUNK)
        ks = slice(h * GLA_DK, (h + 1) * GLA_DK)
        vs = slice(h * GLA_DV, (h + 1) * GLA_DV)
        q = q_ref[rows, ks] * (GLA_DK ** -0.5)
        k = k_ref[rows, ks]
        v = v_ref[rows, vs]
        vb = v.astype(BF16)
        b = _cumsum_rows(lf_ref[rows, ks])
        b_last = b[n_rows - 1:n_rows]
        st = st_ref[h]
        o_inter = lax.dot_general((q * jnp.exp(b)).astype(BF16), st.astype(BF16), _NT,
                                  preferred_element_type=F32)
        outs = []
        for i in range(n_sub):
            r0 = i * GLA_SUB
            bi = b[r0:r0 + GLA_SUB]
            qi = q[r0:r0 + GLA_SUB]
            o_i = o_inter[r0:r0 + GLA_SUB]
            if i > 0:
                b_ref = b[r0 - 1:r0]
                qh = (qi * jnp.exp(bi - b_ref)).astype(BF16)
                kh = (k[:r0] * jnp.exp(b_ref - b[:r0])).astype(BF16)
                a_off = lax.dot_general(qh, kh, _NT, preferred_element_type=F32)
                o_i = o_i + jnp.dot(a_off.astype(BF16), vb[:r0], preferred_element_type=F32)
            for j in range(GLA_SUB):
                r = r0 + j
                decay = jnp.exp(jnp.where(sub_row >= j, bi - b[r:r + 1], -jnp.inf))
                a_col = jnp.sum(qi * k[r:r + 1] * decay, axis=1, keepdims=True)
                o_i = o_i + a_col * v[r:r + 1]
            outs.append(o_i)
        o = jnp.concatenate(outs, axis=0)

        kd = (k * jnp.exp(b_last - b)).astype(BF16)
        st_new = st * jnp.exp(b_last) + lax.dot_general(vb, kd, _TN, preferred_element_type=F32)
        st_ref[h] = st_new

        gate = gate_ref[rows, vs]
        on = _rms(o, g_ref[...])
        o_ref[rows, vs] = (on * (gate / (1.0 + jnp.exp(-gate)))).astype(BF16)

    @pl.when(c == n_chunks - 1)
    def _():
        for h in range(GLA_HEADS):
            send_ref[0, h] = st_ref[h].T


def _gla(gq, gk, gv, lf, gate, s0, gla_norm_g, row0, n_seq, n_chunks):
    per_step = GLA_CHUNKS_PER_STEP if n_chunks % GLA_CHUNKS_PER_STEP == 0 else 1
    tr = per_step * CHUNK
    assert row0 % tr == 0
    blk0 = row0 // tr
    n_steps = n_chunks // per_step
    row = lambda w: pl.BlockSpec((tr, w), lambda b, c: (blk0 + b * n_steps + c, 0))
    state = pl.BlockSpec((1, GLA_HEADS, GLA_DK, GLA_DV), lambda b, c: (b, 0, 0, 0))
    return pl.pallas_call(
        functools.partial(_gla_kernel, n_chunks=n_steps),
        grid=(n_seq, n_steps),
        in_specs=[row(GLA_KW), row(GLA_KW), row(GLA_WIDTH), row(GLA_KW), row(GLA_WIDTH), state,
                  pl.BlockSpec((1, GLA_DV), lambda b, c: (0, 0))],
        out_specs=[pl.BlockSpec((tr, GLA_WIDTH), lambda b, c: (b * n_steps + c, 0)), state],
        out_shape=[jax.ShapeDtypeStruct((n_seq * n_chunks * CHUNK, GLA_WIDTH), BF16),
                   jax.ShapeDtypeStruct((n_seq, GLA_HEADS, GLA_DK, GLA_DV), F32)],
        scratch_shapes=[pltpu.VMEM((GLA_HEADS, GLA_DV, GLA_DK), F32)],
        compiler_params=pltpu.CompilerParams(dimension_semantics=("arbitrary", "arbitrary"),
                                             vmem_limit_bytes=VMEM_LIMIT_BYTES),
        name="gla",
    )(gq, gk, gv, lf, gate, s0, gla_norm_g.reshape(1, -1))


OUT_TM = 256


def _layer_norm(y, g, b):
    mu = jnp.mean(y, axis=-1, keepdims=True)
    yc = y - mu
    var = jnp.mean(yc * yc, axis=-1, keepdims=True)
    return yc * lax.rsqrt(var + LN_EPS) * g + b


def _outproj_kernel(mlap_ref, mlas_ref, glap_ref, glas_ref, xp_ref, xs_ref, wo_ref, g_ref, b_ref, wq_ref,
                    h_ref, h8_ref, hinv_ref, qp_ref, *, n_prompt_tiles):
    in_prompt = pl.program_id(0) < n_prompt_tiles
    mla = jnp.where(in_prompt, mlap_ref[...], mlas_ref[...])
    gla = jnp.where(in_prompt, glap_ref[...], glas_ref[...])
    x = jnp.where(in_prompt, xp_ref[...], xs_ref[...])
    mix = (jnp.dot(mla, wo_ref[:MLA_WIDTH], preferred_element_type=F32)
           + jnp.dot(gla, wo_ref[MLA_WIDTH:], preferred_element_type=F32))
    h = _layer_norm(DN_ALPHA * x + mix, g_ref[...], b_ref[...])
    h_ref[...] = h
    amax = jnp.maximum(jnp.max(jnp.abs(h), axis=1, keepdims=True), FP8_TINY)
    h8_ref[...] = _fp8_scaled(h, amax)
    hinv_ref[...] = jnp.broadcast_to(amax * (1.0 / FP8_TARGET), hinv_ref.shape)
    qp = jnp.dot(h.astype(BF16), wq_ref[...], preferred_element_type=F32).astype(BF16)
    per_head = qp_ref.shape[2]
    for head in range(qp_ref.shape[0]):
        qp_ref[head] = qp[:, head * per_head:(head + 1) * per_head]


def _outproj(mla_p, mla_s, gla_p, gla_s, xp, xs, w_o_b, ln1_g, ln1_b, peer_wq_b):
    tp, ts = xp.shape[0], xs.shape[0]
    t = tp + ts
    tm = OUT_TM
    assert tp % tm == 0 and ts % tm == 0
    n_p = tp // tm
    row = lambda w: pl.BlockSpec((tm, w), lambda i: (i, 0))
    return pl.pallas_call(
        functools.partial(_outproj_kernel, n_prompt_tiles=n_p),
        grid=(t // tm,),
        in_specs=[*_two_stream_specs(tm, MLA_WIDTH, n_p), *_two_stream_specs(tm, GLA_WIDTH, n_p),
                  *_two_stream_specs(tm, D_MODEL, n_p), _const_spec(w_o_b.shape),
                  _const_spec((1, D_MODEL)), _const_spec((1, D_MODEL)), _const_spec(peer_wq_b.shape)],
        out_specs=[row(D_MODEL), row(D_MODEL), row(LANES),
                   pl.BlockSpec((PEER_HEADS, tm, 2 * PEER_HALF), lambda i: (0, i, 0))],
        out_shape=[jax.ShapeDtypeStruct((t, D_MODEL), F32), jax.ShapeDtypeStruct((t, D_MODEL), FP8),
                   jax.ShapeDtypeStruct((t, LANES), F32),
                   jax.ShapeDtypeStruct((PEER_HEADS, t, 2 * PEER_HALF), BF16)],
        compiler_params=pltpu.CompilerParams(dimension_semantics=("arbitrary",),
                                             vmem_limit_bytes=VMEM_LIMIT_BYTES),
        name="outproj",
    )(mla_p, mla_s, gla_p, gla_s, xp, xs, w_o_b, ln1_g.reshape(1, -1), ln1_b.reshape(1, -1), peer_wq_b)


PEER_HEADS = 8
N_KEYS = 128
PEER_HALF = 128
PEER_TOPK = 16
CAND_WIDE_RANKS = 8
CAND_ROWS = PEER_TOPK + (CAND_WIDE_RANKS - 1) * SUBLANES + (PEER_TOPK - CAND_WIDE_RANKS)


def _first_max(x, row):
    n = x.shape[0]
    blocks = [(x[r:r + SUBLANES], row[r:r + SUBLANES]) for r in range(0, n, SUBLANES)]
    while len(blocks) > 1:
        merged = []
        for j in range(0, len(blocks) - 1, 2):
            (va, ia), (vb, ib) = blocks[j], blocks[j + 1]
            take = va >= vb
            merged.append((jnp.where(take, va, vb), jnp.where(take, ia, ib)))
        if len(blocks) % 2:
            merged.append(blocks[-1])
        blocks = merged
    v, i = blocks[0]
    m = jnp.max(v, axis=0, keepdims=True)
    pick = jnp.min(jnp.where(v == m, i, float(n)), axis=0, keepdims=True)
    return m, pick


def _extract_top(xs, row, n_out, on_pick):
    xs = list(xs)
    for k in range(n_out):
        for p, x in enumerate(xs):
            m, pick = _first_max(x, row)
            hit = row == pick
            on_pick(p, k, m, hit, pick)
            xs[p] = jnp.where(hit, -jnp.inf, x)


def _route_head(qp_ref, keys_ref, row_ref, sv_ref, si_ref, gt_ref, et_ref, h):
    tm = qp_ref.shape[1]
    q = qp_ref[h]
    scores = [lax.dot_general(keys_ref[half], q[:, half * PEER_HALF:(half + 1) * PEER_HALF], _NT,
                              preferred_element_type=F32) for half in range(2)]

    def keep(half, k, m, hit, pick):
        sv_ref[half, k:k + 1, :] = m
        si_ref[half, k:k + 1, :] = pick

    _extract_top(scores, row_ref[...], PEER_TOPK, keep)

    s1, s2 = sv_ref[0], sv_ref[1]
    i1, i2 = si_ref[0], si_ref[1]
    bc = lambda r, n: jnp.broadcast_to(r, (n, tm))
    wide = range(1, CAND_WIDE_RANKS)
    cand = jnp.concatenate([bc(s1[0:1], PEER_TOPK) + s2]
                           + [bc(s1[a:a + 1], SUBLANES) + s2[:SUBLANES] for a in wide]
                           + [s1[CAND_WIDE_RANKS:] + bc(s2[0:1], PEER_TOPK - CAND_WIDE_RANKS)], axis=0)
    i1 = i1 * N_KEYS
    c_expert = jnp.concatenate([bc(i1[0:1], PEER_TOPK) + i2]
                               + [bc(i1[a:a + 1], SUBLANES) + i2[:SUBLANES] for a in wide]
                               + [i1[CAND_WIDE_RANKS:] + bc(i2[0:1], PEER_TOPK - CAND_WIDE_RANKS)], axis=0)
    base = h * PEER_TOPK

    def keep_pair(_, k, m, hit, pick):
        gt_ref[base + k:base + k + 1, :] = m
        et_ref[base + k:base + k + 1, :] = jnp.sum(jnp.where(hit, c_expert, 0.0), axis=0, keepdims=True)

    _extract_top([cand], row_ref[:CAND_ROWS], PEER_TOPK, keep_pair)

    sc = gt_ref[base:base + PEER_TOPK, :]
    e = jnp.exp(sc - sc[0:1])
    gt_ref[base:base + PEER_TOPK, :] = e / jnp.sum(e, axis=0, keepdims=True)


RS_TM = 128
W_ROW_TILE = SUBLANES


def _w_table_tokens(t):
    return -(-t // EXP_TM) * EXP_TM


def _route_scatter_kernel(qp_ref, keys_ref, row_ref, w_ref, sv_ref, si_ref, gt_ref, et_ref, sel_ref, *, n_tiles):
    i = pl.program_id(0)
    tm = qp_ref.shape[1]

    @pl.when(i == 0)
    def _():
        sel_ref[...] = jnp.zeros(sel_ref.shape, F32)

    @pl.when(i > n_tiles)
    def _():
        w_ref[...] = jnp.zeros(w_ref.shape, F32)

    @pl.when(i <= n_tiles)
    def _():
        cur = i % 2
        prev = 1 - cur
        key = lax.broadcasted_iota(jnp.int32, (N_KEYS, N_KEYS), 0).astype(F32)
        zero = jnp.zeros(key.shape, BF16)
        per_head = tm // PEER_HEADS
        for h in range(PEER_HEADS):
            _route_head(qp_ref, keys_ref, row_ref, sv_ref, si_ref, gt_ref, et_ref, h)
            t0 = h * per_head
            g8 = sel_ref[prev, 0, t0:t0 + per_head, :]
            i18 = sel_ref[prev, 1, t0:t0 + per_head, :]
            i28 = sel_ref[prev, 2, t0:t0 + per_head, :]
            for u in range(0, per_head, 2):
                a_ts, b_ts = [], []
                for v in (u, u + 1):
                    g = jnp.broadcast_to(g8[v:v + 1], key.shape)
                    i1 = jnp.broadcast_to(i18[v:v + 1], key.shape)
                    i2 = jnp.broadcast_to(i28[v:v + 1], key.shape)
                    a_ts.append(jnp.where(key == i1, g, 0.0).astype(BF16))
                    b_ts.append(jnp.where(key == i2, 1.0, 0.0).astype(BF16))
                a_pair = jnp.concatenate(a_ts, axis=1)
                b_pair = jnp.concatenate([jnp.concatenate([b_ts[0], zero], axis=1),
                                          jnp.concatenate([zero, b_ts[1]], axis=1)], axis=0)
                w = lax.dot_general(a_pair, b_pair, _NT, preferred_element_type=F32)
                for n, v in enumerate((u, u + 1)):
                    w_ref[:, t0 + v, :, :] = w[:, n * N_KEYS:(n + 1) * N_KEYS].reshape(
                        N_KEYS // W_ROW_TILE, W_ROW_TILE, N_KEYS)
        sel_ref[cur, 0] = gt_ref[...].T
        expert = et_ref[...]
        key0 = jnp.floor(expert * (1.0 / N_KEYS))
        sel_ref[cur, 1] = key0.T
        sel_ref[cur, 2] = (expert - key0 * N_KEYS).T


def _route_scatter(qp, keys_b):
    t = qp.shape[1]
    tm = RS_TM
    t_pad = _w_table_tokens(t)
    assert t % tm == 0 and t_pad % tm == 0
    n_tiles = t // tm
    n_sel = PEER_HEADS * PEER_TOPK
    n_grp = N_KEYS // W_ROW_TILE
    row_index = jnp.broadcast_to(jnp.arange(N_KEYS, dtype=F32)[:, None], (N_KEYS, tm))
    return pl.pallas_call(
        functools.partial(_route_scatter_kernel, n_tiles=n_tiles),
        grid=(t_pad // tm + 1,),
        in_specs=[pl.BlockSpec((PEER_HEADS, tm, 2 * PEER_HALF), lambda i: (0, jnp.minimum(i, n_tiles - 1), 0)),
                  pl.BlockSpec(keys_b.shape, lambda i: (0, 0, 0)),
                  pl.BlockSpec((N_KEYS, tm), lambda i: (0, 0))],
        out_specs=pl.BlockSpec((n_grp, tm, W_ROW_TILE, N_KEYS), lambda i: (0, jnp.maximum(i - 1, 0), 0, 0)),
        out_shape=jax.ShapeDtypeStruct((n_grp, t_pad, W_ROW_TILE, N_KEYS), F32),
        scratch_shapes=[pltpu.VMEM((2, PEER_TOPK, tm), F32), pltpu.VMEM((2, PEER_TOPK, tm), F32),
                        pltpu.VMEM((n_sel, tm), F32), pltpu.VMEM((n_sel, tm), F32),
                        pltpu.VMEM((2, 3, tm, n_sel), F32)],
        compiler_params=pltpu.CompilerParams(dimension_semantics=("arbitrary",),
                                             vmem_limit_bytes=VMEM_LIMIT_BYTES),
        name="peer_route_scatter",
    )(qp, keys_b, row_index)


EXP_TM = 1024
EXP_ROWS = W_ROW_TILE
EXP_TE = EXP_ROWS * N_KEYS
EXP_SUBTILES = 4
GELU_C = 2.0 ** -0.5


def _experts_kernel(h8_ref, hinv_ref, w_ref, u_ref, uinv_ref, v_ref, vinv_ref, h_ref, g_ref, b_ref, y_ref):
    j = pl.program_id(1)
    tm = h8_ref.shape[0]

    @pl.when(j == 0)
    def _():
        y_ref[...] = DN_ALPHA * h_ref[...]

    sub = tm // EXP_SUBTILES
    u_inv = jnp.concatenate([uinv_ref[r, 0:1, :] for r in range(EXP_ROWS)], axis=1)
    half_v_inv = 0.5 * jnp.concatenate([vinv_ref[r, 0:1, :] for r in range(EXP_ROWS)], axis=1)
    acts = []
    for s in range(EXP_SUBTILES):
        h_inv = hinv_ref[s * sub:(s + 1) * sub, :]
        a8 = lax.dot_general(h8_ref[s * sub:(s + 1) * sub, :], u_ref[...], _NT, preferred_element_type=F32)
        acts.append(a8 * jnp.concatenate([h_inv] * EXP_ROWS, axis=1) * u_inv)
    for s in range(EXP_SUBTILES):
        w = jnp.concatenate([w_ref[pl.ds(s * sub * W_ROW_TILE + r, sub, stride=W_ROW_TILE), :]
                             for r in range(EXP_ROWS)], axis=1)
        a = acts[s]
        p = w * (a * (1.0 + lax.erf(a * GELU_C))) * half_v_inv
        pmax = jnp.maximum(jnp.max(jnp.abs(p), axis=1, keepdims=True), FP8_TINY)
        out = jnp.dot(_fp8_scaled(p, pmax), v_ref[...], preferred_element_type=F32)
        y_ref[s * sub:(s + 1) * sub, :] += out * (pmax * (1.0 / FP8_TARGET))

    @pl.when(j == pl.num_programs(1) - 1)
    def _():
        y_ref[...] = _layer_norm(y_ref[...], g_ref[...], b_ref[...])


def _experts(h8, h_inv, h, w_table, u8, u_inv, v8, v_inv, ln2_g, ln2_b, row0, n_rows):
    tm = min(EXP_TM, n_rows)
    assert n_rows % tm == 0 and row0 % tm == 0 and tm % (EXP_SUBTILES * FP8_ROW_TILE) == 0
    nt = n_rows // tm
    blk0 = row0 // tm
    ne = u8.shape[0] // EXP_TE
    tiles_per_group = w_table.shape[1] // tm
    w2d = w_table.reshape(-1, N_KEYS)
    once = lambda w: pl.BlockSpec((tm, w), lambda i, j: (blk0 + i, 0), pipeline_mode=pl.Buffered(1))
    return pl.pallas_call(
        _experts_kernel,
        grid=(nt, ne),
        in_specs=[once(D_MODEL), once(LANES),
                  pl.BlockSpec((tm * W_ROW_TILE, N_KEYS), lambda i, j: (j * tiles_per_group + blk0 + i, 0)),
                  pl.BlockSpec((EXP_TE, D_MODEL), lambda i, j: (j, 0)),
                  pl.BlockSpec((EXP_ROWS, SUBLANES, LANES), lambda i, j: (j, 0, 0)),
                  pl.BlockSpec((EXP_TE, D_MODEL), lambda i, j: (j, 0)),
                  pl.BlockSpec((EXP_ROWS, SUBLANES, LANES), lambda i, j: (j, 0, 0)),
                  once(D_MODEL), pl.BlockSpec((1, D_MODEL), lambda i, j: (0, 0)),
                  pl.BlockSpec((1, D_MODEL), lambda i, j: (0, 0))],
        out_specs=pl.BlockSpec((tm, D_MODEL), lambda i, j: (i, 0)),
        out_shape=jax.ShapeDtypeStruct((n_rows, D_MODEL), F32),
        compiler_params=pltpu.CompilerParams(dimension_semantics=("arbitrary", "arbitrary"),
                                             vmem_limit_bytes=VMEM_LIMIT_BYTES),
        name="peer_experts",
    )(h8, h_inv, w2d, u8, u_inv, v8, v_inv, h, ln2_g.reshape(1, -1), ln2_b.reshape(1, -1))


def kernel(x_prompt, x_sample, cache_kv_latent, cache_k_rope, state_gla, w_in, q_norm_g, w_uq, kv_norm_g, w_ukv,
           w_gk2, b_gk, gla_norm_g, w_o, ln1_g, ln1_b, peer_wq, peer_keys, peer_u, peer_v, ln2_g, ln2_b):
    bp, sp, _ = x_prompt.shape
    bs, ss, _ = x_sample.shape
    past = cache_kv_latent.shape[1]
    tp, ts = bp * sp, bs * ss
    xp = x_prompt.reshape(tp, D_MODEL)
    xs = x_sample.reshape(ts, D_MODEL)
    cos_p, sin_p = _rope_tables(jnp.arange(sp, dtype=jnp.int32))
    cos_s, sin_s = _rope_tables(past + jnp.arange(ss, dtype=jnp.int32))
    cos = jnp.concatenate([jnp.tile(cos_p, (bp, 1)), jnp.tile(cos_s, (bs, 1))])
    sin = jnp.concatenate([jnp.tile(sin_p, (bp, 1)), jnp.tile(sin_s, (bs, 1))])
    w_in_r, wq_r, wukv_r, wvt_r, wgk_r = _mixer_weights(w_in, w_uq, w_ukv, w_gk2)
    q, k, v, vt, ckv_p, ckv_s, kr_p, kr_s, gq, gk, gv, gate, lf = _mixer(
        xp, xs, cos, sin, w_in_r, wq_r, wukv_r, wvt_r, wgk_r, q_norm_g, kv_norm_g, b_gk)
    o_p, (u8, u_inv), (v8, v_inv) = _mla_prompt(q, k, vt, tp, peer_u, peer_v)
    o_s = _mla_sample(q, k, v, cache_kv_latent, cache_k_rope, wukv_r, tp, ss)
    g_p, st_p = _gla(gq, gk, gv, lf, gate, jnp.zeros((bp,) + state_gla.shape[1:], F32), gla_norm_g,
                     0, bp, sp // CHUNK)
    g_s, st_s = _gla(gq, gk, gv, lf, gate, state_gla, gla_norm_g, tp, bs, ss // CHUNK)

    h, h8, h_inv, qp = _outproj(o_p, o_s, g_p, g_s, xp, xs, w_o.astype(BF16), ln1_g, ln1_b, peer_wq.astype(BF16))
    w_table = _route_scatter(qp, peer_keys.astype(BF16))
    y_p = _experts(h8, h_inv, h, w_table, u8, u_inv, v8, v_inv, ln2_g, ln2_b, 0, tp)
    y_s = _experts(h8, h_inv, h, w_table, u8, u_inv, v8, v_inv, ln2_g, ln2_b, tp, ts)

    dt = x_prompt.dtype
    return (y_p.reshape(bp, sp, D_MODEL), y_s.reshape(bs, ss, D_MODEL),
            ckv_p.reshape(bp, sp, KV_RANK), kr_p.reshape(bp, sp, ROPE_DIM), st_p.astype(dt),
            ckv_s.reshape(bs, ss, KV_RANK), kr_s.reshape(bs, ss, ROPE_DIM), st_s.astype(dt))
```

```python
import functools
import math

import jax
import jax.numpy as jnp
from jax import lax
from jax.experimental import pallas as pl
from jax.experimental.pallas import tpu as pltpu

F32 = jnp.float32
BF16 = jnp.bfloat16
FP8 = jnp.float8_e4m3fn
FP8_ROW_TILE = 32
FP8_TARGET = 128.0
FP8_TINY = 1e-30


def _fp8_scaled(x, amax):
    return (x * (FP8_TARGET / amax)).astype(FP8)

LANES = 128
SUBLANES = 8
VMEM_LIMIT_BYTES = 60 * 1024 * 1024

D_MODEL = 2048
CHUNK = 64
MLA_HEADS = 8
Q_RANK = 512
KV_RANK = 256
NOPE_DIM = 128
ROPE_DIM = 64
V_DIM = 128
ROPE_THETA = 10000.0
MLA_SCALE = (NOPE_DIM + ROPE_DIM) ** -0.5
Q_SCALE = MLA_SCALE * math.log2(math.e)
QK_PAD = 256
GLA_HEADS = 4
GLA_DK = 128
GLA_DV = 256
GATE_RANK = 16
GATE_NORMALIZER = 16.0
GLA_KW = GLA_HEADS * GLA_DK
GLA_WIDTH = GLA_HEADS * GLA_DV
MLA_WIDTH = MLA_HEADS * V_DIM
DN_ALPHA = 2.0 ** 0.25
LN_EPS = 1e-5
RMS_EPS = 1e-6

Z_CQ = 0
Z_CKV = Z_CQ + Q_RANK
Z_KR = Z_CKV + KV_RANK
Z_KROT = Z_KR + LANES
Z_GQ = Z_KROT + LANES
Z_GK = Z_GQ + GLA_KW
Z_GV = Z_GK + GLA_KW
Z_GATE = Z_GV + GLA_WIDTH
Z_WIDTH = Z_GATE + GLA_WIDTH


def _const_spec(shape):
    nd = len(shape)
    return pl.BlockSpec(shape, lambda *_: (0,) * nd, pipeline_mode=pl.Buffered(1))


def _rms(x, g):
    return x * lax.rsqrt(jnp.mean(x * x, axis=-1, keepdims=True) + RMS_EPS) * g


MIX_TM = 256


def _two_stream_specs(tm, width, n_first):
    return (pl.BlockSpec((tm, width), lambda i: (jnp.minimum(i, n_first - 1), 0)),
            pl.BlockSpec((tm, width), lambda i: (jnp.maximum(i - n_first, 0), 0)))


def _mixer_kernel(xp_ref, xs_ref, w_in_ref, wq_ref, wukv_ref, wvt_ref, wgk_ref, qg_ref, kvg_ref, bgk_ref,
                  cos_ref, sin_ref, q_ref, k_ref, v_ref, vt_ref, ckvp_ref, ckvs_ref, krp_ref, krs_ref,
                  gq_ref, gk_ref, gv_ref, gate_ref, lf_ref, *, n_prompt_tiles):
    in_prompt = pl.program_id(0) < n_prompt_tiles
    xb = jnp.where(in_prompt, xp_ref[...], xs_ref[...]).astype(BF16)
    cos = cos_ref[...]
    sin = sin_ref[...]

    z_lat = jnp.dot(xb, w_in_ref[:, Z_CQ:Z_GQ], preferred_element_type=F32)
    cq = _rms(z_lat[:, Z_CQ:Z_CKV], qg_ref[...])
    c_kv = _rms(z_lat[:, Z_CKV:Z_KR], kvg_ref[...])
    krga = z_lat[:, Z_KR:Z_KROT]
    k_rope = krga * cos + z_lat[:, Z_KROT:Z_GQ] * sin

    @pl.when(in_prompt)
    def _():
        ckvp_ref[...] = c_kv
        krp_ref[...] = k_rope[:, :ROPE_DIM]

    @pl.when(jnp.logical_not(in_prompt))
    def _():
        ckvs_ref[...] = c_kv
        krs_ref[...] = k_rope[:, :ROPE_DIM]

    k_rope_b = k_rope.astype(BF16)

    q = jnp.dot(cq.astype(BF16), wq_ref[...], preferred_element_type=F32)
    kv = jnp.dot(c_kv.astype(BF16), wukv_ref[...], preferred_element_type=F32)
    for h in range(MLA_HEADS):
        lo = h * QK_PAD
        q_ref[:, lo:lo + NOPE_DIM] = (q[:, lo:lo + NOPE_DIM] * Q_SCALE).astype(BF16)
        rot = q[:, MLA_HEADS * QK_PAD + h * LANES:MLA_HEADS * QK_PAD + (h + 1) * LANES]
        q_ref[:, lo + NOPE_DIM:lo + QK_PAD] = (
            (q[:, lo + NOPE_DIM:lo + QK_PAD] * cos + rot * sin) * Q_SCALE).astype(BF16)
        k_ref[:, lo:lo + NOPE_DIM] = kv[:, h * NOPE_DIM:(h + 1) * NOPE_DIM].astype(BF16)
        k_ref[:, lo + NOPE_DIM:lo + QK_PAD] = k_rope_b
    v_ref[...] = kv[:, MLA_HEADS * NOPE_DIM:].astype(BF16)
    vt_ref[...] = jnp.dot(wvt_ref[...], c_kv.T.astype(BF16), preferred_element_type=F32).astype(BF16)

    pre = jnp.dot(krga.astype(BF16), wgk_ref[...], preferred_element_type=F32) + bgk_ref[...]
    lf_ref[...] = (jnp.minimum(pre, 0.0) - jnp.log1p(jnp.exp(-jnp.abs(pre)))) * (1.0 / GATE_NORMALIZER)

    gq_ref[...] = jnp.dot(xb, w_in_ref[:, Z_GQ:Z_GK], preferred_element_type=F32)
    gk_ref[...] = jnp.dot(xb, w_in_ref[:, Z_GK:Z_GV], preferred_element_type=F32)
    gv_ref[...] = jnp.dot(xb, w_in_ref[:, Z_GV:Z_GATE], preferred_element_type=F32)
    gate_ref[...] = jnp.dot(xb, w_in_ref[:, Z_GATE:Z_WIDTH], preferred_element_type=F32)


def _rotate_half_cols(w):
    half = ROPE_DIM // 2
    return jnp.concatenate([-w[..., half:], w[..., :half]], axis=-1)


def _mixer_weights(w_in, w_uq, w_ukv, w_gk2):
    pts = []
    acc = 0
    for s in (Q_RANK, KV_RANK, ROPE_DIM, GLA_KW, GLA_KW, GLA_WIDTH, GLA_WIDTH):
        acc += s
        pts.append(acc)
    cq, ckv, kr, gq, gk, gv, gate, ga = jnp.split(w_in, pts, axis=1)
    d = w_in.shape[0]
    w_in_r = jnp.concatenate(
        [cq, ckv, kr, ga, jnp.zeros((d, LANES - ROPE_DIM - GATE_RANK), F32),
         _rotate_half_cols(kr), jnp.zeros((d, LANES - ROPE_DIM), F32), gq, gk, gv, gate], axis=1).astype(BF16)

    wq = w_uq.reshape(Q_RANK, MLA_HEADS, NOPE_DIM + ROPE_DIM)
    wq_rope = wq[..., NOPE_DIM:]
    pad = jnp.zeros((Q_RANK, MLA_HEADS, QK_PAD - NOPE_DIM - ROPE_DIM), F32)
    wq_main = jnp.concatenate([wq, pad], axis=-1).reshape(Q_RANK, MLA_HEADS * QK_PAD)
    wq_rot = jnp.concatenate([_rotate_half_cols(wq_rope), pad], axis=-1).reshape(Q_RANK, MLA_HEADS * LANES)
    wq_r = jnp.concatenate([wq_main, wq_rot], axis=1).astype(BF16)

    wkv = w_ukv.reshape(KV_RANK, MLA_HEADS, NOPE_DIM + V_DIM)
    wukv_r = jnp.concatenate([wkv[..., :NOPE_DIM].reshape(KV_RANK, -1),
                              wkv[..., NOPE_DIM:].reshape(KV_RANK, -1)], axis=1).astype(BF16)

    wvt_r = wkv[..., NOPE_DIM:].reshape(KV_RANK, -1).T.astype(BF16)

    wgk_r = jnp.zeros((LANES, GLA_KW), F32).at[ROPE_DIM:ROPE_DIM + GATE_RANK].set(w_gk2).astype(BF16)
    return w_in_r, wq_r, wukv_r, wvt_r, wgk_r


def _rope_tables(pos):
    half = ROPE_DIM // 2
    freqs = ROPE_THETA ** (-jnp.arange(half, dtype=F32) / half)
    ang = pos.astype(F32)[:, None] * freqs[None, :]
    zeros = jnp.zeros((pos.shape[0], LANES - ROPE_DIM), F32)
    cos = jnp.concatenate([jnp.cos(ang), jnp.cos(ang), zeros], axis=1)
    sin = jnp.concatenate([jnp.sin(ang), jnp.sin(ang), zeros], axis=1)
    return cos, sin


def _mixer(xp, xs, cos, sin, w_in_r, wq_r, wukv_r, wvt_r, wgk_r, q_norm_g, kv_norm_g, b_gk):
    tp, ts = xp.shape[0], xs.shape[0]
    t = tp + ts
    tm = MIX_TM
    assert tp % tm == 0 and ts % tm == 0
    n_p = tp // tm
    row = lambda w: pl.BlockSpec((tm, w), lambda i: (i, 0))
    col = pl.BlockSpec((MLA_WIDTH, tm), lambda i: (0, i))
    full = lambda w, dt: (row(w), jax.ShapeDtypeStruct((t, w), dt))
    ckv_p, ckv_s = _two_stream_specs(tm, KV_RANK, n_p)
    kr_p, kr_s = _two_stream_specs(tm, ROPE_DIM, n_p)
    outs = [full(MLA_HEADS * QK_PAD, BF16), full(MLA_HEADS * QK_PAD, BF16), full(MLA_WIDTH, BF16),
            (col, jax.ShapeDtypeStruct((MLA_WIDTH, t), BF16)),
            (ckv_p, jax.ShapeDtypeStruct((tp, KV_RANK), F32)), (ckv_s, jax.ShapeDtypeStruct((ts, KV_RANK), F32)),
            (kr_p, jax.ShapeDtypeStruct((tp, ROPE_DIM), F32)), (kr_s, jax.ShapeDtypeStruct((ts, ROPE_DIM), F32)),
            full(GLA_KW, F32), full(GLA_KW, F32), full(GLA_WIDTH, F32), full(GLA_WIDTH, F32), full(GLA_KW, F32)]
    return pl.pallas_call(
        functools.partial(_mixer_kernel, n_prompt_tiles=n_p),
        grid=(t // tm,),
        in_specs=[*_two_stream_specs(tm, D_MODEL, n_p), _const_spec(w_in_r.shape), _const_spec(wq_r.shape),
                  _const_spec(wukv_r.shape), _const_spec(wvt_r.shape), _const_spec(wgk_r.shape),
                  _const_spec((1, Q_RANK)), _const_spec((1, KV_RANK)), _const_spec((1, GLA_KW)),
                  row(LANES), row(LANES)],
        out_specs=[o[0] for o in outs],
        out_shape=[o[1] for o in outs],
        compiler_params=pltpu.CompilerParams(dimension_semantics=("arbitrary",),
                                             vmem_limit_bytes=VMEM_LIMIT_BYTES),
        name="mixer",
    )(xp, xs, w_in_r, wq_r, wukv_r, wvt_r, wgk_r, q_norm_g.reshape(1, -1), kv_norm_g.reshape(1, -1),
      b_gk.reshape(1, -1), cos, sin)


MLA_TQ = 512
MLA_LOOKAHEAD = 2
MLA_SIDE_ROWS = 128
NEG_BIG = -1e30
_NT = (((1,), (1,)), ((), ()))


def _mla_prompt_kernel(qi_ref, ki_ref, q_ref, k_ref, vt_ref, u_ref, v_ref, o_ref, u8_ref, uinv_ref, v8_ref, vinv_ref,
                       m_ref, l_ref, acc_ref, *, side_steps):
    step = pl.program_id(0)
    qi = qi_ref[step]
    ki = ki_ref[step]
    tq = q_ref.shape[0]
    tk = k_ref.shape[0]

    @pl.when(step < side_steps)
    def _():
        for src, dst, inv in ((u_ref, u8_ref, uinv_ref), (v_ref, v8_ref, vinv_ref)):
            x = src[...]
            amax = jnp.max(jnp.max(jnp.abs(x), axis=0, keepdims=True), axis=1, keepdims=True)
            amax = jnp.maximum(amax, FP8_TINY)
            dst[...] = _fp8_scaled(x, amax)
            inv[0] = jnp.broadcast_to(amax * (1.0 / FP8_TARGET), inv.shape[1:])

    @pl.when(ki == 0)
    def _():
        m_ref[...] = jnp.full(m_ref.shape, NEG_BIG, F32)
        l_ref[...] = jnp.zeros(l_ref.shape, F32)
        acc_ref[...] = jnp.zeros(acc_ref.shape, F32)

    def update(masked):
        if masked:
            key_chunk = lax.broadcasted_iota(jnp.int32, (tk, tq), 0) // CHUNK
            qry_chunk = lax.broadcasted_iota(jnp.int32, (tk, tq), 1) // CHUNK
            visible = key_chunk <= qry_chunk
        def scores(h):
            return lax.dot_general(k_ref[:, h * QK_PAD:(h + 1) * QK_PAD], q_ref[:, h * QK_PAD:(h + 1) * QK_PAD],
                                   _NT, preferred_element_type=F32)

        ahead = [scores(h) for h in range(MLA_LOOKAHEAD)]
        for h in range(MLA_HEADS):
            s = ahead.pop(0)
            if h + MLA_LOOKAHEAD < MLA_HEADS:
                ahead.append(scores(h + MLA_LOOKAHEAD))
            if masked:
                s = jnp.where(visible, s, NEG_BIG)
            m_old = m_ref[h]
            m_new = jnp.maximum(m_old, jnp.max(s, axis=0, keepdims=True))
            p = jnp.exp2(s - m_new)
            alpha = jnp.exp2(m_old - m_new)
            l_ref[h] = alpha * l_ref[h] + jnp.sum(p, axis=0, keepdims=True)
            acc_ref[h] = alpha * acc_ref[h] + jnp.dot(vt_ref[h * V_DIM:(h + 1) * V_DIM, :], p.astype(BF16),
                                                      preferred_element_type=F32)
            m_ref[h] = m_new

    @pl.when(ki < qi)
    def _():
        update(False)

    @pl.when(ki == qi)
    def _():
        update(True)
        for h in range(MLA_HEADS):
            o_ref[:, h * V_DIM:(h + 1) * V_DIM] = (acc_ref[h] / l_ref[h]).T.astype(BF16)


def _mla_prompt(q, k, vt, n_tokens, table_u, table_v):
    tq = MLA_TQ
    assert n_tokens % tq == 0
    nq = n_tokens // tq
    pairs = [(a, b) for a in range(nq) for b in range(a + 1)]
    qi = jnp.asarray([p[0] for p in pairs], jnp.int32)
    ki = jnp.asarray([p[1] for p in pairs], jnp.int32)
    rows, cols = table_u.shape
    assert table_v.shape == (rows, cols) and rows % MLA_SIDE_ROWS == 0
    side_steps = rows // MLA_SIDE_ROWS
    assert side_steps <= len(pairs)
    side = lambda: pl.BlockSpec((MLA_SIDE_ROWS, cols), lambda s, qi, ki: (jnp.minimum(s, side_steps - 1), 0))
    inv = lambda: pl.BlockSpec((1, SUBLANES, LANES), lambda s, qi, ki: (jnp.minimum(s, side_steps - 1), 0, 0))
    grid_spec = pltpu.PrefetchScalarGridSpec(
        num_scalar_prefetch=2,
        grid=(len(pairs),),
        in_specs=[pl.BlockSpec((tq, MLA_HEADS * QK_PAD), lambda s, qi, ki: (qi[s], 0)),
                  pl.BlockSpec((tq, MLA_HEADS * QK_PAD), lambda s, qi, ki: (ki[s], 0)),
                  pl.BlockSpec((MLA_WIDTH, tq), lambda s, qi, ki: (0, ki[s])), side(), side()],
        out_specs=[pl.BlockSpec((tq, MLA_WIDTH), lambda s, qi, ki: (qi[s], 0)), side(), inv(), side(), inv()],
        scratch_shapes=[pltpu.VMEM((MLA_HEADS, 1, tq), F32), pltpu.VMEM((MLA_HEADS, 1, tq), F32),
                        pltpu.VMEM((MLA_HEADS, V_DIM, tq), F32)],
    )
    table8 = jax.ShapeDtypeStruct((rows, cols), FP8)
    inv_shape = jax.ShapeDtypeStruct((side_steps, SUBLANES, LANES), F32)
    o, u8, u_inv, v8, v_inv = pl.pallas_call(
        functools.partial(_mla_prompt_kernel, side_steps=side_steps),
        grid_spec=grid_spec,
        out_shape=[jax.ShapeDtypeStruct((n_tokens, MLA_WIDTH), BF16), table8, inv_shape, table8, inv_shape],
        compiler_params=pltpu.CompilerParams(dimension_semantics=("arbitrary",),
                                             vmem_limit_bytes=VMEM_LIMIT_BYTES),
        name="mla_prompt",
    )(qi, ki, q, k, vt, table_u, table_v)
    return o, (u8, u_inv), (v8, v_inv)


def _mla_sample_kernel(q_ref, kn_ref, vn_ref, lat_ref, krc_ref, wukv_ref, o_ref):
    kvc = jnp.dot(lat_ref[0].astype(BF16), wukv_ref[...], preferred_element_type=F32)
    krc = krc_ref[0]
    for h in range(MLA_HEADS):
        q = q_ref[:, h * QK_PAD:(h + 1) * QK_PAD]
        knc = kvc[:, h * NOPE_DIM:(h + 1) * NOPE_DIM].astype(BF16)
        vc = kvc[:, MLA_HEADS * NOPE_DIM + h * V_DIM:MLA_HEADS * NOPE_DIM + (h + 1) * V_DIM].astype(BF16)
        s_c = (lax.dot_general(q[:, :NOPE_DIM], knc, _NT, preferred_element_type=F32)
               + lax.dot_general(q[:, NOPE_DIM:], krc, _NT, preferred_element_type=F32))
        s_n = lax.dot_general(q, kn_ref[:, h * QK_PAD:(h + 1) * QK_PAD], _NT, preferred_element_type=F32)
        m = jnp.maximum(jnp.max(s_c, axis=1, keepdims=True), jnp.max(s_n, axis=1, keepdims=True))
        p_c = jnp.exp2(s_c - m)
        p_n = jnp.exp2(s_n - m)
        l = jnp.sum(p_c, axis=1, keepdims=True) + jnp.sum(p_n, axis=1, keepdims=True)
        o = (jnp.dot(p_c.astype(BF16), vc, preferred_element_type=F32)
             + jnp.dot(p_n.astype(BF16), vn_ref[:, h * V_DIM:(h + 1) * V_DIM], preferred_element_type=F32))
        o_ref[:, h * V_DIM:(h + 1) * V_DIM] = (o / l).astype(BF16)


def _mla_sample(q, k, v, cache_lat, cache_kr, wukv_r, row0, n_new):
    bs, past, _ = cache_lat.shape
    assert row0 % n_new == 0
    blk0 = row0 // n_new
    krc = jnp.pad(cache_kr, ((0, 0), (0, 0), (0, LANES - ROPE_DIM))).astype(BF16)
    new = lambda w: pl.BlockSpec((n_new, w), lambda b: (blk0 + b, 0))
    return pl.pallas_call(
        _mla_sample_kernel,
        grid=(bs,),
        in_specs=[new(MLA_HEADS * QK_PAD), new(MLA_HEADS * QK_PAD), new(MLA_WIDTH),
                  pl.BlockSpec((1, past, KV_RANK), lambda b: (b, 0, 0)),
                  pl.BlockSpec((1, past, LANES), lambda b: (b, 0, 0)),
                  _const_spec(wukv_r.shape)],
        out_specs=pl.BlockSpec((n_new, MLA_WIDTH), lambda b: (b, 0)),
        out_shape=jax.ShapeDtypeStruct((bs * n_new, MLA_WIDTH), BF16),
        compiler_params=pltpu.CompilerParams(dimension_semantics=("arbitrary",),
                                             vmem_limit_bytes=VMEM_LIMIT_BYTES),
        name="mla_sample",
    )(q, k, v, cache_lat, krc, wukv_r)


GLA_CHUNKS_PER_STEP = 2
GLA_SUB = 16
_TN = (((0,), (0,)), ((), ()))


def _cumsum_rows(x):
    n = x.shape[0]
    row = lax.broadcasted_iota(jnp.int32, x.shape, 0)
    s = 1
    while s < n:
        x = x + jnp.where(row >= s, pltpu.roll(x, s, axis=0), 0.0)
        s *= 2
    return x


def _gla_kernel(q_ref, k_ref, v_ref, lf_ref, gate_ref, s0_ref, g_ref, o_ref, send_ref, st_ref, *, n_chunks):
    c = pl.program_id(1)
    n_rows = CHUNK
    n_sub = n_rows // GLA_SUB

    @pl.when(c == 0)
    def _():
        for h in range(GLA_HEADS):
            st_ref[h] = s0_ref[0, h].T

    sub_row = lax.broadcasted_iota(jnp.int32, (GLA_SUB, GLA_DK), 0)
    for h, ci in [(h, ci) for h in range(GLA_HEADS) for ci in range(q_ref.shape[0] // CHUNK)]:
        rows = slice(ci * CHUNK, (ci + 1) * CHUNK)
        ks = slice(h * GLA_DK, (h + 1) * GLA_DK)
        vs = slice(h * GLA_DV, (h + 1) * GLA_DV)
        q = q_ref[rows, ks] * (GLA_DK ** -0.5)
        k = k_ref[rows, ks]
        v = v_ref[rows, vs]
        vb = v.astype(BF16)
        b = _cumsum_rows(lf_ref[rows, ks])
        b_last = b[n_rows - 1:n_rows]
        st = st_ref[h]
        o_inter = lax.dot_general((q * jnp.exp(b)).astype(BF16), st.astype(BF16), _NT,
                                  preferred_element_type=F32)
        outs = []
        for i in range(n_sub):
            r0 = i * GLA_SUB
            bi = b[r0:r0 + GLA_SUB]
            qi = q[r0:r0 + GLA_SUB]
            o_i = o_inter[r0:r0 + GLA_SUB]
            if i > 0:
                b_ref = b[r0 - 1:r0]
                qh = (qi * jnp.exp(bi - b_ref)).astype(BF16)
                kh = (k[:r0] * jnp.exp(b_ref - b[:r0])).astype(BF16)
                a_off = lax.dot_general(qh, kh, _NT, preferred_element_type=F32)
                o_i = o_i + jnp.dot(a_off.astype(BF16), vb[:r0], preferred_element_type=F32)
            for j in range(GLA_SUB):
                r = r0 + j
                decay = jnp.exp(jnp.where(sub_row >= j, bi - b[r:r + 1], -jnp.inf))
                a_col = jnp.sum(qi * k[r:r + 1] * decay, axis=1, keepdims=True)
                o_i = o_i + a_col * v[r:r + 1]
            outs.append(o_i)
        o = jnp.concatenate(outs, axis=0)

        kd = (k * jnp.exp(b_last - b)).astype(BF16)
        st_new = st * jnp.exp(b_last) + lax.dot_general(vb, kd, _TN, preferred_element_type=F32)
        st_ref[h] = st_new

        gate = gate_ref[rows, vs]
        on = _rms(o, g_ref[...])
        o_ref[rows, vs] = (on * (gate / (1.0 + jnp.exp(-gate)))).astype(BF16)

    @pl.when(c == n_chunks - 1)
    def _():
        for h in range(GLA_HEADS):
            send_ref[0, h] = st_ref[h].T


def _gla(gq, gk, gv, lf, gate, s0, gla_norm_g, row0, n_seq, n_chunks):
    per_step = GLA_CHUNKS_PER_STEP if n_chunks % GLA_CHUNKS_PER_STEP == 0 else 1
    tr = per_step * CHUNK
    assert row0 % tr == 0
    blk0 = row0 // tr
    n_steps = n_chunks // per_step
    row = lambda w: pl.BlockSpec((tr, w), lambda b, c: (blk0 + b * n_steps + c, 0))
    state = pl.BlockSpec((1, GLA_HEADS, GLA_DK, GLA_DV), lambda b, c: (b, 0, 0, 0))
    return pl.pallas_call(
        functools.partial(_gla_kernel, n_chunks=n_steps),
        grid=(n_seq, n_steps),
        in_specs=[row(GLA_KW), row(GLA_KW), row(GLA_WIDTH), row(GLA_KW), row(GLA_WIDTH), state,
                  pl.BlockSpec((1, GLA_DV), lambda b, c: (0, 0))],
        out_specs=[pl.BlockSpec((tr, GLA_WIDTH), lambda b, c: (b * n_steps + c, 0)), state],
        out_shape=[jax.ShapeDtypeStruct((n_seq * n_chunks * CHUNK, GLA_WIDTH), BF16),
                   jax.ShapeDtypeStruct((n_seq, GLA_HEADS, GLA_DK, GLA_DV), F32)],
        scratch_shapes=[pltpu.VMEM((GLA_HEADS, GLA_DV, GLA_DK), F32)],
        compiler_params=pltpu.CompilerParams(dimension_semantics=("arbitrary", "arbitrary"),
                                             vmem_limit_bytes=VMEM_LIMIT_BYTES),
        name="gla",
    )(gq, gk, gv, lf, gate, s0, gla_norm_g.reshape(1, -1))


OUT_TM = 256


def _layer_norm(y, g, b):
    mu = jnp.mean(y, axis=-1, keepdims=True)
    yc = y - mu
    var = jnp.mean(yc * yc, axis=-1, keepdims=True)
    return yc * lax.rsqrt(var + LN_EPS) * g + b


def _outproj_kernel(mlap_ref, mlas_ref, glap_ref, glas_ref, xp_ref, xs_ref, wo_ref, g_ref, b_ref, wq_ref,
                    h_ref, h8_ref, hinv_ref, qp_ref, *, n_prompt_tiles):
    in_prompt = pl.program_id(0) < n_prompt_tiles
    mla = jnp.where(in_prompt, mlap_ref[...], mlas_ref[...])
    gla = jnp.where(in_prompt, glap_ref[...], glas_ref[...])
    x = jnp.where(in_prompt, xp_ref[...], xs_ref[...])
    mix = (jnp.dot(mla, wo_ref[:MLA_WIDTH], preferred_element_type=F32)
           + jnp.dot(gla, wo_ref[MLA_WIDTH:], preferred_element_type=F32))
    h = _layer_norm(DN_ALPHA * x + mix, g_ref[...], b_ref[...])
    h_ref[...] = h
    amax = jnp.maximum(jnp.max(jnp.abs(h), axis=1, keepdims=True), FP8_TINY)
    h8_ref[...] = _fp8_scaled(h, amax)
    hinv_ref[...] = jnp.broadcast_to(amax * (1.0 / FP8_TARGET), hinv_ref.shape)
    qp = jnp.dot(h.astype(BF16), wq_ref[...], preferred_element_type=F32).astype(BF16)
    per_head = qp_ref.shape[2]
    for head in range(qp_ref.shape[0]):
        qp_ref[head] = qp[:, head * per_head:(head + 1) * per_head]


def _outproj(mla_p, mla_s, gla_p, gla_s, xp, xs, w_o_b, ln1_g, ln1_b, peer_wq_b):
    tp, ts = xp.shape[0], xs.shape[0]
    t = tp + ts
    tm = OUT_TM
    assert tp % tm == 0 and ts % tm == 0
    n_p = tp // tm
    row = lambda w: pl.BlockSpec((tm, w), lambda i: (i, 0))
    return pl.pallas_call(
        functools.partial(_outproj_kernel, n_prompt_tiles=n_p),
        grid=(t // tm,),
        in_specs=[*_two_stream_specs(tm, MLA_WIDTH, n_p), *_two_stream_specs(tm, GLA_WIDTH, n_p),
                  *_two_stream_specs(tm, D_MODEL, n_p), _const_spec(w_o_b.shape),
                  _const_spec((1, D_MODEL)), _const_spec((1, D_MODEL)), _const_spec(peer_wq_b.shape)],
        out_specs=[row(D_MODEL), row(D_MODEL), row(LANES),
                   pl.BlockSpec((PEER_HEADS, tm, 2 * PEER_HALF), lambda i: (0, i, 0))],
        out_shape=[jax.ShapeDtypeStruct((t, D_MODEL), F32), jax.ShapeDtypeStruct((t, D_MODEL), FP8),
                   jax.ShapeDtypeStruct((t, LANES), F32),
                   jax.ShapeDtypeStruct((PEER_HEADS, t, 2 * PEER_HALF), BF16)],
        compiler_params=pltpu.CompilerParams(dimension_semantics=("arbitrary",),
                                             vmem_limit_bytes=VMEM_LIMIT_BYTES),
        name="outproj",
    )(mla_p, mla_s, gla_p, gla_s, xp, xs, w_o_b, ln1_g.reshape(1, -1), ln1_b.reshape(1, -1), peer_wq_b)


PEER_HEADS = 8
N_KEYS = 128
PEER_HALF = 128
PEER_TOPK = 16
CAND_WIDE_RANKS = 8
CAND_ROWS = PEER_TOPK + (CAND_WIDE_RANKS - 1) * SUBLANES + (PEER_TOPK - CAND_WIDE_RANKS)


def _first_max(x, row):
    n = x.shape[0]
    blocks = [(x[r:r + SUBLANES], row[r:r + SUBLANES]) for r in range(0, n, SUBLANES)]
    while len(blocks) > 1:
        merged = []
        for j in range(0, len(blocks) - 1, 2):
            (va, ia), (vb, ib) = blocks[j], blocks[j + 1]
            take = va >= vb
            merged.append((jnp.where(take, va, vb), jnp.where(take, ia, ib)))
        if len(blocks) % 2:
            merged.append(blocks[-1])
        blocks = merged
    v, i = blocks[0]
    m = jnp.max(v, axis=0, keepdims=True)
    pick = jnp.min(jnp.where(v == m, i, float(n)), axis=0, keepdims=True)
    return m, pick


def _extract_top(xs, row, n_out, on_pick):
    xs = list(xs)
    for k in range(n_out):
        for p, x in enumerate(xs):
            m, pick = _first_max(x, row)
            hit = row == pick
            on_pick(p, k, m, hit, pick)
            xs[p] = jnp.where(hit, -jnp.inf, x)


def _route_head(qp_ref, keys_ref, row_ref, sv_ref, si_ref, gt_ref, et_ref, h):
    tm = qp_ref.shape[1]
    q = qp_ref[h]
    scores = [lax.dot_general(keys_ref[half], q[:, half * PEER_HALF:(half + 1) * PEER_HALF], _NT,
                              preferred_element_type=F32) for half in range(2)]

    def keep(half, k, m, hit, pick):
        sv_ref[half, k:k + 1, :] = m
        si_ref[half, k:k + 1, :] = pick

    _extract_top(scores, row_ref[...], PEER_TOPK, keep)

    s1, s2 = sv_ref[0], sv_ref[1]
    i1, i2 = si_ref[0], si_ref[1]
    bc = lambda r, n: jnp.broadcast_to(r, (n, tm))
    wide = range(1, CAND_WIDE_RANKS)
    cand = jnp.concatenate([bc(s1[0:1], PEER_TOPK) + s2]
                           + [bc(s1[a:a + 1], SUBLANES) + s2[:SUBLANES] for a in wide]
                           + [s1[CAND_WIDE_RANKS:] + bc(s2[0:1], PEER_TOPK - CAND_WIDE_RANKS)], axis=0)
    i1 = i1 * N_KEYS
    c_expert = jnp.concatenate([bc(i1[0:1], PEER_TOPK) + i2]
                               + [bc(i1[a:a + 1], SUBLANES) + i2[:SUBLANES] for a in wide]
                               + [i1[CAND_WIDE_RANKS:] + bc(i2[0:1], PEER_TOPK - CAND_WIDE_RANKS)], axis=0)
    base = h * PEER_TOPK

    def keep_pair(_, k, m, hit, pick):
        gt_ref[base + k:base + k + 1, :] = m
        et_ref[base + k:base + k + 1, :] = jnp.sum(jnp.where(hit, c_expert, 0.0), axis=0, keepdims=True)

    _extract_top([cand], row_ref[:CAND_ROWS], PEER_TOPK, keep_pair)

    sc = gt_ref[base:base + PEER_TOPK, :]
    e = jnp.exp(sc - sc[0:1])
    gt_ref[base:base + PEER_TOPK, :] = e / jnp.sum(e, axis=0, keepdims=True)


RS_TM = 128
W_ROW_TILE = SUBLANES


def _w_table_tokens(t):
    return -(-t // EXP_TM) * EXP_TM


def _route_scatter_kernel(qp_ref, keys_ref, row_ref, w_ref, sv_ref, si_ref, gt_ref, et_ref, sel_ref, *, n_tiles):
    i = pl.program_id(0)
    tm = qp_ref.shape[1]

    @pl.when(i == 0)
    def _():
        sel_ref[...] = jnp.zeros(sel_ref.shape, F32)

    @pl.when(i > n_tiles)
    def _():
        w_ref[...] = jnp.zeros(w_ref.shape, F32)

    @pl.when(i <= n_tiles)
    def _():
        cur = i % 2
        prev = 1 - cur
        key = lax.broadcasted_iota(jnp.int32, (N_KEYS, N_KEYS), 0).astype(F32)
        zero = jnp.zeros(key.shape, BF16)
        per_head = tm // PEER_HEADS
        for h in range(PEER_HEADS):
            _route_head(qp_ref, keys_ref, row_ref, sv_ref, si_ref, gt_ref, et_ref, h)
            t0 = h * per_head
            g8 = sel_ref[prev, 0, t0:t0 + per_head, :]
            i18 = sel_ref[prev, 1, t0:t0 + per_head, :]
            i28 = sel_ref[prev, 2, t0:t0 + per_head, :]
            for u in range(0, per_head, 2):
                a_ts, b_ts = [], []
                for v in (u, u + 1):
                    g = jnp.broadcast_to(g8[v:v + 1], key.shape)
                    i1 = jnp.broadcast_to(i18[v:v + 1], key.shape)
                    i2 = jnp.broadcast_to(i28[v:v + 1], key.shape)
                    a_ts.append(jnp.where(key == i1, g, 0.0).astype(BF16))
                    b_ts.append(jnp.where(key == i2, 1.0, 0.0).astype(BF16))
                a_pair = jnp.concatenate(a_ts, axis=1)
                b_pair = jnp.concatenate([jnp.concatenate([b_ts[0], zero], axis=1),
                                          jnp.concatenate([zero, b_ts[1]], axis=1)], axis=0)
                w = lax.dot_general(a_pair, b_pair, _NT, preferred_element_type=F32)
                for n, v in enumerate((u, u + 1)):
                    w_ref[:, t0 + v, :, :] = w[:, n * N_KEYS:(n + 1) * N_KEYS].reshape(
                        N_KEYS // W_ROW_TILE, W_ROW_TILE, N_KEYS)
        sel_ref[cur, 0] = gt_ref[...].T
        expert = et_ref[...]
        key0 = jnp.floor(expert * (1.0 / N_KEYS))
        sel_ref[cur, 1] = key0.T
        sel_ref[cur, 2] = (expert - key0 * N_KEYS).T


def _route_scatter(qp, keys_b):
    t = qp.shape[1]
    tm = RS_TM
    t_pad = _w_table_tokens(t)
    assert t % tm == 0 and t_pad % tm == 0
    n_tiles = t // tm
    n_sel = PEER_HEADS * PEER_TOPK
    n_grp = N_KEYS // W_ROW_TILE
    row_index = jnp.broadcast_to(jnp.arange(N_KEYS, dtype=F32)[:, None], (N_KEYS, tm))
    return pl.pallas_call(
        functools.partial(_route_scatter_kernel, n_tiles=n_tiles),
        grid=(t_pad // tm + 1,),
        in_specs=[pl.BlockSpec((PEER_HEADS, tm, 2 * PEER_HALF), lambda i: (0, jnp.minimum(i, n_tiles - 1), 0)),
                  pl.BlockSpec(keys_b.shape, lambda i: (0, 0, 0)),
                  pl.BlockSpec((N_KEYS, tm), lambda i: (0, 0))],
        out_specs=pl.BlockSpec((n_grp, tm, W_ROW_TILE, N_KEYS), lambda i: (0, jnp.maximum(i - 1, 0), 0, 0)),
        out_shape=jax.ShapeDtypeStruct((n_grp, t_pad, W_ROW_TILE, N_KEYS), F32),
        scratch_shapes=[pltpu.VMEM((2, PEER_TOPK, tm), F32), pltpu.VMEM((2, PEER_TOPK, tm), F32),
                        pltpu.VMEM((n_sel, tm), F32), pltpu.VMEM((n_sel, tm), F32),
                        pltpu.VMEM((2, 3, tm, n_sel), F32)],
        compiler_params=pltpu.CompilerParams(dimension_semantics=("arbitrary",),
                                             vmem_limit_bytes=VMEM_LIMIT_BYTES),
        name="peer_route_scatter",
    )(qp, keys_b, row_index)


EXP_TM = 1024
EXP_ROWS = W_ROW_TILE
EXP_TE = EXP_ROWS * N_KEYS
EXP_SUBTILES = 4
GELU_C = 2.0 ** -0.5


def _experts_kernel(h8_ref, hinv_ref, w_ref, u_ref, uinv_ref, v_ref, vinv_ref, h_ref, g_ref, b_ref, y_ref):
    j = pl.program_id(1)
    tm = h8_ref.shape[0]

    @pl.when(j == 0)
    def _():
        y_ref[...] = DN_ALPHA * h_ref[...]

    sub = tm // EXP_SUBTILES
    u_inv = jnp.concatenate([uinv_ref[r, 0:1, :] for r in range(EXP_ROWS)], axis=1)
    half_v_inv = 0.5 * jnp.concatenate([vinv_ref[r, 0:1, :] for r in range(EXP_ROWS)], axis=1)
    acts = []
    for s in range(EXP_SUBTILES):
        h_inv = hinv_ref[s * sub:(s + 1) * sub, :]
        a8 = lax.dot_general(h8_ref[s * sub:(s + 1) * sub, :], u_ref[...], _NT, preferred_element_type=F32)
        acts.append(a8 * jnp.concatenate([h_inv] * EXP_ROWS, axis=1) * u_inv)
    for s in range(EXP_SUBTILES):
        w = jnp.concatenate([w_ref[pl.ds(s * sub * W_ROW_TILE + r, sub, stride=W_ROW_TILE), :]
                             for r in range(EXP_ROWS)], axis=1)
        a = acts[s]
        p = w * (a * (1.0 + lax.erf(a * GELU_C))) * half_v_inv
        pmax = jnp.maximum(jnp.max(jnp.abs(p), axis=1, keepdims=True), FP8_TINY)
        out = jnp.dot(_fp8_scaled(p, pmax), v_ref[...], preferred_element_type=F32)
        y_ref[s * sub:(s + 1) * sub, :] += out * (pmax * (1.0 / FP8_TARGET))

    @pl.when(j == pl.num_programs(1) - 1)
    def _():
        y_ref[...] = _layer_norm(y_ref[...], g_ref[...], b_ref[...])


def _experts(h8, h_inv, h, w_table, u8, u_inv, v8, v_inv, ln2_g, ln2_b, row0, n_rows):
    tm = min(EXP_TM, n_rows)
    assert n_rows % tm == 0 and row0 % tm == 0 and tm % (EXP_SUBTILES * FP8_ROW_TILE) == 0
    nt = n_rows // tm
    blk0 = row0 // tm
    ne = u8.shape[0] // EXP_TE
    tiles_per_group = w_table.shape[1] // tm
    w2d = w_table.reshape(-1, N_KEYS)
    once = lambda w: pl.BlockSpec((tm, w), lambda i, j: (blk0 + i, 0), pipeline_mode=pl.Buffered(1))
    return pl.pallas_call(
        _experts_kernel,
        grid=(nt, ne),
        in_specs=[once(D_MODEL), once(LANES),
                  pl.BlockSpec((tm * W_ROW_TILE, N_KEYS), lambda i, j: (j * tiles_per_group + blk0 + i, 0)),
                  pl.BlockSpec((EXP_TE, D_MODEL), lambda i, j: (j, 0)),
                  pl.BlockSpec((EXP_ROWS, SUBLANES, LANES), lambda i, j: (j, 0, 0)),
                  pl.BlockSpec((EXP_TE, D_MODEL), lambda i, j: (j, 0)),
                  pl.BlockSpec((EXP_ROWS, SUBLANES, LANES), lambda i, j: (j, 0, 0)),
                  once(D_MODEL), pl.BlockSpec((1, D_MODEL), lambda i, j: (0, 0)),
                  pl.BlockSpec((1, D_MODEL), lambda i, j: (0, 0))],
        out_specs=pl.BlockSpec((tm, D_MODEL), lambda i, j: (i, 0)),
        out_shape=jax.ShapeDtypeStruct((n_rows, D_MODEL), F32),
        compiler_params=pltpu.CompilerParams(dimension_semantics=("arbitrary", "arbitrary"),
                                             vmem_limit_bytes=VMEM_LIMIT_BYTES),
        name="peer_experts",
    )(h8, h_inv, w2d, u8, u_inv, v8, v_inv, h, ln2_g.reshape(1, -1), ln2_b.reshape(1, -1))


def kernel(x_prompt, x_sample, cache_kv_latent, cache_k_rope, state_gla, w_in, q_norm_g, w_uq, kv_norm_g, w_ukv,
           w_gk2, b_gk, gla_norm_g, w_o, ln1_g, ln1_b, peer_wq, peer_keys, peer_u, peer_v, ln2_g, ln2_b):
    bp, sp, _ = x_prompt.shape
    bs, ss, _ = x_sample.shape
    past = cache_kv_latent.shape[1]
    tp, ts = bp * sp, bs * ss
    xp = x_prompt.reshape(tp, D_MODEL)
    xs = x_sample.reshape(ts, D_MODEL)
    cos_p, sin_p = _rope_tables(jnp.arange(sp, dtype=jnp.int32))
    cos_s, sin_s = _rope_tables(past + jnp.arange(ss, dtype=jnp.int32))
    cos = jnp.concatenate([jnp.tile(cos_p, (bp, 1)), jnp.tile(cos_s, (bs, 1))])
    sin = jnp.concatenate([jnp.tile(sin_p, (bp, 1)), jnp.tile(sin_s, (bs, 1))])
    w_in_r, wq_r, wukv_r, wvt_r, wgk_r = _mixer_weights(w_in, w_uq, w_ukv, w_gk2)
    q, k, v, vt, ckv_p, ckv_s, kr_p, kr_s, gq, gk, gv, gate, lf = _mixer(
        xp, xs, cos, sin, w_in_r, wq_r, wukv_r, wvt_r, wgk_r, q_norm_g, kv_norm_g, b_gk)
    o_p, (u8, u_inv), (v8, v_inv) = _mla_prompt(q, k, vt, tp, peer_u, peer_v)
    o_s = _mla_sample(q, k, v, cache_kv_latent, cache_k_rope, wukv_r, tp, ss)
    g_p, st_p = _gla(gq, gk, gv, lf, gate, jnp.zeros((bp,) + state_gla.shape[1:], F32), gla_norm_g,
                     0, bp, sp // CHUNK)
    g_s, st_s = _gla(gq, gk, gv, lf, gate, state_gla, gla_norm_g, tp, bs, ss // CHUNK)

    h, h8, h_inv, qp = _outproj(o_p, o_s, g_p, g_s, xp, xs, w_o.astype(BF16), ln1_g, ln1_b, peer_wq.astype(BF16))
    w_table = _route_scatter(qp, peer_keys.astype(BF16))
    y_p = _experts(h8, h_inv, h, w_table, u8, u_inv, v8, v_inv, ln2_g, ln2_b, 0, tp)
    y_s = _experts(h8, h_inv, h, w_table, u8, u_inv, v8, v_inv, ln2_g, ln2_b, tp, ts)

    dt = x_prompt.dtype
    return (y_p.reshape(bp, sp, D_MODEL), y_s.reshape(bs, ss, D_MODEL),
            ckv_p.reshape(bp, sp, KV_RANK), kr_p.reshape(bp, sp, ROPE_DIM), st_p.astype(dt),
            ckv_s.reshape(bs, ss, KV_RANK), kr_s.reshape(bs, ss, ROPE_DIM), st_s.astype(dt))
```

```python
import functools
import math

import jax
import jax.numpy as jnp
from jax import lax
from jax.experimental import pallas as pl
from jax.experimental.pallas import tpu as pltpu

F32 = jnp.float32
BF16 = jnp.bfloat16
FP8 = jnp.float8_e4m3fn
FP8_ROW_TILE = 32
FP8_TARGET = 128.0
FP8_TINY = 1e-30


def _fp8_scaled(x, amax):
    return (x * (FP8_TARGET / amax)).astype(FP8)

LANES = 128
SUBLANES = 8
VMEM_LIMIT_BYTES = 60 * 1024 * 1024

D_MODEL = 2048
CHUNK = 64
MLA_HEADS = 8
Q_RANK = 512
KV_RANK = 256
NOPE_DIM = 128
ROPE_DIM = 64
V_DIM = 128
ROPE_THETA = 10000.0
MLA_SCALE = (NOPE_DIM + ROPE_DIM) ** -0.5
Q_SCALE = MLA_SCALE * math.log2(math.e)
QK_PAD = 256
GLA_HEADS = 4
GLA_DK = 128
GLA_DV = 256
GATE_RANK = 16
GATE_NORMALIZER = 16.0
GLA_KW = GLA_HEADS * GLA_DK
GLA_WIDTH = GLA_HEADS * GLA_DV
MLA_WIDTH = MLA_HEADS * V_DIM
DN_ALPHA = 2.0 ** 0.25
LN_EPS = 1e-5
RMS_EPS = 1e-6

Z_CQ = 0
Z_CKV = Z_CQ + Q_RANK
Z_KR = Z_CKV + KV_RANK
Z_KROT = Z_KR + LANES
Z_GQ = Z_KROT + LANES
Z_GK = Z_GQ + GLA_KW
Z_GV = Z_GK + GLA_KW
Z_GATE = Z_GV + GLA_WIDTH
Z_WIDTH = Z_GATE + GLA_WIDTH


def _const_spec(shape):
    nd = len(shape)
    return pl.BlockSpec(shape, lambda *_: (0,) * nd, pipeline_mode=pl.Buffered(1))


def _rms(x, g):
    return x * lax.rsqrt(jnp.mean(x * x, axis=-1, keepdims=True) + RMS_EPS) * g


MIX_TM = 256


def _two_stream_specs(tm, width, n_first):
    return (pl.BlockSpec((tm, width), lambda i: (jnp.minimum(i, n_first - 1), 0)),
            pl.BlockSpec((tm, width), lambda i: (jnp.maximum(i - n_first, 0), 0)))


def _mixer_kernel(xp_ref, xs_ref, w_in_ref, wq_ref, wukv_ref, wvt_ref, wgk_ref, qg_ref, kvg_ref, bgk_ref,
                  cos_ref, sin_ref, q_ref, k_ref, v_ref, vt_ref, ckvp_ref, ckvs_ref, krp_ref, krs_ref,
                  gq_ref, gk_ref, gv_ref, gate_ref, lf_ref, *, n_prompt_tiles):
    in_prompt = pl.program_id(0) < n_prompt_tiles
    xb = jnp.where(in_prompt, xp_ref[...], xs_ref[...]).astype(BF16)
    cos = cos_ref[...]
    sin = sin_ref[...]

    z_lat = jnp.dot(xb, w_in_ref[:, Z_CQ:Z_GQ], preferred_element_type=F32)
    cq = _rms(z_lat[:, Z_CQ:Z_CKV], qg_ref[...])
    c_kv = _rms(z_lat[:, Z_CKV:Z_KR], kvg_ref[...])
    krga = z_lat[:, Z_KR:Z_KROT]
    k_rope = krga * cos + z_lat[:, Z_KROT:Z_GQ] * sin

    @pl.when(in_prompt)
    def _():
        ckvp_ref[...] = c_kv
        krp_ref[...] = k_rope[:, :ROPE_DIM]

    @pl.when(jnp.logical_not(in_prompt))
    def _():
        ckvs_ref[...] = c_kv
        krs_ref[...] = k_rope[:, :ROPE_DIM]

    k_rope_b = k_rope.astype(BF16)

    q = jnp.dot(cq.astype(BF16), wq_ref[...], preferred_element_type=F32)
    kv = jnp.dot(c_kv.astype(BF16), wukv_ref[...], preferred_element_type=F32)
    for h in range(MLA_HEADS):
        lo = h * QK_PAD
        q_ref[:, lo:lo + NOPE_DIM] = (q[:, lo:lo + NOPE_DIM] * Q_SCALE).astype(BF16)
        rot = q[:, MLA_HEADS * QK_PAD + h * LANES:MLA_HEADS * QK_PAD + (h + 1) * LANES]
        q_ref[:, lo + NOPE_DIM:lo + QK_PAD] = (
            (q[:, lo + NOPE_DIM:lo + QK_PAD] * cos + rot * sin) * Q_SCALE).astype(BF16)
        k_ref[:, lo:lo + NOPE_DIM] = kv[:, h * NOPE_DIM:(h + 1) * NOPE_DIM].astype(BF16)
        k_ref[:, lo + NOPE_DIM:lo + QK_PAD] = k_rope_b
    v_ref[...] = kv[:, MLA_HEADS * NOPE_DIM:].astype(BF16)
    vt_ref[...] = jnp.dot(wvt_ref[...], c_kv.T.astype(BF16), preferred_element_type=F32).astype(BF16)

    pre = jnp.dot(krga.astype(BF16), wgk_ref[...], preferred_element_type=F32) + bgk_ref[...]
    lf_ref[...] = (jnp.minimum(pre, 0.0) - jnp.log1p(jnp.exp(-jnp.abs(pre)))) * (1.0 / GATE_NORMALIZER)

    gq_ref[...] = jnp.dot(xb, w_in_ref[:, Z_GQ:Z_GK], preferred_element_type=F32)
    gk_ref[...] = jnp.dot(xb, w_in_ref[:, Z_GK:Z_GV], preferred_element_type=F32)
    gv_ref[...] = jnp.dot(xb, w_in_ref[:, Z_GV:Z_GATE], preferred_element_type=F32)
    gate_ref[...] = jnp.dot(xb, w_in_ref[:, Z_GATE:Z_WIDTH], preferred_element_type=F32)


def _rotate_half_cols(w):
    half = ROPE_DIM // 2
    return jnp.concatenate([-w[..., half:], w[..., :half]], axis=-1)


def _mixer_weights(w_in, w_uq, w_ukv, w_gk2):
    pts = []
    acc = 0
    for s in (Q_RANK, KV_RANK, ROPE_DIM, GLA_KW, GLA_KW, GLA_WIDTH, GLA_WIDTH):
        acc += s
        pts.append(acc)
    cq, ckv, kr, gq, gk, gv, gate, ga = jnp.split(w_in, pts, axis=1)
    d = w_in.shape[0]
    w_in_r = jnp.concatenate(
        [cq, ckv, kr, ga, jnp.zeros((d, LANES - ROPE_DIM - GATE_RANK), F32),
         _rotate_half_cols(kr), jnp.zeros((d, LANES - ROPE_DIM), F32), gq, gk, gv, gate], axis=1).astype(BF16)

    wq = w_uq.reshape(Q_RANK, MLA_HEADS, NOPE_DIM + ROPE_DIM)
    wq_rope = wq[..., NOPE_DIM:]
    pad = jnp.zeros((Q_RANK, MLA_HEADS, QK_PAD - NOPE_DIM - ROPE_DIM), F32)
    wq_main = jnp.concatenate([wq, pad], axis=-1).reshape(Q_RANK, MLA_HEADS * QK_PAD)
    wq_rot = jnp.concatenate([_rotate_half_cols(wq_rope), pad], axis=-1).reshape(Q_RANK, MLA_HEADS * LANES)
    wq_r = jnp.concatenate([wq_main, wq_rot], axis=1).astype(BF16)

    wkv = w_ukv.reshape(KV_RANK, MLA_HEADS, NOPE_DIM + V_DIM)
    wukv_r = jnp.concatenate([wkv[..., :NOPE_DIM].reshape(KV_RANK, -1),
                              wkv[..., NOPE_DIM:].reshape(KV_RANK, -1)], axis=1).astype(BF16)

    wvt_r = wkv[..., NOPE_DIM:].reshape(KV_RANK, -1).T.astype(BF16)

    wgk_r = jnp.zeros((LANES, GLA_KW), F32).at[ROPE_DIM:ROPE_DIM + GATE_RANK].set(w_gk2).astype(BF16)
    return w_in_r, wq_r, wukv_r, wvt_r, wgk_r


def _rope_tables(pos):
    half = ROPE_DIM // 2
    freqs = ROPE_THETA ** (-jnp.arange(half, dtype=F32) / half)
    ang = pos.astype(F32)[:, None] * freqs[None, :]
    zeros = jnp.zeros((pos.shape[0], LANES - ROPE_DIM), F32)
    cos = jnp.concatenate([jnp.cos(ang), jnp.cos(ang), zeros], axis=1)
    sin = jnp.concatenate([jnp.sin(ang), jnp.sin(ang), zeros], axis=1)
    return cos, sin


def _mixer(xp, xs, cos, sin, w_in_r, wq_r, wukv_r, wvt_r, wgk_r, q_norm_g, kv_norm_g, b_gk):
    tp, ts = xp.shape[0], xs.shape[0]
    t = tp + ts
    tm = MIX_TM
    assert tp % tm == 0 and ts % tm == 0
    n_p = tp // tm
    row = lambda w: pl.BlockSpec((tm, w), lambda i: (i, 0))
    col = pl.BlockSpec((MLA_WIDTH, tm), lambda i: (0, i))
    full = lambda w, dt: (row(w), jax.ShapeDtypeStruct((t, w), dt))
    ckv_p, ckv_s = _two_stream_specs(tm, KV_RANK, n_p)
    kr_p, kr_s = _two_stream_specs(tm, ROPE_DIM, n_p)
    outs = [full(MLA_HEADS * QK_PAD, BF16), full(MLA_HEADS * QK_PAD, BF16), full(MLA_WIDTH, BF16),
            (col, jax.ShapeDtypeStruct((MLA_WIDTH, t), BF16)),
            (ckv_p, jax.ShapeDtypeStruct((tp, KV_RANK), F32)), (ckv_s, jax.ShapeDtypeStruct((ts, KV_RANK), F32)),
            (kr_p, jax.ShapeDtypeStruct((tp, ROPE_DIM), F32)), (kr_s, jax.ShapeDtypeStruct((ts, ROPE_DIM), F32)),
            full(GLA_KW, F32), full(GLA_KW, F32), full(GLA_WIDTH, F32), full(GLA_WIDTH, F32), full(GLA_KW, F32)]
    return pl.pallas_call(
        functools.partial(_mixer_kernel, n_prompt_tiles=n_p),
        grid=(t // tm,),
        in_specs=[*_two_stream_specs(tm, D_MODEL, n_p), _const_spec(w_in_r.shape), _const_spec(wq_r.shape),
                  _const_spec(wukv_r.shape), _const_spec(wvt_r.shape), _const_spec(wgk_r.shape),
                  _const_spec((1, Q_RANK)), _const_spec((1, KV_RANK)), _const_spec((1, GLA_KW)),
                  row(LANES), row(LANES)],
        out_specs=[o[0] for o in outs],
        out_shape=[o[1] for o in outs],
        compiler_params=pltpu.CompilerParams(dimension_semantics=("arbitrary",),
                                             vmem_limit_bytes=VMEM_LIMIT_BYTES),
        name="mixer",
    )(xp, xs, w_in_r, wq_r, wukv_r, wvt_r, wgk_r, q_norm_g.reshape(1, -1), kv_norm_g.reshape(1, -1),
      b_gk.reshape(1, -1), cos, sin)


MLA_TQ = 512
MLA_TK = 512
MLA_LOOKAHEAD = 2
MLA_SIDE_ROWS = 128
SIDE_BLOCK = 128
NEG_BIG = -1e30
_NT = (((1,), (1,)), ((), ()))


def _mla_prompt_kernel(qi_ref, ki_ref, q_ref, k_ref, vt_ref, u_ref, v_ref, o_ref, u8_ref, uinv_ref, v8_ref, vinv_ref,
                       m_ref, l_ref, acc_ref, *, side_steps):
    step = pl.program_id(0)
    qi = qi_ref[step]
    ki = ki_ref[step]
    tq = q_ref.shape[0]
    tk = k_ref.shape[0]

    @pl.when(step < side_steps)
    def _():
        for src, dst, inv in ((u_ref, u8_ref, uinv_ref), (v_ref, v8_ref, vinv_ref)):
            for blk in range(inv.shape[0]):
                rows = slice(blk * SIDE_BLOCK, (blk + 1) * SIDE_BLOCK)
                x = src[rows, :]
                amax = jnp.max(jnp.max(jnp.abs(x), axis=0, keepdims=True), axis=1, keepdims=True)
                amax = jnp.maximum(amax, FP8_TINY)
                dst[rows, :] = _fp8_scaled(x, amax)
                inv[blk] = jnp.broadcast_to(amax * (1.0 / FP8_TARGET), inv.shape[1:])

    @pl.when(ki == 0)
    def _():
        m_ref[...] = jnp.full(m_ref.shape, NEG_BIG, F32)
        l_ref[...] = jnp.zeros(l_ref.shape, F32)
        acc_ref[...] = jnp.zeros(acc_ref.shape, F32)

    k_per_q = tq // tk
    first_masked = qi * k_per_q

    def update(masked):
        if masked:
            key_chunk = lax.broadcasted_iota(jnp.int32, (tk, tq), 0) // CHUNK + ki * (tk // CHUNK)
            qry_chunk = lax.broadcasted_iota(jnp.int32, (tk, tq), 1) // CHUNK + qi * (tq // CHUNK)
            visible = key_chunk <= qry_chunk
        def scores(h):
            return lax.dot_general(k_ref[:, h * QK_PAD:(h + 1) * QK_PAD], q_ref[:, h * QK_PAD:(h + 1) * QK_PAD],
                                   _NT, preferred_element_type=F32)

        ahead = [scores(h) for h in range(MLA_LOOKAHEAD)]
        for h in range(MLA_HEADS):
            s = ahead.pop(0)
            if h + MLA_LOOKAHEAD < MLA_HEADS:
                ahead.append(scores(h + MLA_LOOKAHEAD))
            if masked:
                s = jnp.where(visible, s, NEG_BIG)
            m_old = m_ref[h]
            m_new = jnp.maximum(m_old, jnp.max(s, axis=0, keepdims=True))
            p = jnp.exp2(s - m_new)
            alpha = jnp.exp2(m_old - m_new)
            l_ref[h] = alpha * l_ref[h] + jnp.sum(p, axis=0, keepdims=True)
            acc_ref[h] = alpha * acc_ref[h] + jnp.dot(vt_ref[h * V_DIM:(h + 1) * V_DIM, :], p.astype(BF16),
                                                      preferred_element_type=F32)
            m_ref[h] = m_new

    @pl.when(ki < first_masked)
    def _():
        update(False)

    @pl.when(ki >= first_masked)
    def _():
        update(True)

    @pl.when(ki == first_masked + k_per_q - 1)
    def _():
        for h in range(MLA_HEADS):
            o_ref[:, h * V_DIM:(h + 1) * V_DIM] = (acc_ref[h] / l_ref[h]).T.astype(BF16)


def _mla_prompt(q, k, vt, n_tokens, table_u, table_v):
    tq, tk = MLA_TQ, MLA_TK
    assert n_tokens % tq == 0 and tq % tk == 0 and tk % CHUNK == 0
    nq = n_tokens // tq
    pairs = [(a, b) for a in range(nq) for b in range((a + 1) * (tq // tk))]
    qi = jnp.asarray([p[0] for p in pairs], jnp.int32)
    ki = jnp.asarray([p[1] for p in pairs], jnp.int32)
    rows, cols = table_u.shape
    assert table_v.shape == (rows, cols) and rows % MLA_SIDE_ROWS == 0 and MLA_SIDE_ROWS % SIDE_BLOCK == 0
    side_steps = rows // MLA_SIDE_ROWS
    assert side_steps <= len(pairs)
    side = lambda: pl.BlockSpec((MLA_SIDE_ROWS, cols), lambda s, qi, ki: (jnp.minimum(s, side_steps - 1), 0))
    inv = lambda: pl.BlockSpec((MLA_SIDE_ROWS // SIDE_BLOCK, SUBLANES, LANES),
                               lambda s, qi, ki: (jnp.minimum(s, side_steps - 1), 0, 0))
    grid_spec = pltpu.PrefetchScalarGridSpec(
        num_scalar_prefetch=2,
        grid=(len(pairs),),
        in_specs=[pl.BlockSpec((tq, MLA_HEADS * QK_PAD), lambda s, qi, ki: (qi[s], 0)),
                  pl.BlockSpec((tk, MLA_HEADS * QK_PAD), lambda s, qi, ki: (ki[s], 0)),
                  pl.BlockSpec((MLA_WIDTH, tk), lambda s, qi, ki: (0, ki[s])), side(), side()],
        out_specs=[pl.BlockSpec((tq, MLA_WIDTH), lambda s, qi, ki: (qi[s], 0)), side(), inv(), side(), inv()],
        scratch_shapes=[pltpu.VMEM((MLA_HEADS, 1, tq), F32), pltpu.VMEM((MLA_HEADS, 1, tq), F32),
                        pltpu.VMEM((MLA_HEADS, V_DIM, tq), F32)],
    )
    table8 = jax.ShapeDtypeStruct((rows, cols), FP8)
    inv_shape = jax.ShapeDtypeStruct((rows // SIDE_BLOCK, SUBLANES, LANES), F32)
    o, u8, u_inv, v8, v_inv = pl.pallas_call(
        functools.partial(_mla_prompt_kernel, side_steps=side_steps),
        grid_spec=grid_spec,
        out_shape=[jax.ShapeDtypeStruct((n_tokens, MLA_WIDTH), BF16), table8, inv_shape, table8, inv_shape],
        compiler_params=pltpu.CompilerParams(dimension_semantics=("arbitrary",),
                                             vmem_limit_bytes=VMEM_LIMIT_BYTES),
        name="mla_prompt",
    )(qi, ki, q, k, vt, table_u, table_v)
    return o, (u8, u_inv), (v8, v_inv)


def _mla_sample_kernel(q_ref, kn_ref, vn_ref, lat_ref, krc_ref, wukv_ref, o_ref):
    kvc = jnp.dot(lat_ref[0].astype(BF16), wukv_ref[...], preferred_element_type=F32)
    krc = krc_ref[0]
    for h in range(MLA_HEADS):
        q = q_ref[:, h * QK_PAD:(h + 1) * QK_PAD]
        knc = kvc[:, h * NOPE_DIM:(h + 1) * NOPE_DIM].astype(BF16)
        vc = kvc[:, MLA_HEADS * NOPE_DIM + h * V_DIM:MLA_HEADS * NOPE_DIM + (h + 1) * V_DIM].astype(BF16)
        s_c = (lax.dot_general(q[:, :NOPE_DIM], knc, _NT, preferred_element_type=F32)
               + lax.dot_general(q[:, NOPE_DIM:], krc, _NT, preferred_element_type=F32))
        s_n = lax.dot_general(q, kn_ref[:, h * QK_PAD:(h + 1) * QK_PAD], _NT, preferred_element_type=F32)
        m = jnp.maximum(jnp.max(s_c, axis=1, keepdims=True), jnp.max(s_n, axis=1, keepdims=True))
        p_c = jnp.exp2(s_c - m)
        p_n = jnp.exp2(s_n - m)
        l = jnp.sum(p_c, axis=1, keepdims=True) + jnp.sum(p_n, axis=1, keepdims=True)
        o = (jnp.dot(p_c.astype(BF16), vc, preferred_element_type=F32)
             + jnp.dot(p_n.astype(BF16), vn_ref[:, h * V_DIM:(h + 1) * V_DIM], preferred_element_type=F32))
        o_ref[:, h * V_DIM:(h + 1) * V_DIM] = (o / l).astype(BF16)


def _mla_sample(q, k, v, cache_lat, cache_kr, wukv_r, row0, n_new):
    bs, past, _ = cache_lat.shape
    assert row0 % n_new == 0
    blk0 = row0 // n_new
    krc = jnp.pad(cache_kr, ((0, 0), (0, 0), (0, LANES - ROPE_DIM))).astype(BF16)
    new = lambda w: pl.BlockSpec((n_new, w), lambda b: (blk0 + b, 0))
    return pl.pallas_call(
        _mla_sample_kernel,
        grid=(bs,),
        in_specs=[new(MLA_HEADS * QK_PAD), new(MLA_HEADS * QK_PAD), new(MLA_WIDTH),
                  pl.BlockSpec((1, past, KV_RANK), lambda b: (b, 0, 0)),
                  pl.BlockSpec((1, past, LANES), lambda b: (b, 0, 0)),
                  _const_spec(wukv_r.shape)],
        out_specs=pl.BlockSpec((n_new, MLA_WIDTH), lambda b: (b, 0)),
        out_shape=jax.ShapeDtypeStruct((bs * n_new, MLA_WIDTH), BF16),
        compiler_params=pltpu.CompilerParams(dimension_semantics=("arbitrary",),
                                             vmem_limit_bytes=VMEM_LIMIT_BYTES),
        name="mla_sample",
    )(q, k, v, cache_lat, krc, wukv_r)


GLA_CHUNKS_PER_STEP = 4
GLA_SUB = 16
_TN = (((0,), (0,)), ((), ()))


def _cumsum_rows(x):
    n = x.shape[0]
    row = lax.broadcasted_iota(jnp.int32, x.shape, 0)
    s = 1
    while s < n:
        x = x + jnp.where(row >= s, pltpu.roll(x, s, axis=0), 0.0)
        s *= 2
    return x


def _gla_kernel(q_ref, k_ref, v_ref, lf_ref, gate_ref, s0_ref, g_ref, o_ref, send_ref, st_ref, *, n_chunks):
    c = pl.program_id(1)
    n_rows = CHUNK
    n_sub = n_rows // GLA_SUB

    @pl.when(c == 0)
    def _():
        for h in range(GLA_HEADS):
            st_ref[h] = s0_ref[0, h].T

    sub_row = lax.broadcasted_iota(jnp.int32, (GLA_SUB, GLA_DK), 0)
    for h, ci in [(h, ci) for h in range(GLA_HEADS) for ci in range(q_ref.shape[0] // CHUNK)]:
        rows = slice(ci * CHUNK, (ci + 1) * CHUNK)
        ks = slice(h * GLA_DK, (h + 1) * GLA_DK)
        vs = slice(h * GLA_DV, (h + 1) * GLA_DV)
        q = q_ref[rows, ks] * (GLA_DK ** -0.5)
        k = k_ref[rows, ks]
        v = v_ref[rows, vs]
        vb = v.astype(BF16)
        b = _cumsum_rows(lf_ref[rows, ks])
        b_last = b[n_rows - 1:n_rows]
        st = st_ref[h]
        o_inter = lax.dot_general((q * jnp.exp(b)).astype(BF16), st.astype(BF16), _NT,
                                  preferred_element_type=F32)
        outs = []
        for i in range(n_sub):
            r0 = i * GLA_SUB
            bi = b[r0:r0 + GLA_SUB]
            qi = q[r0:r0 + GLA_SUB]
            o_i = o_inter[r0:r0 + GLA_SUB]
            if i > 0:
                b_ref = b[r0 - 1:r0]
                qh = (qi * jnp.exp(bi - b_ref)).astype(BF16)
                kh = (k[:r0] * jnp.exp(b_ref - b[:r0])).astype(BF16)
                a_off = lax.dot_general(qh, kh, _NT, preferred_element_type=F32)
                o_i = o_i + jnp.dot(a_off.astype(BF16), vb[:r0], preferred_element_type=F32)
            for j in range(GLA_SUB):
                r = r0 + j
                decay = jnp.exp(jnp.where(sub_row >= j, bi - b[r:r + 1], -jnp.inf))
                a_col = jnp.sum(qi * k[r:r + 1] * decay, axis=1, keepdims=True)
                o_i = o_i + a_col * v[r:r + 1]
            outs.append(o_i)
        o = jnp.concatenate(outs, axis=0)

        kd = (k * jnp.exp(b_last - b)).astype(BF16)
        st_new = st * jnp.exp(b_last) + lax.dot_general(vb, kd, _TN, preferred_element_type=F32)
        st_ref[h] = st_new

        gate = gate_ref[rows, vs]
        on = _rms(o, g_ref[...])
        o_ref[rows, vs] = (on * (gate / (1.0 + jnp.exp(-gate)))).astype(BF16)

    @pl.when(c == n_chunks - 1)
    def _():
        for h in range(GLA_HEADS):
            send_ref[0, h] = st_ref[h].T


def _gla(gq, gk, gv, lf, gate, s0, gla_norm_g, row0, n_seq, n_chunks):
    per_step = GLA_CHUNKS_PER_STEP if n_chunks % GLA_CHUNKS_PER_STEP == 0 else 1
    tr = per_step * CHUNK
    assert row0 % tr == 0
    blk0 = row0 // tr
    n_steps = n_chunks // per_step
    row = lambda w: pl.BlockSpec((tr, w), lambda b, c: (blk0 + b * n_steps + c, 0))
    state = pl.BlockSpec((1, GLA_HEADS, GLA_DK, GLA_DV), lambda b, c: (b, 0, 0, 0))
    return pl.pallas_call(
        functools.partial(_gla_kernel, n_chunks=n_steps),
        grid=(n_seq, n_steps),
        in_specs=[row(GLA_KW), row(GLA_KW), row(GLA_WIDTH), row(GLA_KW), row(GLA_WIDTH), state,
                  pl.BlockSpec((1, GLA_DV), lambda b, c: (0, 0))],
        out_specs=[pl.BlockSpec((tr, GLA_WIDTH), lambda b, c: (b * n_steps + c, 0)), state],
        out_shape=[jax.ShapeDtypeStruct((n_seq * n_chunks * CHUNK, GLA_WIDTH), BF16),
                   jax.ShapeDtypeStruct((n_seq, GLA_HEADS, GLA_DK, GLA_DV), F32)],
        scratch_shapes=[pltpu.VMEM((GLA_HEADS, GLA_DV, GLA_DK), F32)],
        compiler_params=pltpu.CompilerParams(dimension_semantics=("arbitrary", "arbitrary"),
                                             vmem_limit_bytes=VMEM_LIMIT_BYTES),
        name="gla",
    )(gq, gk, gv, lf, gate, s0, gla_norm_g.reshape(1, -1))


OUT_TM = 256


def _layer_norm(y, g, b):
    mu = jnp.mean(y, axis=-1, keepdims=True)
    yc = y - mu
    var = jnp.mean(yc * yc, axis=-1, keepdims=True)
    return yc * lax.rsqrt(var + LN_EPS) * g + b


def _outproj_kernel(mlap_ref, mlas_ref, glap_ref, glas_ref, xp_ref, xs_ref, wo_ref, g_ref, b_ref, wq_ref,
                    h_ref, h8_ref, hinv_ref, qp_ref, *, n_prompt_tiles):
    in_prompt = pl.program_id(0) < n_prompt_tiles
    mla = jnp.where(in_prompt, mlap_ref[...], mlas_ref[...])
    gla = jnp.where(in_prompt, glap_ref[...], glas_ref[...])
    x = jnp.where(in_prompt, xp_ref[...], xs_ref[...])
    mix = (jnp.dot(mla, wo_ref[:MLA_WIDTH], preferred_element_type=F32)
           + jnp.dot(gla, wo_ref[MLA_WIDTH:], preferred_element_type=F32))
    h = _layer_norm(DN_ALPHA * x + mix, g_ref[...], b_ref[...])
    h_ref[...] = h
    amax = jnp.maximum(jnp.max(jnp.abs(h), axis=1, keepdims=True), FP8_TINY)
    h8_ref[...] = _fp8_scaled(h, amax)
    hinv_ref[...] = jnp.broadcast_to(amax * (1.0 / FP8_TARGET), hinv_ref.shape)
    qp = jnp.dot(h.astype(BF16), wq_ref[...], preferred_element_type=F32).astype(BF16)
    per_head = qp_ref.shape[2]
    for head in range(qp_ref.shape[0]):
        qp_ref[head] = qp[:, head * per_head:(head + 1) * per_head]


def _outproj(mla_p, mla_s, gla_p, gla_s, xp, xs, w_o_b, ln1_g, ln1_b, peer_wq_b):
    tp, ts = xp.shape[0], xs.shape[0]
    t = tp + ts
    tm = OUT_TM
    assert tp % tm == 0 and ts % tm == 0
    n_p = tp // tm
    row = lambda w: pl.BlockSpec((tm, w), lambda i: (i, 0))
    return pl.pallas_call(
        functools.partial(_outproj_kernel, n_prompt_tiles=n_p),
        grid=(t // tm,),
        in_specs=[*_two_stream_specs(tm, MLA_WIDTH, n_p), *_two_stream_specs(tm, GLA_WIDTH, n_p),
                  *_two_stream_specs(tm, D_MODEL, n_p), _const_spec(w_o_b.shape),
                  _const_spec((1, D_MODEL)), _const_spec((1, D_MODEL)), _const_spec(peer_wq_b.shape)],
        out_specs=[row(D_MODEL), row(D_MODEL), row(LANES),
                   pl.BlockSpec((PEER_HEADS, tm, 2 * PEER_HALF), lambda i: (0, i, 0))],
        out_shape=[jax.ShapeDtypeStruct((t, D_MODEL), F32), jax.ShapeDtypeStruct((t, D_MODEL), FP8),
                   jax.ShapeDtypeStruct((t, LANES), F32),
                   jax.ShapeDtypeStruct((PEER_HEADS, t, 2 * PEER_HALF), BF16)],
        compiler_params=pltpu.CompilerParams(dimension_semantics=("arbitrary",),
                                             vmem_limit_bytes=VMEM_LIMIT_BYTES),
        name="outproj",
    )(mla_p, mla_s, gla_p, gla_s, xp, xs, w_o_b, ln1_g.reshape(1, -1), ln1_b.reshape(1, -1), peer_wq_b)


PEER_HEADS = 8
N_KEYS = 128
PEER_HALF = 128
PEER_TOPK = 16
CAND_WIDE_RANKS = 8
CAND_ROWS = PEER_TOPK + (CAND_WIDE_RANKS - 1) * SUBLANES + (PEER_TOPK - CAND_WIDE_RANKS)


def _first_max(x, row):
    n = x.shape[0]
    blocks = [(x[r:r + SUBLANES], row[r:r + SUBLANES]) for r in range(0, n, SUBLANES)]
    while len(blocks) > 1:
        merged = []
        for j in range(0, len(blocks) - 1, 2):
            (va, ia), (vb, ib) = blocks[j], blocks[j + 1]
            take = va >= vb
            merged.append((jnp.where(take, va, vb), jnp.where(take, ia, ib)))
        if len(blocks) % 2:
            merged.append(blocks[-1])
        blocks = merged
    v, i = blocks[0]
    m = jnp.max(v, axis=0, keepdims=True)
    pick = jnp.min(jnp.where(v == m, i, float(n)), axis=0, keepdims=True)
    return m, pick


def _extract_top(xs, row, n_out, on_pick):
    xs = list(xs)
    for k in range(n_out):
        for p, x in enumerate(xs):
            m, pick = _first_max(x, row)
            hit = row == pick
            on_pick(p, k, m, hit, pick)
            xs[p] = jnp.where(hit, -jnp.inf, x)


def _route_head(qp_ref, keys_ref, row_ref, sv_ref, si_ref, gt_ref, et_ref, h):
    tm = qp_ref.shape[1]
    q = qp_ref[h]
    scores = [lax.dot_general(keys_ref[half], q[:, half * PEER_HALF:(half + 1) * PEER_HALF], _NT,
                              preferred_element_type=F32) for half in range(2)]

    def keep(half, k, m, hit, pick):
        sv_ref[half, k:k + 1, :] = m
        si_ref[half, k:k + 1, :] = pick

    _extract_top(scores, row_ref[...], PEER_TOPK, keep)

    s1, s2 = sv_ref[0], sv_ref[1]
    i1, i2 = si_ref[0], si_ref[1]
    bc = lambda r, n: jnp.broadcast_to(r, (n, tm))
    wide = range(1, CAND_WIDE_RANKS)
    cand = jnp.concatenate([bc(s1[0:1], PEER_TOPK) + s2]
                           + [bc(s1[a:a + 1], SUBLANES) + s2[:SUBLANES] for a in wide]
                           + [s1[CAND_WIDE_RANKS:] + bc(s2[0:1], PEER_TOPK - CAND_WIDE_RANKS)], axis=0)
    i1 = i1 * N_KEYS
    c_expert = jnp.concatenate([bc(i1[0:1], PEER_TOPK) + i2]
                               + [bc(i1[a:a + 1], SUBLANES) + i2[:SUBLANES] for a in wide]
                               + [i1[CAND_WIDE_RANKS:] + bc(i2[0:1], PEER_TOPK - CAND_WIDE_RANKS)], axis=0)
    base = h * PEER_TOPK

    def keep_pair(_, k, m, hit, pick):
        gt_ref[base + k:base + k + 1, :] = m
        et_ref[base + k:base + k + 1, :] = jnp.sum(jnp.where(hit, c_expert, 0.0), axis=0, keepdims=True)

    _extract_top([cand], row_ref[:CAND_ROWS], PEER_TOPK, keep_pair)

    sc = gt_ref[base:base + PEER_TOPK, :]
    e = jnp.exp(sc - sc[0:1])
    gt_ref[base:base + PEER_TOPK, :] = e / jnp.sum(e, axis=0, keepdims=True)


RS_TM = 128
W_ROW_TILE = SUBLANES


def _w_table_tokens(t):
    return -(-t // EXP_TM) * EXP_TM


def _route_scatter_kernel(qp_ref, keys_ref, row_ref, w_ref, sv_ref, si_ref, gt_ref, et_ref, sel_ref, *, n_tiles):
    i = pl.program_id(0)
    tm = qp_ref.shape[1]

    @pl.when(i == 0)
    def _():
        sel_ref[...] = jnp.zeros(sel_ref.shape, F32)

    @pl.when(i > n_tiles)
    def _():
        w_ref[...] = jnp.zeros(w_ref.shape, F32)

    @pl.when(i <= n_tiles)
    def _():
        cur = i % 2
        prev = 1 - cur
        key = lax.broadcasted_iota(jnp.int32, (N_KEYS, N_KEYS), 0).astype(F32)
        zero = jnp.zeros(key.shape, BF16)
        per_head = tm // PEER_HEADS
        for h in range(PEER_HEADS):
            _route_head(qp_ref, keys_ref, row_ref, sv_ref, si_ref, gt_ref, et_ref, h)
            t0 = h * per_head
            g8 = sel_ref[prev, 0, t0:t0 + per_head, :]
            i18 = sel_ref[prev, 1, t0:t0 + per_head, :]
            i28 = sel_ref[prev, 2, t0:t0 + per_head, :]
            for u in range(0, per_head, 2):
                a_ts, b_ts = [], []
                for v in (u, u + 1):
                    g = jnp.broadcast_to(g8[v:v + 1], key.shape)
                    i1 = jnp.broadcast_to(i18[v:v + 1], key.shape)
                    i2 = jnp.broadcast_to(i28[v:v + 1], key.shape)
                    a_ts.append(jnp.where(key == i1, g, 0.0).astype(BF16))
                    b_ts.append(jnp.where(key == i2, 1.0, 0.0).astype(BF16))
                a_pair = jnp.concatenate(a_ts, axis=1)
                b_pair = jnp.concatenate([jnp.concatenate([b_ts[0], zero], axis=1),
                                          jnp.concatenate([zero, b_ts[1]], axis=1)], axis=0)
                w = lax.dot_general(a_pair, b_pair, _NT, preferred_element_type=F32)
                for n, v in enumerate((u, u + 1)):
                    w_ref[:, t0 + v, :, :] = w[:, n * N_KEYS:(n + 1) * N_KEYS].reshape(
                        N_KEYS // W_ROW_TILE, W_ROW_TILE, N_KEYS)
        sel_ref[cur, 0] = gt_ref[...].T
        expert = et_ref[...]
        key0 = jnp.floor(expert * (1.0 / N_KEYS))
        sel_ref[cur, 1] = key0.T
        sel_ref[cur, 2] = (expert - key0 * N_KEYS).T


def _route_scatter(qp, keys_b):
    t = qp.shape[1]
    tm = RS_TM
    t_pad = _w_table_tokens(t)
    assert t % tm == 0 and t_pad % tm == 0
    n_tiles = t // tm
    n_sel = PEER_HEADS * PEER_TOPK
    n_grp = N_KEYS // W_ROW_TILE
    row_index = jnp.broadcast_to(jnp.arange(N_KEYS, dtype=F32)[:, None], (N_KEYS, tm))
    return pl.pallas_call(
        functools.partial(_route_scatter_kernel, n_tiles=n_tiles),
        grid=(t_pad // tm + 1,),
        in_specs=[pl.BlockSpec((PEER_HEADS, tm, 2 * PEER_HALF), lambda i: (0, jnp.minimum(i, n_tiles - 1), 0)),
                  pl.BlockSpec(keys_b.shape, lambda i: (0, 0, 0)),
                  pl.BlockSpec((N_KEYS, tm), lambda i: (0, 0))],
        out_specs=pl.BlockSpec((n_grp, tm, W_ROW_TILE, N_KEYS), lambda i: (0, jnp.maximum(i - 1, 0), 0, 0)),
        out_shape=jax.ShapeDtypeStruct((n_grp, t_pad, W_ROW_TILE, N_KEYS), F32),
        scratch_shapes=[pltpu.VMEM((2, PEER_TOPK, tm), F32), pltpu.VMEM((2, PEER_TOPK, tm), F32),
                        pltpu.VMEM((n_sel, tm), F32), pltpu.VMEM((n_sel, tm), F32),
                        pltpu.VMEM((2, 3, tm, n_sel), F32)],
        compiler_params=pltpu.CompilerParams(dimension_semantics=("arbitrary",),
                                             vmem_limit_bytes=VMEM_LIMIT_BYTES),
        name="peer_route_scatter",
    )(qp, keys_b, row_index)


EXP_TM = 1024
EXP_ROWS = W_ROW_TILE
EXP_TE = EXP_ROWS * N_KEYS
EXP_SUBTILES = 4
GELU_C = 2.0 ** -0.5


def _experts_kernel(h8_ref, hinv_ref, w_ref, u_ref, uinv_ref, v_ref, vinv_ref, h_ref, g_ref, b_ref, y_ref):
    j = pl.program_id(1)
    tm = h8_ref.shape[0]

    @pl.when(j == 0)
    def _():
        y_ref[...] = DN_ALPHA * h_ref[...]

    sub = tm // EXP_SUBTILES
    u_inv = jnp.concatenate([uinv_ref[r, 0:1, :] for r in range(EXP_ROWS)], axis=1)
    half_v_inv = 0.5 * jnp.concatenate([vinv_ref[r, 0:1, :] for r in range(EXP_ROWS)], axis=1)
    acts = []
    for s in range(EXP_SUBTILES):
        h_inv = hinv_ref[s * sub:(s + 1) * sub, :]
        a8 = lax.dot_general(h8_ref[s * sub:(s + 1) * sub, :], u_ref[...], _NT, preferred_element_type=F32)
        acts.append(a8 * jnp.concatenate([h_inv] * EXP_ROWS, axis=1) * u_inv)
    for s in range(EXP_SUBTILES):
        w = jnp.concatenate([w_ref[pl.ds(s * sub * W_ROW_TILE + r, sub, stride=W_ROW_TILE), :]
                             for r in range(EXP_ROWS)], axis=1)
        a = acts[s]
        p = w * (a * (1.0 + lax.erf(a * GELU_C))) * half_v_inv
        pmax = jnp.maximum(jnp.max(jnp.abs(p), axis=1, keepdims=True), FP8_TINY)
        out = jnp.dot(_fp8_scaled(p, pmax), v_ref[...], preferred_element_type=F32)
        y_ref[s * sub:(s + 1) * sub, :] += out * (pmax * (1.0 / FP8_TARGET))

    @pl.when(j == pl.num_programs(1) - 1)
    def _():
        y_ref[...] = _layer_norm(y_ref[...], g_ref[...], b_ref[...])


def _experts(h8, h_inv, h, w_table, u8, u_inv, v8, v_inv, ln2_g, ln2_b, row0, n_rows):
    tm = min(EXP_TM, n_rows)
    assert n_rows % tm == 0 and row0 % tm == 0 and tm % (EXP_SUBTILES * FP8_ROW_TILE) == 0
    nt = n_rows // tm
    blk0 = row0 // tm
    ne = u8.shape[0] // EXP_TE
    tiles_per_group = w_table.shape[1] // tm
    w2d = w_table.reshape(-1, N_KEYS)
    once = lambda w: pl.BlockSpec((tm, w), lambda i, j: (blk0 + i, 0), pipeline_mode=pl.Buffered(1))
    return pl.pallas_call(
        _experts_kernel,
        grid=(nt, ne),
        in_specs=[once(D_MODEL), once(LANES),
                  pl.BlockSpec((tm * W_ROW_TILE, N_KEYS), lambda i, j: (j * tiles_per_group + blk0 + i, 0)),
                  pl.BlockSpec((EXP_TE, D_MODEL), lambda i, j: (j, 0)),
                  pl.BlockSpec((EXP_ROWS, SUBLANES, LANES), lambda i, j: (j, 0, 0)),
                  pl.BlockSpec((EXP_TE, D_MODEL), lambda i, j: (j, 0)),
                  pl.BlockSpec((EXP_ROWS, SUBLANES, LANES), lambda i, j: (j, 0, 0)),
                  once(D_MODEL), pl.BlockSpec((1, D_MODEL), lambda i, j: (0, 0)),
                  pl.BlockSpec((1, D_MODEL), lambda i, j: (0, 0))],
        out_specs=pl.BlockSpec((tm, D_MODEL), lambda i, j: (i, 0)),
        out_shape=jax.ShapeDtypeStruct((n_rows, D_MODEL), F32),
        compiler_params=pltpu.CompilerParams(dimension_semantics=("arbitrary", "arbitrary"),
                                             vmem_limit_bytes=VMEM_LIMIT_BYTES),
        name="peer_experts",
    )(h8, h_inv, w2d, u8, u_inv, v8, v_inv, h, ln2_g.reshape(1, -1), ln2_b.reshape(1, -1))


def kernel(x_prompt, x_sample, cache_kv_latent, cache_k_rope, state_gla, w_in, q_norm_g, w_uq, kv_norm_g, w_ukv,
           w_gk2, b_gk, gla_norm_g, w_o, ln1_g, ln1_b, peer_wq, peer_keys, peer_u, peer_v, ln2_g, ln2_b):
    bp, sp, _ = x_prompt.shape
    bs, ss, _ = x_sample.shape
    past = cache_kv_latent.shape[1]
    tp, ts = bp * sp, bs * ss
    xp = x_prompt.reshape(tp, D_MODEL)
    xs = x_sample.reshape(ts, D_MODEL)
    cos_p, sin_p = _rope_tables(jnp.arange(sp, dtype=jnp.int32))
    cos_s, sin_s = _rope_tables(past + jnp.arange(ss, dtype=jnp.int32))
    cos = jnp.concatenate([jnp.tile(cos_p, (bp, 1)), jnp.tile(cos_s, (bs, 1))])
    sin = jnp.concatenate([jnp.tile(sin_p, (bp, 1)), jnp.tile(sin_s, (bs, 1))])
    w_in_r, wq_r, wukv_r, wvt_r, wgk_r = _mixer_weights(w_in, w_uq, w_ukv, w_gk2)
    q, k, v, vt, ckv_p, ckv_s, kr_p, kr_s, gq, gk, gv, gate, lf = _mixer(
        xp, xs, cos, sin, w_in_r, wq_r, wukv_r, wvt_r, wgk_r, q_norm_g, kv_norm_g, b_gk)
    o_p, (u8, u_inv), (v8, v_inv) = _mla_prompt(q, k, vt, tp, peer_u, peer_v)
    o_s = _mla_sample(q, k, v, cache_kv_latent, cache_k_rope, wukv_r, tp, ss)
    g_p, st_p = _gla(gq, gk, gv, lf, gate, jnp.zeros((bp,) + state_gla.shape[1:], F32), gla_norm_g,
                     0, bp, sp // CHUNK)
    g_s, st_s = _gla(gq, gk, gv, lf, gate, state_gla, gla_norm_g, tp, bs, ss // CHUNK)

    h, h8, h_inv, qp = _outproj(o_p, o_s, g_p, g_s, xp, xs, w_o.astype(BF16), ln1_g, ln1_b, peer_wq.astype(BF16))
    w_table = _route_scatter(qp, peer_keys.astype(BF16))
    y_p = _experts(h8, h_inv, h, w_table, u8, u_inv, v8, v_inv, ln2_g, ln2_b, 0, tp)
    y_s = _experts(h8, h_inv, h, w_table, u8, u_inv, v8, v_inv, ln2_g, ln2_b, tp, ts)

    dt = x_prompt.dtype
    return (y_p.reshape(bp, sp, D_MODEL), y_s.reshape(bs, ss, D_MODEL),
            ckv_p.reshape(bp, sp, KV_RANK), kr_p.reshape(bp, sp, ROPE_DIM), st_p.astype(dt),
            ckv_s.reshape(bs, ss, KV_RANK), kr_s.reshape(bs, ss, ROPE_DIM), st_s.astype(dt))
```
